```python
import math
import jax, jax.numpy as jnp
from jax import lax
import numpy as np

D_MODEL = 1024
BATCH = 8
SEQ = 2048
DEPTH = 2
DEC_BATCH = 128
DEC_SEQ = 8
PAST_LEN = 2048
PAGE_SIZE = 128

D_MIX = D_MODEL
H_DIFF = 4
QK_HEAD_DIM = D_MIX // 16
V_HEAD_DIM = 2 * QK_HEAD_DIM
DIFF_WIDTH = H_DIFF * V_HEAD_DIM
A_GROUPS = 4
A_WIDTH = D_MIX - DIFF_WIDTH
A_GROUP_DIM = A_WIDTH // A_GROUPS
CHUNK = 128
Q_COLS = H_DIFF * 2 * QK_HEAD_DIM
K_COLS = Q_COLS
V_COLS = DIFF_WIDTH
IN_COLS = Q_COLS + K_COLS + V_COLS + 2 * A_WIDTH
D_FF = 11 * D_MODEL // 4
CONV_W = 3
REL_BUCKETS = 32
REL_MAX_EXACT = REL_BUCKETS // 2
REL_MAX_DIST = 128
Q_BLOCK = 128
NORM_EPS = 1e-6
SUBLN_EPS = 1e-5
NEG_INF = -1e30

kernel_name = "hymba_diffattn_chunkmlp_convffn_step"


def rmsnorm(x, g, eps=NORM_EPS):
    xf = x.astype(jnp.float32)
    y = xf * lax.rsqrt(jnp.mean(xf * xf, axis=-1, keepdims=True) + eps)
    return (y * g.astype(jnp.float32)).astype(x.dtype)


def layernorm(x, g, b, eps=NORM_EPS):
    xf = x.astype(jnp.float32)
    xc = xf - jnp.mean(xf, axis=-1, keepdims=True)
    y = xc * lax.rsqrt(jnp.mean(xc * xc, axis=-1, keepdims=True) + eps)
    return (y * g.astype(jnp.float32) + b.astype(jnp.float32)).astype(x.dtype)


def t5_bucket(dist):
    n = jnp.maximum(dist, 0)
    nf = jnp.maximum(n, 1).astype(jnp.float32)
    large = REL_MAX_EXACT + (jnp.log(nf / REL_MAX_EXACT) / math.log(REL_MAX_DIST / REL_MAX_EXACT)
                             * (REL_BUCKETS - REL_MAX_EXACT)).astype(jnp.int32)
    large = jnp.minimum(large, REL_BUCKETS - 1)
    return jnp.where(n < REL_MAX_EXACT, n, large)


def lambda_init(layer):
    return 0.8 - 0.6 * math.exp(-0.3 * layer)


def diff_lambda(lq1, lk1, lq2, lk2, lam_init):
    f = jnp.float32
    return (jnp.exp(jnp.sum(lq1.astype(f) * lk1.astype(f)))
            - jnp.exp(jnp.sum(lq2.astype(f) * lk2.astype(f))) + lam_init)


def project(xn, w_in):
    lead = xn.shape[:-1]
    z = xn @ w_in
    q, k, v, a = jnp.split(z, [Q_COLS, Q_COLS + K_COLS, Q_COLS + K_COLS + V_COLS], axis=-1)
    q = q.reshape(*lead, H_DIFF, 2, QK_HEAD_DIM)
    k = k.reshape(*lead, H_DIFF, 2 * QK_HEAD_DIM)
    v = v.reshape(*lead, H_DIFF, V_HEAD_DIM)
    u_a, v_a = jnp.split(jax.nn.gelu(a), 2, axis=-1)
    return q, k, v, u_a, v_a


def diff_attn_core(q, k, v, qpos, kpos, rel_bias, lam):
    B, Tk = k.shape[:2]
    k2 = k.reshape(B, Tk, H_DIFF, 2, QK_HEAD_DIM)
    s = jnp.einsum('bqhmd,bkhmd->bhmqk', q, k2).astype(jnp.float32) * (QK_HEAD_DIM ** -0.5)
    dist = qpos[:, None] - kpos[None, :]
    bias = jnp.transpose(rel_bias.astype(jnp.float32)[t5_bucket(dist)], (2, 0, 1))
    s = jnp.where(dist >= 0, s + bias[None, :, None], NEG_INF)
    p = jax.nn.softmax(s, axis=-1)
    a = p[:, :, 0] - lam * p[:, :, 1]
    return jnp.einsum('bhqk,bkhd->bqhd', a.astype(v.dtype), v)


def diff_attn_prompt(q, k, v, rel_bias, lam):
    B, S = q.shape[:2]
    nblk = S // Q_BLOCK
    qb = jnp.moveaxis(q.reshape(B, nblk, Q_BLOCK, H_DIFF, 2, QK_HEAD_DIM), 1, 0)
    kpos = jnp.arange(S)

    def one_block(args):
        q_blk, bi = args
        qpos = bi * Q_BLOCK + jnp.arange(Q_BLOCK)
        return diff_attn_core(q_blk, k, v, qpos, kpos, rel_bias, lam)

    out = lax.map(one_block, (qb, jnp.arange(nblk)))
    return jnp.moveaxis(out, 0, 1).reshape(B, S, H_DIFF, V_HEAD_DIM)


def diff_head_out(o, subln_gain, lam_init):
    B, L = o.shape[:2]
    o = rmsnorm(o, subln_gain, SUBLN_EPS) * (1.0 - lam_init)
    return o.reshape(B, L, DIFF_WIDTH)


def spatial_gate(u_a, v_a, ln_g, ln_b, w_s, b_s, t_len):
    B, L = v_a.shape[:2]
    vn = layernorm(v_a, ln_g, ln_b)
    vr = vn.reshape(B, L // t_len, t_len, A_GROUPS, A_GROUP_DIM)
    w = jnp.tril(w_s[:, :t_len, :t_len])
    s = jnp.einsum('gts,bnsgc->bntgc', w, vr) + b_s[:, :t_len].T[None, None, :, :, None]
    return u_a * s.reshape(B, L, A_WIDTH), vn


def conv_ffn(xn, buf, w_up, conv_w, conv_b, w_down):
    L = xn.shape[1]
    h = xn @ w_up
    hp = jnp.concatenate([buf.astype(h.dtype), h], axis=1)
    c = conv_b
    for j in range(CONV_W):
        c = c + conv_w[j] * hp[:, j:j + L]
    g, u = jnp.split(c, 2, axis=-1)
    return (jax.nn.silu(g) * u) @ w_down, hp[:, L:]


def setup_inputs(seed: int = 0) -> dict:
    key = jax.random.key(seed)
    ks = jax.random.split(key, 32)
    f = jnp.float32
    n_pages = PAST_LEN // PAGE_SIZE
    n_pool = (DEC_BATCH * n_pages * 5) // 4

    def nrm(k, shape, scale=1.0):
        return jax.random.normal(k, shape, f) * scale

    def gain(k, shape):
        return 1.0 + 0.05 * jax.random.normal(k, shape, f)

    page_table = jax.random.permutation(ks[5], n_pool)[:DEC_BATCH * n_pages]
    page_table = page_table.reshape(DEC_BATCH, n_pages).astype(jnp.int32)
    return {
        "x_prompt": nrm(ks[0], (BATCH, SEQ, D_MODEL)),
        "x_sample": nrm(ks[1], (DEC_BATCH, DEC_SEQ, D_MODEL)),
        "cache_k": nrm(ks[2], (DEPTH, n_pool, PAGE_SIZE, H_DIFF, 2 * QK_HEAD_DIM)),
        "cache_v": nrm(ks[3], (DEPTH, n_pool, PAGE_SIZE, H_DIFF, V_HEAD_DIM)),
        "state_conv": nrm(ks[4], (DEPTH, DEC_BATCH, CONV_W - 1, 2 * D_FF)),
        "page_table": page_table,
        "rel_bias": nrm(ks[6], (REL_BUCKETS, H_DIFF), 0.5),
        "norm_mix_pre": gain(ks[7], (DEPTH, D_MODEL)),
        "norm_mix_post": gain(ks[8], (DEPTH, D_MODEL)),
        "norm_ffn_pre": gain(ks[9], (DEPTH, D_MODEL)),
        "norm_ffn_post": gain(ks[10], (DEPTH, D_MODEL)),
        "w_in": nrm(ks[11], (DEPTH, D_MODEL, IN_COLS), D_MODEL ** -0.5),
        "lambda_q1": nrm(ks[12], (DEPTH, QK_HEAD_DIM), 0.1),
        "lambda_k1": nrm(ks[13], (DEPTH, QK_HEAD_DIM), 0.1),
        "lambda_q2": nrm(ks[14], (DEPTH, QK_HEAD_DIM), 0.1),
        "lambda_k2": nrm(ks[15], (DEPTH, QK_HEAD_DIM), 0.1),
        "subln_gain": gain(ks[16], (DEPTH, V_HEAD_DIM)),
        "sgu_ln_gain": gain(ks[17], (DEPTH, A_WIDTH)),
        "sgu_ln_bias": nrm(ks[18], (DEPTH, A_WIDTH), 0.02),
        "w_spatial": nrm(ks[19], (DEPTH, A_GROUPS, CHUNK, CHUNK), CHUNK ** -0.5),
        "b_spatial": 1.0 + nrm(ks[20], (DEPTH, A_GROUPS, CHUNK), 0.02),
        "w_out": nrm(ks[21], (DEPTH, D_MIX, D_MODEL), D_MIX ** -0.5),
        "w_up": nrm(ks[22], (DEPTH, D_MODEL, 2 * D_FF), D_MODEL ** -0.5),
        "conv_w": nrm(ks[23], (DEPTH, CONV_W, 2 * D_FF), CONV_W ** -0.5),
        "conv_b": nrm(ks[24], (DEPTH, 2 * D_FF), 0.02),
        "w_down": nrm(ks[25], (DEPTH, D_FF, D_MODEL), D_FF ** -0.5),
    }


def reference(x_prompt, x_sample, cache_k, cache_v, state_conv, page_table, rel_bias,
              norm_mix_pre, norm_mix_post, norm_ffn_pre, norm_ffn_post, w_in,
              lambda_q1, lambda_k1, lambda_q2, lambda_k2, subln_gain, sgu_ln_gain, sgu_ln_bias,
              w_spatial, b_spatial, w_out, w_up, conv_w, conv_b, w_down):
    n_pages = page_table.shape[1]
    past_len = n_pages * PAGE_SIZE
    dec_batch, dec_seq = x_sample.shape[:2]

    def layer(x, l, attend, conv_buf, t_len):
        lam_init = lambda_init(l)
        lam = diff_lambda(lambda_q1[l], lambda_k1[l], lambda_q2[l], lambda_k2[l], lam_init)
        xn = rmsnorm(x, norm_mix_pre[l])
        q, k, v, u_a, v_a = project(xn, w_in[l])
        o_b = diff_head_out(attend(q, k, v, lam), subln_gain[l], lam_init)
        o_a, vn = spatial_gate(u_a, v_a, sgu_ln_gain[l], sgu_ln_bias[l], w_spatial[l], b_spatial[l], t_len)
        mix = jnp.concatenate([o_b, o_a], axis=-1) @ w_out[l]
        x = x + rmsnorm(mix, norm_mix_post[l])
        f, new_buf = conv_ffn(rmsnorm(x, norm_ffn_pre[l]), conv_buf, w_up[l], conv_w[l], conv_b[l], w_down[l])
        x = x + rmsnorm(f, norm_ffn_post[l])
        return x, k, v, new_buf, vn

    def prompt_attend(q, k, v, lam):
        return diff_attn_prompt(q, k, v, rel_bias, lam)

    def sample_attend(l):
        def attend(q, k, v, lam):
            pk = cache_k[l][page_table].reshape(dec_batch, past_len, H_DIFF, 2 * QK_HEAD_DIM)
            pv = cache_v[l][page_table].reshape(dec_batch, past_len, H_DIFF, V_HEAD_DIM)
            k_all = jnp.concatenate([pk.astype(k.dtype), k], axis=1)
            v_all = jnp.concatenate([pv.astype(v.dtype), v], axis=1)
            qpos = past_len + jnp.arange(dec_seq)
            kpos = jnp.arange(past_len + dec_seq)
            return diff_attn_core(q, k_all, v_all, qpos, kpos, rel_bias, lam)
        return attend

    zero_buf = jnp.zeros((x_prompt.shape[0], CONV_W - 1, 2 * D_FF), x_prompt.dtype)
    y_p, y_s = x_prompt, x_sample
    kp, vp, cp, ksm, vsm, csm, svs = [], [], [], [], [], [], []
    for l in range(DEPTH):
        y_p, k_l, v_l, c_l, _ = layer(y_p, l, prompt_attend, zero_buf, CHUNK)
        kp.append(k_l); vp.append(v_l); cp.append(c_l)
        y_s, k_l, v_l, c_l, vn_l = layer(y_s, l, sample_attend(l), state_conv[l], dec_seq)
        ksm.append(k_l); vsm.append(v_l); csm.append(c_l); svs.append(vn_l)
    return (y_p, y_s, jnp.stack(kp), jnp.stack(vp), jnp.stack(cp),
            jnp.stack(ksm), jnp.stack(vsm), jnp.stack(csm), jnp.stack(svs))
```

```python
import functools
import math

import jax
import jax.numpy as jnp
from jax import lax
from jax.experimental import pallas as pl
from jax.experimental.pallas import tpu as pltpu

F32 = jnp.float32
BF16 = jnp.bfloat16

D_MODEL = 1024
DEPTH = 2
PAGE_SIZE = 128
H_DIFF = 4
QK_HEAD_DIM = 64
V_HEAD_DIM = 128
HEAD_W = 2 * QK_HEAD_DIM
DIFF_WIDTH = H_DIFF * V_HEAD_DIM
A_GROUPS = 4
A_WIDTH = 512
A_GROUP_DIM = 128
CHUNK = 128
Q_COLS = H_DIFF * HEAD_W
IN_COLS = 3 * Q_COLS + 2 * A_WIDTH
D_FF = 2816
CONV_W = 3
REL_BUCKETS = 32
REL_MAX_EXACT = 16
REL_MAX_DIST = 128
NORM_EPS = 1e-6
SUBLN_EPS = 1e-5
NEG_INF = -1e30

V7X_VMEM_LIMIT_BYTES = 56 * 1024 * 1024

TM_IN = 512
TQ = 256
TM_FFN = 256
TF = 256


def _lambda_init(layer):
    return 0.8 - 0.6 * math.exp(-0.3 * layer)


def _t5_bucket(dist):
    n = jnp.maximum(dist, 0)
    nf = jnp.maximum(n, 1).astype(F32)
    large = REL_MAX_EXACT + (jnp.log(nf / REL_MAX_EXACT) / math.log(REL_MAX_DIST / REL_MAX_EXACT)
                             * (REL_BUCKETS - REL_MAX_EXACT)).astype(jnp.int32)
    large = jnp.minimum(large, REL_BUCKETS - 1)
    return jnp.where(n < REL_MAX_EXACT, n, large)


def _bucket_table(dist):
    return jnp.where(dist >= 0, _t5_bucket(dist), -1).astype(jnp.int32)


def _const_spec(shape):
    nd = len(shape)
    return pl.BlockSpec(shape, lambda *_: (0,) * nd, pipeline_mode=pl.Buffered(1))


def _rms(x, gain, eps):
    y = x * lax.rsqrt(jnp.mean(x * x, axis=-1, keepdims=True) + eps)
    return y * gain


def _nt_dot(a, b):
    return lax.dot_general(a, b, (((1,), (1,)), ((), ())), preferred_element_type=F32)


def _diff_lambda(lamv_ref, lam_init):
    p = lamv_ref[...]
    a = jnp.sum(p[0:1] * p[1:2], axis=-1, keepdims=True)
    b = jnp.sum(p[2:3] * p[3:4], axis=-1, keepdims=True)
    return jnp.exp(a) - jnp.exp(b) + lam_init


def _bias_from_buckets(idx, rb_ref, head):
    far = rb_ref[REL_BUCKETS - 1, head]
    tbl = jnp.full(idx.shape, NEG_INF, F32)
    for b in range(REL_BUCKETS):
        tbl = jnp.where(idx == b, rb_ref[b, head] - far, tbl)
    return tbl


def _inproj_kernel(x_ref, g_ref, w_ref, lng_ref, lnb_ref, wsp_ref, bsp_ref, *out_refs,
                   tm, sample):
    if sample:
        q_ref, kf_ref, vf_ref, oa_ref, vn_ref = out_refs
    else:
        q_ref, kf_ref, vf_ref, kb_ref, vb_ref, oa_ref = out_refs

    x = x_ref[...]
    xn = _rms(x, g_ref[...], NORM_EPS).astype(BF16)

    def proj(c0, c1):
        return jnp.dot(xn, w_ref[:, c0:c1], preferred_element_type=F32)

    q = proj(0, Q_COLS) * (QK_HEAD_DIM ** -0.5)
    k = proj(Q_COLS, 2 * Q_COLS)
    v = proj(2 * Q_COLS, 3 * Q_COLS)
    q_ref[...] = q.astype(q_ref.dtype)
    kf_ref[...] = k
    vf_ref[...] = v
    if not sample:
        kb_ref[...] = k.astype(BF16)
        vb_ref[...] = v.astype(BF16)

    a0 = 3 * Q_COLS
    u_a = jax.nn.gelu(proj(a0, a0 + A_WIDTH))
    v_a = jax.nn.gelu(proj(a0 + A_WIDTH, a0 + 2 * A_WIDTH))
    xc = v_a - jnp.mean(v_a, axis=-1, keepdims=True)
    vn = xc * lax.rsqrt(jnp.mean(xc * xc, axis=-1, keepdims=True) + NORM_EPS)
    vn = vn * lng_ref[...] + lnb_ref[...]
    if sample:
        vn_ref[...] = vn

    row = lax.broadcasted_iota(jnp.int32, (CHUNK, CHUNK), 0)
    col = lax.broadcasted_iota(jnp.int32, (CHUNK, CHUNK), 1)
    if sample:
        keep = jnp.logical_and((row >> 3) == (col >> 3), (col & 7) <= (row & 7))
    else:
        keep = col <= row
    vnb = vn.astype(BF16)
    for g in range(A_GROUPS):
        wg = jnp.where(keep, wsp_ref[g], 0.0).astype(BF16)
        bg = bsp_ref[g]
        c0, c1 = g * A_GROUP_DIM, (g + 1) * A_GROUP_DIM
        for c in range(tm // CHUNK):
            r0, r1 = c * CHUNK, (c + 1) * CHUNK
            s = jnp.dot(wg, vnb[r0:r1, c0:c1], preferred_element_type=F32) + bg
            oa_ref[r0:r1, c0:c1] = (u_a[r0:r1, c0:c1] * s).astype(oa_ref.dtype)


def _inproj(x, gain, w_in, ln_g, ln_b, wsp, bsp, *, sample):
    rows = x.shape[0]
    tm = TM_IN
    row_spec = lambda w: pl.BlockSpec((tm, w), lambda i: (i, 0))
    in_specs = [
        row_spec(D_MODEL),
        _const_spec((1, D_MODEL)),
        _const_spec((D_MODEL, IN_COLS)),
        _const_spec((1, A_WIDTH)),
        _const_spec((1, A_WIDTH)),
        _const_spec((A_GROUPS, CHUNK, CHUNK)),
        _const_spec((A_GROUPS, CHUNK, A_GROUP_DIM)),
    ]
    sds = jax.ShapeDtypeStruct
    if sample:
        out_shape = [sds((rows, Q_COLS), F32), sds((rows, Q_COLS), F32), sds((rows, Q_COLS), F32),
                     sds((rows, A_WIDTH), F32), sds((rows, A_WIDTH), F32)]
    else:
        out_shape = [sds((rows, Q_COLS), BF16), sds((rows, Q_COLS), F32), sds((rows, Q_COLS), F32),
                     sds((rows, Q_COLS), BF16), sds((rows, Q_COLS), BF16), sds((rows, A_WIDTH), BF16)]
    out_specs = [row_spec(s.shape[1]) for s in out_shape]
    return pl.pallas_call(
        functools.partial(_inproj_kernel, tm=tm, sample=sample),
        grid=(rows // tm,),
        in_specs=in_specs,
        out_specs=out_specs,
        out_shape=out_shape,
        compiler_params=pltpu.CompilerParams(
            dimension_semantics=("arbitrary",), vmem_limit_bytes=V7X_VMEM_LIMIT_BYTES),
        name="inproj_sample" if sample else "inproj_prompt",
    )(x, gain, w_in, ln_g, ln_b, wsp, bsp)


def _subln(o, gain, lam_init):
    return _rms(o, gain, SUBLN_EPS) * (1.0 - lam_init)


def _prompt_attn_kernel(rb_ref, lamv_ref, idx_ref, sg_ref, q_ref, k_ref, v_ref, o_ref, bias_scr,
                        *, lam_init):
    qi = pl.program_id(1)

    @pl.when(jnp.logical_and(pl.program_id(0) == 0, qi == 0))
    def _():
        for h in range(H_DIFF):
            for t in range(2):
                bias_scr[h, t] = _bias_from_buckets(idx_ref[t], rb_ref, h)

    lam = _diff_lambda(lamv_ref, lam_init)
    lane = lax.broadcasted_iota(jnp.int32, (TQ, HEAD_W), 1)
    first_map = lane < QK_HEAD_DIM

    for h in range(H_DIFF):
        c0, c1 = h * HEAD_W, (h + 1) * HEAD_W
        qh = q_ref[:, c0:c1]
        zero = jnp.zeros_like(qh)
        qm = (jnp.where(first_map, qh, zero), jnp.where(first_map, zero, qh))

        def block(j, carry, bias):
            kj = k_ref[pl.ds(pl.multiple_of(j * TQ, TQ), TQ), c0:c1]
            vj = v_ref[pl.ds(pl.multiple_of(j * TQ, TQ), TQ), c0:c1]
            out = []
            for m in range(2):
                m_old, l_old, acc = carry[m]
                s = _nt_dot(qm[m], kj)
                if bias is not None:
                    s = s + bias
                m_new = jnp.maximum(m_old, jnp.max(s, axis=-1, keepdims=True))
                alpha = jnp.exp(m_old - m_new)
                p = jnp.exp(s - m_new)
                l_new = alpha * l_old + jnp.sum(p, axis=-1, keepdims=True)
                acc = alpha * acc + jnp.dot(p.astype(BF16), vj, preferred_element_type=F32)
                out.append((m_new, l_new, acc))
            return tuple(out)

        init = tuple((jnp.full((TQ, 1), NEG_INF, F32), jnp.zeros((TQ, 1), F32),
                      jnp.zeros((TQ, V_HEAD_DIM), F32)) for _ in range(2))
        carry = lax.fori_loop(0, jnp.maximum(qi - 1, 0), lambda j, c: block(j, c, None), init)
        carry = lax.fori_loop(jnp.maximum(qi - 1, 0), qi,
                              lambda j, c: block(j, c, bias_scr[h, 1]), carry)
        (_, l1, a1), (_, l2, a2) = block(qi, carry, bias_scr[h, 0])
        o = a1 / l1 - lam * (a2 / l2)
        o_ref[:, c0:c1] = _subln(o, sg_ref[...], lam_init).astype(o_ref.dtype)


def _prompt_attn(q, k, v, rel_bias, lamv, subln_gain, *, batch, seq, lam_init):
    nq = seq // TQ
    r = jnp.arange(TQ)[:, None]
    c = jnp.arange(TQ)[None, :]
    idx = jnp.stack([_bucket_table(r - c), _bucket_table(TQ + r - c)])
    return pl.pallas_call(
        functools.partial(_prompt_attn_kernel, lam_init=lam_init),
        grid=(batch, nq),
        in_specs=[
            pl.BlockSpec(memory_space=pltpu.SMEM),
            _const_spec((4, QK_HEAD_DIM)),
            _const_spec((2, TQ, TQ)),
            _const_spec((1, V_HEAD_DIM)),
            pl.BlockSpec((TQ, Q_COLS), lambda b, i: (b * nq + i, 0)),
            pl.BlockSpec((seq, Q_COLS), lambda b, i: (b, 0)),
            pl.BlockSpec((seq, Q_COLS), lambda b, i: (b, 0)),
        ],
        out_specs=pl.BlockSpec((TQ, DIFF_WIDTH), lambda b, i: (b * nq + i, 0)),
        out_shape=jax.ShapeDtypeStruct((batch * seq, DIFF_WIDTH), BF16),
        scratch_shapes=[pltpu.VMEM((H_DIFF, 2, TQ, TQ), F32)],
        compiler_params=pltpu.CompilerParams(
            dimension_semantics=("arbitrary", "arbitrary"),
            vmem_limit_bytes=V7X_VMEM_LIMIT_BYTES),
        name="prompt_attn",
    )(rel_bias, lamv, idx, subln_gain, q, k, v)


def _sample_attn_kernel(pt_ref, rb_ref, lamv_ref, idx_ref, sg_ref, q_ref, kn_ref, vn_ref, *rest,
                        n_pages, dec_seq, lam_init):
    k_refs = rest[:n_pages]
    v_refs = rest[n_pages:2 * n_pages]
    o_ref, s_scr, kpad, vpad, bias_scr = rest[2 * n_pages:]
    del pt_ref
    rows = 2 * dec_seq

    @pl.when(pl.program_id(0) == 0)
    def _():
        for h in range(H_DIFF):
            for t in range(2):
                bias_scr[t, h * rows:(h + 1) * rows, :] = _bias_from_buckets(idx_ref[t], rb_ref, h)

    kpad[...] = jnp.zeros_like(kpad)
    vpad[...] = jnp.zeros_like(vpad)
    kpad[0:dec_seq, :] = kn_ref[...]
    vpad[0:dec_seq, :] = vn_ref[...]

    nrow = H_DIFF * rows
    q_rep = jnp.concatenate([q_ref[...]] * (nrow // dec_seq), axis=0)
    rgrp = lax.broadcasted_iota(jnp.int32, (nrow, Q_COLS), 0) // dec_seq
    cgrp = lax.broadcasted_iota(jnp.int32, (nrow, Q_COLS), 1) // QK_HEAD_DIM
    q_bd = jnp.where(rgrp == cgrp, q_rep, 0.0).astype(BF16)

    for p in range(n_pages):
        s = _nt_dot(q_bd, k_refs[p][...].astype(BF16))
        if p == n_pages - 1:
            s = s + bias_scr[0]
        s_scr[:, p * PAGE_SIZE:(p + 1) * PAGE_SIZE] = s
    s_scr[:, n_pages * PAGE_SIZE:] = _nt_dot(q_bd, kpad[...].astype(BF16)) + bias_scr[1]

    s = s_scr[...]
    e = jnp.exp(s - jnp.max(s, axis=-1, keepdims=True))
    pn = e / jnp.sum(e, axis=-1, keepdims=True)
    lam = _diff_lambda(lamv_ref, lam_init)
    a = jnp.concatenate(
        [pn[h * rows:h * rows + dec_seq] - lam * pn[h * rows + dec_seq:(h + 1) * rows]
         for h in range(H_DIFF)], axis=0).astype(BF16)

    acc = jnp.dot(a[:, n_pages * PAGE_SIZE:], vpad[...].astype(BF16), preferred_element_type=F32)
    for p in range(n_pages):
        acc = acc + jnp.dot(a[:, p * PAGE_SIZE:(p + 1) * PAGE_SIZE], v_refs[p][...].astype(BF16),
                            preferred_element_type=F32)
    for h in range(H_DIFF):
        o = acc[h * dec_seq:(h + 1) * dec_seq, h * V_HEAD_DIM:(h + 1) * V_HEAD_DIM]
        o_ref[:, h * V_HEAD_DIM:(h + 1) * V_HEAD_DIM] = _subln(o, sg_ref[...], lam_init)


def _sample_attn(q, k_new, v_new, cache_k, cache_v, page_table, rel_bias, lamv, subln_gain,
                 *, layer, dec_seq, lam_init):
    dec_batch, n_pages = page_table.shape
    n_pool = cache_k.shape[1]
    ck = cache_k.reshape(DEPTH, n_pool, PAGE_SIZE, Q_COLS)
    cv = cache_v.reshape(DEPTH, n_pool, PAGE_SIZE, DIFF_WIDTH)
    past_len = n_pages * PAGE_SIZE
    rows = 2 * dec_seq
    i = (jnp.arange(rows) % dec_seq)[:, None]
    c = jnp.arange(PAGE_SIZE)[None, :]
    idx_last = _bucket_table(past_len + i - (past_len - PAGE_SIZE + c))
    idx_new = jnp.where(c < dec_seq, _bucket_table(i - c), -1)
    idx = jnp.stack([idx_last, idx_new])
    nkeys = past_len + PAGE_SIZE

    def page_spec(p):
        return pl.BlockSpec((None, None, PAGE_SIZE, Q_COLS),
                            lambda b, pt: (layer, pt[b, p], 0, 0))

    const = lambda shape: pl.BlockSpec(shape, lambda b, pt: (0,) * len(shape))
    row_spec = pl.BlockSpec((dec_seq, Q_COLS), lambda b, pt: (b, 0))
    grid_spec = pltpu.PrefetchScalarGridSpec(
        num_scalar_prefetch=1,
        grid=(dec_batch,),
        in_specs=[
            pl.BlockSpec(memory_space=pltpu.SMEM),
            const((4, QK_HEAD_DIM)),
            const((2, rows, PAGE_SIZE)),
            const((1, V_HEAD_DIM)),
            row_spec, row_spec, row_spec,
        ] + [page_spec(p) for p in range(n_pages)] * 2,
        out_specs=row_spec,
        scratch_shapes=[
            pltpu.VMEM((H_DIFF * rows, nkeys), F32),
            pltpu.VMEM((PAGE_SIZE, Q_COLS), F32),
            pltpu.VMEM((PAGE_SIZE, DIFF_WIDTH), F32),
            pltpu.VMEM((2, H_DIFF * rows, PAGE_SIZE), F32),
        ],
    )
    return pl.pallas_call(
        functools.partial(_sample_attn_kernel, n_pages=n_pages, dec_seq=dec_seq, lam_init=lam_init),
        grid_spec=grid_spec,
        out_shape=jax.ShapeDtypeStruct((dec_batch * dec_seq, DIFF_WIDTH), F32),
        compiler_params=pltpu.CompilerParams(
            dimension_semantics=("arbitrary",), vmem_limit_bytes=V7X_VMEM_LIMIT_BYTES),
        name="sample_attn",
    )(page_table, rel_bias, lamv, idx, subln_gain, q, k_new, v_new,
      *([ck] * n_pages), *([cv] * n_pages))


def _mix_and_prenorm(x_ref, ob_ref, oa_ref, wo_ref, g_post_ref, g_pre_ref):
    mix = jnp.dot(ob_ref[...].astype(BF16), wo_ref[0:DIFF_WIDTH, :], preferred_element_type=F32)
    mix = mix + jnp.dot(oa_ref[...].astype(BF16), wo_ref[DIFF_WIDTH:, :],
                        preferred_element_type=F32)
    x1 = x_ref[...] + _rms(mix, g_post_ref[...], NORM_EPS)
    return x1, _rms(x1, g_pre_ref[...], NORM_EPS).astype(BF16)


def _conv_gate(cw, cb, h, h1, h2):
    return cb + cw[0:1] * h2 + cw[1:2] * h1 + cw[2:3] * h


def _ffn_prompt_kernel(x_ref, ob_ref, oa_ref, wo_ref, g_post_ref, g_pre_ref, g_out_ref,
                       wu_ref, cw_ref, cb_ref, wd_ref, y_ref, st_ref, hbuf, carry, act,
                       *, tm):
    @pl.when(pl.program_id(1) == 0)
    def _():
        carry[...] = jnp.zeros_like(carry)

    x1, xn2 = _mix_and_prenorm(x_ref, ob_ref, oa_ref, wo_ref, g_post_ref, g_pre_ref)

    for j in range(D_FF // TF):
        for half in range(2):
            c0 = half * D_FF + j * TF
            c1 = c0 + TF
            h = jnp.dot(xn2, wu_ref[:, c0:c1], preferred_element_type=F32)
            hbuf[half, 0:8, :] = carry[:, c0:c1]
            hbuf[half, 8:, :] = h
            carry[:, c0:c1] = h[tm - 8:, :]
        conv = []
        for half in range(2):
            c0 = half * D_FF + j * TF
            c1 = c0 + TF
            conv.append(_conv_gate(cw_ref[:, c0:c1], cb_ref[:, c0:c1], hbuf[half, 8:, :],
                                   hbuf[half, 7:7 + tm, :], hbuf[half, 6:6 + tm, :]))
        act[:, j * TF:(j + 1) * TF] = (jax.nn.silu(conv[0]) * conv[1]).astype(BF16)

    f = jnp.dot(act[...], wd_ref[...], preferred_element_type=F32)
    y_ref[...] = x1 + _rms(f, g_out_ref[...], NORM_EPS)
    st_ref[0] = carry[8 - (CONV_W - 1):, :]


def _ffn_prompt(x, o_b, o_a, w_out, g_post, g_pre, g_out, w_up, conv_w, conv_b, w_down,
                *, batch, seq):
    tm = TM_FFN
    nt = seq // tm
    row_spec = lambda w: pl.BlockSpec((tm, w), lambda b, i: (b * nt + i, 0))
    return pl.pallas_call(
        functools.partial(_ffn_prompt_kernel, tm=tm),
        grid=(batch, nt),
        in_specs=[
            row_spec(D_MODEL), row_spec(DIFF_WIDTH), row_spec(A_WIDTH),
            _const_spec((D_MODEL, D_MODEL)),
            _const_spec((1, D_MODEL)), _const_spec((1, D_MODEL)), _const_spec((1, D_MODEL)),
            _const_spec((D_MODEL, 2 * D_FF)),
            _const_spec((CONV_W, 2 * D_FF)),
            _const_spec((1, 2 * D_FF)),
            _const_spec((D_FF, D_MODEL)),
        ],
        out_specs=[row_spec(D_MODEL),
                   pl.BlockSpec((1, CONV_W - 1, 2 * D_FF), lambda b, i: (b, 0, 0))],
        out_shape=[jax.ShapeDtypeStruct((batch * seq, D_MODEL), F32),
                   jax.ShapeDtypeStruct((batch, CONV_W - 1, 2 * D_FF), F32)],
        scratch_shapes=[pltpu.VMEM((2, tm + 8, TF), F32),
                        pltpu.VMEM((8, 2 * D_FF), F32),
                        pltpu.VMEM((tm, D_FF), BF16)],
        compiler_params=pltpu.CompilerParams(
            dimension_semantics=("arbitrary", "arbitrary"),
            vmem_limit_bytes=V7X_VMEM_LIMIT_BYTES),
        name="ffn_prompt",
    )(x, o_b, o_a, w_out, g_post, g_pre, g_out, w_up, conv_w, conv_b, w_down)


def _ffn_sample_kernel(x_ref, ob_ref, oa_ref, wo_ref, g_post_ref, g_pre_ref, g_out_ref,
                       wug_ref, wuu_ref, cwg_ref, cwu_ref, cbg_ref, cbu_ref, wd_ref,
                       stg_ref, stu_ref, y_ref, sog_ref, sou_ref, x1_scr, xn_scr, acc_scr,
                       *, dec_batch, dec_seq):
    j = pl.program_id(0)
    rows = dec_batch * dec_seq

    @pl.when(j == 0)
    def _():
        x1, xn2 = _mix_and_prenorm(x_ref, ob_ref, oa_ref, wo_ref, g_post_ref, g_pre_ref)
        x1_scr[...] = x1
        xn_scr[...] = xn2
        acc_scr[...] = jnp.zeros_like(acc_scr)

    xn2 = xn_scr[...]
    conv = []
    for wu_ref, cw_ref, cb_ref, st_ref, so_ref in ((wug_ref, cwg_ref, cbg_ref, stg_ref, sog_ref),
                                                   (wuu_ref, cwu_ref, cbu_ref, stu_ref, sou_ref)):
        h = jnp.dot(xn2, wu_ref[...], preferred_element_type=F32)
        s0 = st_ref[0]
        s1 = st_ref[1]
        h1 = jnp.concatenate([s1, h[:rows - dec_batch]], axis=0)
        h2 = jnp.concatenate([s0, s1, h[:rows - 2 * dec_batch]], axis=0)
        conv.append(_conv_gate(cw_ref[...], cb_ref[...], h, h1, h2))
        so_ref[0] = h[rows - 2 * dec_batch:rows - dec_batch]
        so_ref[1] = h[rows - dec_batch:]
    act = (jax.nn.silu(conv[0]) * conv[1]).astype(BF16)
    acc_scr[...] += jnp.dot(act, wd_ref[...], preferred_element_type=F32)

    @pl.when(j == pl.num_programs(0) - 1)
    def _():
        y_ref[...] = x1_scr[...] + _rms(acc_scr[...], g_out_ref[...], NORM_EPS)


def _ffn_sample(x, o_b, o_a, w_out, g_post, g_pre, g_out, w_up, conv_w, conv_b, w_down, state,
                *, dec_batch, dec_seq):
    rows = dec_batch * dec_seq
    nj = D_FF // TF
    ns = CONV_W - 1

    def col_tile(shape, ax, first):
        return pl.BlockSpec(shape, lambda j: tuple(first + j if a == ax else 0
                                                   for a in range(len(shape))))

    gate = functools.partial(col_tile, first=0)
    up = functools.partial(col_tile, first=nj)
    state_out = pl.BlockSpec((ns, dec_batch, TF), lambda j: (0, 0, j))
    sds = jax.ShapeDtypeStruct
    y, st_gate, st_up = pl.pallas_call(
        functools.partial(_ffn_sample_kernel, dec_batch=dec_batch, dec_seq=dec_seq),
        grid=(nj,),
        in_specs=[
            _const_spec((rows, D_MODEL)), _const_spec((rows, DIFF_WIDTH)),
            _const_spec((rows, A_WIDTH)),
            _const_spec((D_MODEL, D_MODEL)),
            _const_spec((1, D_MODEL)), _const_spec((1, D_MODEL)), _const_spec((1, D_MODEL)),
            gate((D_MODEL, TF), 1), up((D_MODEL, TF), 1),
            gate((CONV_W, TF), 1), up((CONV_W, TF), 1),
            gate((1, TF), 1), up((1, TF), 1),
            gate((TF, D_MODEL), 0),
            gate((ns, dec_batch, TF), 2), up((ns, dec_batch, TF), 2),
        ],
        out_specs=[pl.BlockSpec((rows, D_MODEL), lambda j: (0, 0)), state_out, state_out],
        out_shape=[sds((rows, D_MODEL), F32), sds((ns, dec_batch, D_FF), F32),
                   sds((ns, dec_batch, D_FF), F32)],
        scratch_shapes=[pltpu.VMEM((rows, D_MODEL), F32), pltpu.VMEM((rows, D_MODEL), BF16),
                        pltpu.VMEM((rows, D_MODEL), F32)],
        compiler_params=pltpu.CompilerParams(
            dimension_semantics=("arbitrary",), vmem_limit_bytes=V7X_VMEM_LIMIT_BYTES),
        name="ffn_sample",
    )(x, o_b, o_a, w_out, g_post, g_pre, g_out, w_up, w_up, conv_w, conv_w, conv_b, conv_b,
      w_down, state, state)
    return y, jnp.concatenate([st_gate, st_up], axis=-1)


def kernel(x_prompt, x_sample, cache_k, cache_v, state_conv, page_table, rel_bias, norm_mix_pre, norm_mix_post, norm_ffn_pre, norm_ffn_post, w_in, lambda_q1, lambda_k1, lambda_q2, lambda_k2, subln_gain, sgu_ln_gain, sgu_ln_bias, w_spatial, b_spatial, w_out, w_up, conv_w, conv_b, w_down):
    batch, seq, _ = x_prompt.shape
    dec_batch, dec_seq, _ = x_sample.shape
    assert seq % TQ == 0 and seq % TM_FFN == 0 and (batch * seq) % TM_IN == 0
    assert (dec_batch * dec_seq) % TM_IN == 0 and CHUNK % dec_seq == 0

    def to_pm(a):
        return a.reshape(dec_batch, dec_seq, -1).transpose(1, 0, 2).reshape(dec_batch * dec_seq, -1)

    def to_bm(a):
        return a.reshape(dec_seq, dec_batch, -1).transpose(1, 0, 2).reshape(dec_batch * dec_seq, -1)

    y_p = x_prompt.reshape(batch * seq, D_MODEL)
    y_s = x_sample.reshape(dec_batch * dec_seq, D_MODEL)
    reps = CHUNK // dec_seq
    kp, vp, cp, ksm, vsm, csm, svs = [], [], [], [], [], [], []
    for l in range(DEPTH):
        lam_init = _lambda_init(l)
        row = lambda a: a[l].reshape(1, -1)
        w_in_l = w_in[l].astype(BF16)
        w_out_l = w_out[l].astype(BF16)
        w_up_l = w_up[l].astype(BF16)
        w_down_l = w_down[l].astype(BF16)
        lamv = jnp.stack([lambda_q1[l], lambda_k1[l], lambda_q2[l], lambda_k2[l]])
        sg = row(subln_gain)
        ffn_args = (w_out_l, row(norm_mix_post), row(norm_ffn_pre), row(norm_ffn_post),
                    w_up_l, conv_w[l], row(conv_b), w_down_l)

        bsp_p = jnp.broadcast_to(b_spatial[l][:, :, None], (A_GROUPS, CHUNK, A_GROUP_DIM))
        q, kf, vf, kb, vb, o_a = _inproj(
            y_p, row(norm_mix_pre), w_in_l, row(sgu_ln_gain), row(sgu_ln_bias),
            w_spatial[l], bsp_p, sample=False)
        o_b = _prompt_attn(q, kb, vb, rel_bias, lamv, sg, batch=batch, seq=seq, lam_init=lam_init)
        y_p, conv_p = _ffn_prompt(y_p, o_b, o_a, *ffn_args, batch=batch, seq=seq)
        kp.append(kf.reshape(batch, seq, H_DIFF, HEAD_W))
        vp.append(vf.reshape(batch, seq, H_DIFF, V_HEAD_DIM))
        cp.append(conv_p)

        wsp_s = jnp.tile(w_spatial[l][:, :dec_seq, :dec_seq], (1, reps, reps))
        bsp_s = jnp.broadcast_to(jnp.tile(b_spatial[l][:, :dec_seq], (1, reps))[:, :, None],
                                 (A_GROUPS, CHUNK, A_GROUP_DIM))
        q, kf, vf, o_a, vn = _inproj(
            y_s, row(norm_mix_pre), w_in_l, row(sgu_ln_gain), row(sgu_ln_bias),
            wsp_s, bsp_s, sample=True)
        o_b = _sample_attn(q, kf, vf, cache_k, cache_v, page_table, rel_bias, lamv, sg,
                           layer=l, dec_seq=dec_seq, lam_init=lam_init)
        state = state_conv[l].transpose(1, 0, 2)
        y_pm, conv_s = _ffn_sample(to_pm(y_s), to_pm(o_b), to_pm(o_a), *ffn_args, state,
                                   dec_batch=dec_batch, dec_seq=dec_seq)
        y_s = to_bm(y_pm)
        ksm.append(kf.reshape(dec_batch, dec_seq, H_DIFF, HEAD_W))
        vsm.append(vf.reshape(dec_batch, dec_seq, H_DIFF, V_HEAD_DIM))
        csm.append(conv_s.transpose(1, 0, 2))
        svs.append(vn.reshape(dec_batch, dec_seq, A_WIDTH))

    return (y_p.reshape(batch, seq, D_MODEL), y_s.reshape(dec_batch, dec_seq, D_MODEL),
            jnp.stack(kp), jnp.stack(vp), jnp.stack(cp),
            jnp.stack(ksm), jnp.stack(vsm), jnp.stack(csm), jnp.stack(svs))
```

```python
import functools
import math

import jax
import jax.numpy as jnp
import numpy as np
from jax import lax
from jax.experimental import pallas as pl
from jax.experimental.pallas import tpu as pltpu

F32 = jnp.float32
BF16 = jnp.bfloat16

D_MODEL = 1024
DEPTH = 2
PAGE_SIZE = 128
H_DIFF = 4
QK_HEAD_DIM = 64
V_HEAD_DIM = 128
HEAD_W = 2 * QK_HEAD_DIM
DIFF_WIDTH = H_DIFF * V_HEAD_DIM
A_GROUPS = 4
A_WIDTH = 512
A_GROUP_DIM = 128
CHUNK = 128
Q_COLS = H_DIFF * HEAD_W
IN_COLS = 3 * Q_COLS + 2 * A_WIDTH
D_FF = 2816
CONV_W = 3
REL_BUCKETS = 32
REL_MAX_EXACT = 16
REL_MAX_DIST = 128
NORM_EPS = 1e-6
SUBLN_EPS = 1e-5
NEG_INF = -1e30
LOG2_E = math.log2(math.e)

V7X_VMEM_LIMIT_BYTES = 56 * 1024 * 1024

TM_IN = 512
TQ = 256
TK = 256
TM_FFN = 256
TF = 256


def _lambda_init(layer):
    return 0.8 - 0.6 * math.exp(-0.3 * layer)


def _bucket_table(dist):
    n = np.maximum(dist, 0)
    nf = np.maximum(n, 1).astype(np.float32)
    large = REL_MAX_EXACT + (np.log(nf / REL_MAX_EXACT) / math.log(REL_MAX_DIST / REL_MAX_EXACT)
                             * (REL_BUCKETS - REL_MAX_EXACT)).astype(np.int32)
    large = np.minimum(large, REL_BUCKETS - 1)
    return np.where(dist >= 0, np.where(n < REL_MAX_EXACT, n, large), -1).astype(np.int32)


def _const_spec(shape):
    nd = len(shape)
    return pl.BlockSpec(shape, lambda *_: (0,) * nd, pipeline_mode=pl.Buffered(1))


def _rms(x, gain, eps):
    y = x * lax.rsqrt(jnp.mean(x * x, axis=-1, keepdims=True) + eps)
    return y * gain


def _nt_dot(a, b):
    return lax.dot_general(a, b, (((1,), (1,)), ((), ())), preferred_element_type=F32)


def _diff_lambda(lamv_ref, lam_init):
    p = lamv_ref[...]
    a = jnp.sum(p[0:1] * p[1:2], axis=-1, keepdims=True)
    b = jnp.sum(p[2:3] * p[3:4], axis=-1, keepdims=True)
    return jnp.exp(a) - jnp.exp(b) + lam_init


def _bias_from_buckets(idx, rb_ref, head, scale):
    far = rb_ref[REL_BUCKETS - 1, head]
    tbl = jnp.full(idx.shape, NEG_INF, F32)
    for b in range(REL_BUCKETS):
        tbl = jnp.where(idx == b, (rb_ref[b, head] - far) * scale, tbl)
    return tbl


def _inproj_kernel(x_ref, g_ref, w_ref, lng_ref, lnb_ref, wsp_ref, bsp_ref, *out_refs,
                   tm, sample):
    if sample:
        q_ref, kf_ref, vf_ref, oa_ref, vn_ref = out_refs
    else:
        qt_ref, kf_ref, vf_ref, kb_ref, vt_ref, oa_ref = out_refs

    x = x_ref[...]
    xn = _rms(x, g_ref[...], NORM_EPS).astype(BF16)

    def proj(c0, c1):
        return jnp.dot(xn, w_ref[:, c0:c1], preferred_element_type=F32)

    q = proj(0, Q_COLS)
    k = proj(Q_COLS, 2 * Q_COLS)
    v = proj(2 * Q_COLS, 3 * Q_COLS)
    for h in range(H_DIFF):
        kf_ref[pl.ds(h, tm, stride=H_DIFF), :] = k[:, h * HEAD_W:(h + 1) * HEAD_W]
        vf_ref[pl.ds(h, tm, stride=H_DIFF), :] = v[:, h * V_HEAD_DIM:(h + 1) * V_HEAD_DIM]
    if sample:
        q_ref[...] = q * (QK_HEAD_DIM ** -0.5)
    else:
        qt_ref[...] = (q * (QK_HEAD_DIM ** -0.5 * LOG2_E)).T.astype(BF16)
        vt_ref[...] = v.T.astype(BF16)
        kb_ref[...] = k.astype(BF16)

    a0 = 3 * Q_COLS
    u_a = jax.nn.gelu(proj(a0, a0 + A_WIDTH))
    v_a = jax.nn.gelu(proj(a0 + A_WIDTH, a0 + 2 * A_WIDTH))
    xc = v_a - jnp.mean(v_a, axis=-1, keepdims=True)
    vn = xc * lax.rsqrt(jnp.mean(xc * xc, axis=-1, keepdims=True) + NORM_EPS)
    vn = vn * lng_ref[...] + lnb_ref[...]
    if sample:
        vn_ref[...] = vn

    row = lax.broadcasted_iota(jnp.int32, (CHUNK, CHUNK), 0)
    col = lax.broadcasted_iota(jnp.int32, (CHUNK, CHUNK), 1)
    if sample:
        keep = jnp.logical_and((row >> 3) == (col >> 3), (col & 7) <= (row & 7))
    else:
        keep = col <= row
    vnb = vn.astype(BF16)
    for g in range(A_GROUPS):
        wg = jnp.where(keep, wsp_ref[g], 0.0).astype(BF16)
        bg = bsp_ref[g]
        c0, c1 = g * A_GROUP_DIM, (g + 1) * A_GROUP_DIM
        for c in range(tm // CHUNK):
            r0, r1 = c * CHUNK, (c + 1) * CHUNK
            s = jnp.dot(wg, vnb[r0:r1, c0:c1], preferred_element_type=F32) + bg
            oa_ref[r0:r1, c0:c1] = (u_a[r0:r1, c0:c1] * s).astype(oa_ref.dtype)


def _inproj(x, gain, w_in, ln_g, ln_b, wsp, bsp, *, sample):
    rows = x.shape[0]
    tm = TM_IN
    row_spec = lambda w: pl.BlockSpec((tm, w), lambda i: (i, 0))
    head_row_spec = pl.BlockSpec((tm * H_DIFF, HEAD_W), lambda i: (i, 0))
    col_spec = pl.BlockSpec((Q_COLS, tm), lambda i: (0, i))
    in_specs = [
        row_spec(D_MODEL),
        _const_spec((1, D_MODEL)),
        _const_spec((D_MODEL, IN_COLS)),
        _const_spec((1, A_WIDTH)),
        _const_spec((1, A_WIDTH)),
        _const_spec((A_GROUPS, CHUNK, CHUNK)),
        _const_spec((A_GROUPS, CHUNK, A_GROUP_DIM)),
    ]
    sds = jax.ShapeDtypeStruct
    kv_shape = sds((rows * H_DIFF, HEAD_W), F32)
    if sample:
        out_shape = [sds((rows, Q_COLS), F32), kv_shape, kv_shape,
                     sds((rows, A_WIDTH), F32), sds((rows, A_WIDTH), F32)]
        out_specs = [row_spec(Q_COLS), head_row_spec, head_row_spec,
                     row_spec(A_WIDTH), row_spec(A_WIDTH)]
    else:
        out_shape = [sds((Q_COLS, rows), BF16), kv_shape, kv_shape,
                     sds((rows, Q_COLS), BF16), sds((Q_COLS, rows), BF16),
                     sds((rows, A_WIDTH), BF16)]
        out_specs = [col_spec, head_row_spec, head_row_spec,
                     row_spec(Q_COLS), col_spec, row_spec(A_WIDTH)]
    return pl.pallas_call(
        functools.partial(_inproj_kernel, tm=tm, sample=sample),
        grid=(rows // tm,),
        in_specs=in_specs,
        out_specs=out_specs,
        out_shape=out_shape,
        compiler_params=pltpu.CompilerParams(
            dimension_semantics=("arbitrary",), vmem_limit_bytes=V7X_VMEM_LIMIT_BYTES),
        name="inproj_sample" if sample else "inproj_prompt",
    )(x, gain, w_in, ln_g, ln_b, wsp, bsp)


def _prompt_attn_kernel(rb_ref, lamv_ref, idx_ref, sg_ref, qt_ref, k_ref, vt_ref, o_ref,
                        bias_scr, qw_scr, m_scr, l_scr, acc_scr, *, lam_init):
    qi = pl.program_id(1)

    @pl.when(jnp.logical_and(pl.program_id(0) == 0, qi == 0))
    def _():
        for h in range(H_DIFF):
            for t in range(2):
                bias_scr[h, t] = _bias_from_buckets(idx_ref[t], rb_ref, h, LOG2_E)

    drow = lax.broadcasted_iota(jnp.int32, (HEAD_W, TQ), 0)
    first_map = drow < QK_HEAD_DIM
    for h in range(H_DIFF):
        qh = qt_ref[h * HEAD_W:(h + 1) * HEAD_W, :]
        zero = jnp.zeros_like(qh)
        qw_scr[h] = jnp.concatenate(
            [jnp.where(first_map, qh, zero), jnp.where(first_map, zero, qh)], axis=1)
    m_scr[...] = jnp.full_like(m_scr, NEG_INF)
    l_scr[...] = jnp.zeros_like(l_scr)
    acc_scr[...] = jnp.zeros_like(acc_scr)

    def block(j, kind):
        r0 = pl.multiple_of(j * TK, TK)
        for h in range(H_DIFF):
            c0, c1 = h * HEAD_W, (h + 1) * HEAD_W
            kj = k_ref[pl.ds(r0, TK), c0:c1]
            st = jnp.dot(kj, qw_scr[h], preferred_element_type=F32)
            if kind is not None:
                bias = bias_scr[h, kind]
                st = st + jnp.concatenate([bias, bias], axis=1)
            m_old = m_scr[h]
            m_new = jnp.maximum(m_old, jnp.max(st, axis=0, keepdims=True))
            alpha = jnp.exp2(m_old - m_new)
            pt = jnp.exp2(st - m_new)
            l_scr[h] = alpha * l_scr[h] + jnp.sum(pt, axis=0, keepdims=True)
            vtj = vt_ref[c0:c1, pl.ds(r0, TK)]
            acc_scr[h] = alpha * acc_scr[h] + jnp.dot(vtj, pt.astype(BF16),
                                                      preferred_element_type=F32)
            m_scr[h] = m_new

    def far_block(j, carry):
        block(j, None)
        return carry

    lax.fori_loop(0, jnp.maximum(qi - 1, 0), far_block, 0)

    @pl.when(qi >= 1)
    def _():
        block(qi - 1, 1)

    block(qi, 0)

    lam = _diff_lambda(lamv_ref, lam_init)
    for h in range(H_DIFF):
        acc = acc_scr[h]
        rl = 1.0 / l_scr[h]
        ot = acc[:, :TQ] * rl[:, :TQ] - lam * (acc[:, TQ:] * rl[:, TQ:])
        ot = ot * lax.rsqrt(jnp.mean(ot * ot, axis=0, keepdims=True) + SUBLN_EPS)
        ot = ot * sg_ref[...] * (1.0 - lam_init)
        o_ref[:, h * V_HEAD_DIM:(h + 1) * V_HEAD_DIM] = ot.T.astype(o_ref.dtype)


def _prompt_attn(qt, k, vt, rel_bias, lamv, subln_gain, *, batch, seq, lam_init):
    nq = seq // TQ
    kk = np.arange(TK)[:, None]
    qq = np.arange(TQ)[None, :]
    idx = jnp.asarray(np.stack([_bucket_table(qq - kk), _bucket_table(TQ + qq - kk)]))
    sg = jnp.broadcast_to(subln_gain.reshape(V_HEAD_DIM, 1), (V_HEAD_DIM, TQ))
    return pl.pallas_call(
        functools.partial(_prompt_attn_kernel, lam_init=lam_init),
        grid=(batch, nq),
        in_specs=[
            pl.BlockSpec(memory_space=pltpu.SMEM),
            _const_spec((4, QK_HEAD_DIM)),
            _const_spec((2, TK, TQ)),
            _const_spec((V_HEAD_DIM, TQ)),
            pl.BlockSpec((Q_COLS, TQ), lambda b, i: (0, b * nq + i)),
            pl.BlockSpec((seq, Q_COLS), lambda b, i: (b, 0)),
            pl.BlockSpec((Q_COLS, seq), lambda b, i: (0, b)),
        ],
        out_specs=pl.BlockSpec((TQ, DIFF_WIDTH), lambda b, i: (b * nq + i, 0)),
        out_shape=jax.ShapeDtypeStruct((batch * seq, DIFF_WIDTH), BF16),
        scratch_shapes=[pltpu.VMEM((H_DIFF, 2, TK, TQ), F32),
                        pltpu.VMEM((H_DIFF, HEAD_W, 2 * TQ), BF16),
                        pltpu.VMEM((H_DIFF, 1, 2 * TQ), F32),
                        pltpu.VMEM((H_DIFF, 1, 2 * TQ), F32),
                        pltpu.VMEM((H_DIFF, V_HEAD_DIM, 2 * TQ), F32)],
        compiler_params=pltpu.CompilerParams(
            dimension_semantics=("arbitrary", "arbitrary"),
            vmem_limit_bytes=V7X_VMEM_LIMIT_BYTES),
        name="prompt_attn",
    )(rel_bias, lamv, idx, sg, qt, k, vt)


def _sample_attn_kernel(pt_ref, rb_ref, lamv_ref, idx_ref, sg_ref, q_ref, kn_ref, vn_ref, *rest,
                        n_pages, dec_seq, lam_init):
    k_refs = rest[:n_pages]
    v_refs = rest[n_pages:2 * n_pages]
    o_ref, s_scr, kpad, vpad, bias_scr = rest[2 * n_pages:]
    del pt_ref
    rows = 2 * dec_seq
    past = n_pages * PAGE_SIZE

    @pl.when(pl.program_id(0) == 0)
    def _():
        for h in range(H_DIFF):
            for t in range(2):
                bias_scr[t, h * rows:(h + 1) * rows, :] = _bias_from_buckets(
                    idx_ref[t], rb_ref, h, 1.0)

    def head_rows(ref, h, n):
        return ref[pl.ds(h, n, stride=H_DIFF), :]

    kpad[...] = jnp.zeros_like(kpad)
    vpad[...] = jnp.zeros_like(vpad)
    for h in range(H_DIFF):
        kpad[h, 0:dec_seq, :] = head_rows(kn_ref, h, dec_seq)
        vpad[h, 0:dec_seq, :] = head_rows(vn_ref, h, dec_seq)

    lane = lax.broadcasted_iota(jnp.int32, (dec_seq, HEAD_W), 1)
    first_map = lane < QK_HEAD_DIM
    for h in range(H_DIFF):
        qh = q_ref[:, h * HEAD_W:(h + 1) * HEAD_W]
        qm = jnp.concatenate([jnp.where(first_map, qh, 0.0), jnp.where(first_map, 0.0, qh)],
                             axis=0).astype(BF16)
        r0, r1 = h * rows, (h + 1) * rows
        for p in range(n_pages):
            s = _nt_dot(qm, head_rows(k_refs[p], h, PAGE_SIZE).astype(BF16))
            if p == n_pages - 1:
                s = s + bias_scr[0, r0:r1, :]
            s_scr[r0:r1, p * PAGE_SIZE:(p + 1) * PAGE_SIZE] = s
        s_scr[r0:r1, past:] = _nt_dot(qm, kpad[h].astype(BF16)) + bias_scr[1, r0:r1, :]

    s = s_scr[...]
    e = jnp.exp(s - jnp.max(s, axis=-1, keepdims=True))
    pn = e / jnp.sum(e, axis=-1, keepdims=True)
    lam = _diff_lambda(lamv_ref, lam_init)
    for h in range(H_DIFF):
        r0 = h * rows
        a = (pn[r0:r0 + dec_seq] - lam * pn[r0 + dec_seq:r0 + rows]).astype(BF16)
        acc = jnp.dot(a[:, past:], vpad[h].astype(BF16), preferred_element_type=F32)
        for p in range(n_pages):
            acc = acc + jnp.dot(a[:, p * PAGE_SIZE:(p + 1) * PAGE_SIZE],
                                head_rows(v_refs[p], h, PAGE_SIZE).astype(BF16),
                                preferred_element_type=F32)
        o_ref[:, h * V_HEAD_DIM:(h + 1) * V_HEAD_DIM] = (
            _rms(acc, sg_ref[...], SUBLN_EPS) * (1.0 - lam_init))


def _sample_attn(q, k_new, v_new, cache_k, cache_v, page_table, rel_bias, lamv, subln_gain,
                 *, layer, dec_seq, lam_init):
    dec_batch, n_pages = page_table.shape
    n_pool = cache_k.shape[1]
    page_rows = PAGE_SIZE * H_DIFF
    ck = cache_k.reshape(DEPTH, n_pool, page_rows, HEAD_W)
    cv = cache_v.reshape(DEPTH, n_pool, page_rows, V_HEAD_DIM)
    past_len = n_pages * PAGE_SIZE
    rows = 2 * dec_seq
    i = (np.arange(rows) % dec_seq)[:, None]
    c = np.arange(PAGE_SIZE)[None, :]
    idx_last = _bucket_table(past_len + i - (past_len - PAGE_SIZE + c))
    idx_new = np.where(c < dec_seq, _bucket_table(i - c), -1).astype(np.int32)
    idx = jnp.asarray(np.stack([idx_last, idx_new]))
    nkeys = past_len + PAGE_SIZE

    def page_spec(p):
        return pl.BlockSpec((None, None, page_rows, HEAD_W),
                            lambda b, pt: (layer, pt[b, p], 0, 0))

    const = lambda shape: pl.BlockSpec(shape, lambda b, pt: (0,) * len(shape))
    row_spec = pl.BlockSpec((dec_seq, Q_COLS), lambda b, pt: (b, 0))
    new_spec = pl.BlockSpec((dec_seq * H_DIFF, HEAD_W), lambda b, pt: (b, 0))
    grid_spec = pltpu.PrefetchScalarGridSpec(
        num_scalar_prefetch=1,
        grid=(dec_batch,),
        in_specs=[
            pl.BlockSpec(memory_space=pltpu.SMEM),
            const((4, QK_HEAD_DIM)),
            const((2, rows, PAGE_SIZE)),
            const((1, V_HEAD_DIM)),
            row_spec, new_spec, new_spec,
        ] + [page_spec(p) for p in range(n_pages)] * 2,
        out_specs=row_spec,
        scratch_shapes=[
            pltpu.VMEM((H_DIFF * rows, nkeys), F32),
            pltpu.VMEM((H_DIFF, PAGE_SIZE, HEAD_W), F32),
            pltpu.VMEM((H_DIFF, PAGE_SIZE, V_HEAD_DIM), F32),
            pltpu.VMEM((2, H_DIFF * rows, PAGE_SIZE), F32),
        ],
    )
    return pl.pallas_call(
        functools.partial(_sample_attn_kernel, n_pages=n_pages, dec_seq=dec_seq, lam_init=lam_init),
        grid_spec=grid_spec,
        out_shape=jax.ShapeDtypeStruct((dec_batch * dec_seq, DIFF_WIDTH), F32),
        compiler_params=pltpu.CompilerParams(
            dimension_semantics=("arbitrary",), vmem_limit_bytes=V7X_VMEM_LIMIT_BYTES),
        name="sample_attn",
    )(page_table, rel_bias, lamv, idx, subln_gain, q, k_new, v_new,
      *([ck] * n_pages), *([cv] * n_pages))


def _mix_and_prenorm(x_ref, ob_ref, oa_ref, wo_ref, g_post_ref, g_pre_ref):
    mix = jnp.dot(ob_ref[...].astype(BF16), wo_ref[0:DIFF_WIDTH, :], preferred_element_type=F32)
    mix = mix + jnp.dot(oa_ref[...].astype(BF16), wo_ref[DIFF_WIDTH:, :],
                        preferred_element_type=F32)
    x1 = x_ref[...] + _rms(mix, g_post_ref[...], NORM_EPS)
    return x1, _rms(x1, g_pre_ref[...], NORM_EPS).astype(BF16)


def _conv_gate(cw, cb, h, h1, h2):
    return cb + cw[0:1] * h2 + cw[1:2] * h1 + cw[2:3] * h


def _ffn_prompt_kernel(x_ref, ob_ref, oa_ref, wo_ref, g_post_ref, g_pre_ref, g_out_ref,
                       wu_ref, cw_ref, cb_ref, wd_ref, y_ref, st_ref, hbuf, carry, act,
                       *, tm):
    @pl.when(pl.program_id(1) == 0)
    def _():
        carry[...] = jnp.zeros_like(carry)

    x1, xn2 = _mix_and_prenorm(x_ref, ob_ref, oa_ref, wo_ref, g_post_ref, g_pre_ref)

    for j in range(D_FF // TF):
        for half in range(2):
            c0 = half * D_FF + j * TF
            c1 = c0 + TF
            h = jnp.dot(xn2, wu_ref[:, c0:c1], preferred_element_type=F32)
            hbuf[half, 0:8, :] = carry[:, c0:c1]
            hbuf[half, 8:, :] = h
            carry[:, c0:c1] = h[tm - 8:, :]
        conv = []
        for half in range(2):
            c0 = half * D_FF + j * TF
            c1 = c0 + TF
            conv.append(_conv_gate(cw_ref[:, c0:c1], cb_ref[:, c0:c1], hbuf[half, 8:, :],
                                   hbuf[half, 7:7 + tm, :], hbuf[half, 6:6 + tm, :]))
        act[:, j * TF:(j + 1) * TF] = (jax.nn.silu(conv[0]) * conv[1]).astype(BF16)

    f = jnp.dot(act[...], wd_ref[...], preferred_element_type=F32)
    y_ref[...] = x1 + _rms(f, g_out_ref[...], NORM_EPS)
    st_ref[0] = carry[8 - (CONV_W - 1):, :]


def _ffn_prompt(x, o_b, o_a, w_out, g_post, g_pre, g_out, w_up, conv_w, conv_b, w_down,
                *, batch, seq):
    tm = TM_FFN
    nt = seq // tm
    row_spec = lambda w: pl.BlockSpec((tm, w), lambda b, i: (b * nt + i, 0))
    return pl.pallas_call(
        functools.partial(_ffn_prompt_kernel, tm=tm),
        grid=(batch, nt),
        in_specs=[
            row_spec(D_MODEL), row_spec(DIFF_WIDTH), row_spec(A_WIDTH),
            _const_spec((D_MODEL, D_MODEL)),
            _const_spec((1, D_MODEL)), _const_spec((1, D_MODEL)), _const_spec((1, D_MODEL)),
            _const_spec((D_MODEL, 2 * D_FF)),
            _const_spec((CONV_W, 2 * D_FF)),
            _const_spec((1, 2 * D_FF)),
            _const_spec((D_FF, D_MODEL)),
        ],
        out_specs=[row_spec(D_MODEL),
                   pl.BlockSpec((1, CONV_W - 1, 2 * D_FF), lambda b, i: (b, 0, 0))],
        out_shape=[jax.ShapeDtypeStruct((batch * seq, D_MODEL), F32),
                   jax.ShapeDtypeStruct((batch, CONV_W - 1, 2 * D_FF), F32)],
        scratch_shapes=[pltpu.VMEM((2, tm + 8, TF), F32),
                        pltpu.VMEM((8, 2 * D_FF), F32),
                        pltpu.VMEM((tm, D_FF), BF16)],
        compiler_params=pltpu.CompilerParams(
            dimension_semantics=("arbitrary", "arbitrary"),
            vmem_limit_bytes=V7X_VMEM_LIMIT_BYTES),
        name="ffn_prompt",
    )(x, o_b, o_a, w_out, g_post, g_pre, g_out, w_up, conv_w, conv_b, w_down)


def _ffn_sample_kernel(x_ref, ob_ref, oa_ref, wo_ref, g_post_ref, g_pre_ref, g_out_ref,
                       wug_ref, wuu_ref, cwg_ref, cwu_ref, cbg_ref, cbu_ref, wd_ref,
                       stg_ref, stu_ref, y_ref, sog_ref, sou_ref, x1_scr, xn_scr, acc_scr,
                       *, dec_batch, dec_seq):
    j = pl.program_id(0)
    rows = dec_batch * dec_seq

    @pl.when(j == 0)
    def _():
        x1, xn2 = _mix_and_prenorm(x_ref, ob_ref, oa_ref, wo_ref, g_post_ref, g_pre_ref)
        x1_scr[...] = x1
        xn_scr[...] = xn2
        acc_scr[...] = jnp.zeros_like(acc_scr)

    xn2 = xn_scr[...]
    conv = []
    for wu_ref, cw_ref, cb_ref, st_ref, so_ref in ((wug_ref, cwg_ref, cbg_ref, stg_ref, sog_ref),
                                                   (wuu_ref, cwu_ref, cbu_ref, stu_ref, sou_ref)):
        h = jnp.dot(xn2, wu_ref[...], preferred_element_type=F32)
        s0 = st_ref[0]
        s1 = st_ref[1]
        h1 = jnp.concatenate([s1, h[:rows - dec_batch]], axis=0)
        h2 = jnp.concatenate([s0, s1, h[:rows - 2 * dec_batch]], axis=0)
        conv.append(_conv_gate(cw_ref[...], cb_ref[...], h, h1, h2))
        so_ref[0] = h[rows - 2 * dec_batch:rows - dec_batch]
        so_ref[1] = h[rows - dec_batch:]
    act = (jax.nn.silu(conv[0]) * conv[1]).astype(BF16)
    acc_scr[...] += jnp.dot(act, wd_ref[...], preferred_element_type=F32)

    @pl.when(j == pl.num_programs(0) - 1)
    def _():
        y_ref[...] = x1_scr[...] + _rms(acc_scr[...], g_out_ref[...], NORM_EPS)


def _ffn_sample(x, o_b, o_a, w_out, g_post, g_pre, g_out, w_up, conv_w, conv_b, w_down, state,
                *, dec_batch, dec_seq):
    rows = dec_batch * dec_seq
    nj = D_FF // TF
    ns = CONV_W - 1

    def col_tile(shape, ax, first):
        return pl.BlockSpec(shape, lambda j: tuple(first + j if a == ax else 0
                                                   for a in range(len(shape))))

    gate = functools.partial(col_tile, first=0)
    up = functools.partial(col_tile, first=nj)
    state_out = pl.BlockSpec((ns, dec_batch, TF), lambda j: (0, 0, j))
    sds = jax.ShapeDtypeStruct
    y, st_gate, st_up = pl.pallas_call(
        functools.partial(_ffn_sample_kernel, dec_batch=dec_batch, dec_seq=dec_seq),
        grid=(nj,),
        in_specs=[
            _const_spec((rows, D_MODEL)), _const_spec((rows, DIFF_WIDTH)),
            _const_spec((rows, A_WIDTH)),
            _const_spec((D_MODEL, D_MODEL)),
            _const_spec((1, D_MODEL)), _const_spec((1, D_MODEL)), _const_spec((1, D_MODEL)),
            gate((D_MODEL, TF), 1), up((D_MODEL, TF), 1),
            gate((CONV_W, TF), 1), up((CONV_W, TF), 1),
            gate((1, TF), 1), up((1, TF), 1),
            gate((TF, D_MODEL), 0),
            gate((ns, dec_batch, TF), 2), up((ns, dec_batch, TF), 2),
        ],
        out_specs=[pl.BlockSpec((rows, D_MODEL), lambda j: (0, 0)), state_out, state_out],
        out_shape=[sds((rows, D_MODEL), F32), sds((ns, dec_batch, D_FF), F32),
                   sds((ns, dec_batch, D_FF), F32)],
        scratch_shapes=[pltpu.VMEM((rows, D_MODEL), F32), pltpu.VMEM((rows, D_MODEL), BF16),
                        pltpu.VMEM((rows, D_MODEL), F32)],
        compiler_params=pltpu.CompilerParams(
            dimension_semantics=("arbitrary",), vmem_limit_bytes=V7X_VMEM_LIMIT_BYTES),
        name="ffn_sample",
    )(x, o_b, o_a, w_out, g_post, g_pre, g_out, w_up, w_up, conv_w, conv_w, conv_b, conv_b,
      w_down, state, state)
    return y, jnp.concatenate([st_gate, st_up], axis=-1)


def kernel(x_prompt, x_sample, cache_k, cache_v, state_conv, page_table, rel_bias, norm_mix_pre, norm_mix_post, norm_ffn_pre, norm_ffn_post, w_in, lambda_q1, lambda_k1, lambda_q2, lambda_k2, subln_gain, sgu_ln_gain, sgu_ln_bias, w_spatial, b_spatial, w_out, w_up, conv_w, conv_b, w_down):
    batch, seq, _ = x_prompt.shape
    dec_batch, dec_seq, _ = x_sample.shape
    assert seq % TQ == 0 and seq % TM_FFN == 0 and (batch * seq) % TM_IN == 0
    assert (dec_batch * dec_seq) % TM_IN == 0 and CHUNK % dec_seq == 0 and TQ == TK

    def to_pm(a):
        return a.reshape(dec_batch, dec_seq, -1).transpose(1, 0, 2).reshape(dec_batch * dec_seq, -1)

    def to_bm(a):
        return a.reshape(dec_seq, dec_batch, -1).transpose(1, 0, 2).reshape(dec_batch * dec_seq, -1)

    y_p = x_prompt.reshape(batch * seq, D_MODEL)
    y_s = x_sample.reshape(dec_batch * dec_seq, D_MODEL)
    reps = CHUNK // dec_seq
    kp, vp, cp, ksm, vsm, csm, svs = [], [], [], [], [], [], []
    for l in range(DEPTH):
        lam_init = _lambda_init(l)
        row = lambda a: a[l].reshape(1, -1)
        w_in_l = w_in[l].astype(BF16)
        w_out_l = w_out[l].astype(BF16)
        w_up_l = w_up[l].astype(BF16)
        w_down_l = w_down[l].astype(BF16)
        lamv = jnp.stack([lambda_q1[l], lambda_k1[l], lambda_q2[l], lambda_k2[l]])
        ffn_args = (w_out_l, row(norm_mix_post), row(norm_ffn_pre), row(norm_ffn_post),
                    w_up_l, conv_w[l], row(conv_b), w_down_l)

        bsp_p = jnp.broadcast_to(b_spatial[l][:, :, None], (A_GROUPS, CHUNK, A_GROUP_DIM))
        qt, kf, vf, kb, vt, o_a = _inproj(
            y_p, row(norm_mix_pre), w_in_l, row(sgu_ln_gain), row(sgu_ln_bias),
            w_spatial[l], bsp_p, sample=False)
        o_b = _prompt_attn(qt, kb, vt, rel_bias, lamv, subln_gain[l],
                           batch=batch, seq=seq, lam_init=lam_init)
        y_p, conv_p = _ffn_prompt(y_p, o_b, o_a, *ffn_args, batch=batch, seq=seq)
        kp.append(kf.reshape(batch, seq, H_DIFF, HEAD_W))
        vp.append(vf.reshape(batch, seq, H_DIFF, V_HEAD_DIM))
        cp.append(conv_p)

        wsp_s = jnp.tile(w_spatial[l][:, :dec_seq, :dec_seq], (1, reps, reps))
        bsp_s = jnp.broadcast_to(jnp.tile(b_spatial[l][:, :dec_seq], (1, reps))[:, :, None],
                                 (A_GROUPS, CHUNK, A_GROUP_DIM))
        q, kf, vf, o_a, vn = _inproj(
            y_s, row(norm_mix_pre), w_in_l, row(sgu_ln_gain), row(sgu_ln_bias),
            wsp_s, bsp_s, sample=True)
        o_b = _sample_attn(q, kf, vf, cache_k, cache_v, page_table, rel_bias, lamv,
                           row(subln_gain), layer=l, dec_seq=dec_seq, lam_init=lam_init)
        state = state_conv[l].transpose(1, 0, 2)
        y_pm, conv_s = _ffn_sample(to_pm(y_s), to_pm(o_b), to_pm(o_a), *ffn_args, state,
                                   dec_batch=dec_batch, dec_seq=dec_seq)
        y_s = to_bm(y_pm)
        ksm.append(kf.reshape(dec_batch, dec_seq, H_DIFF, HEAD_W))
        vsm.append(vf.reshape(dec_batch, dec_seq, H_DIFF, V_HEAD_DIM))
        csm.append(conv_s.transpose(1, 0, 2))
        svs.append(vn.reshape(dec_batch, dec_seq, A_WIDTH))

    return (y_p.reshape(batch, seq, D_MODEL), y_s.reshape(dec_batch, dec_seq, D_MODEL),
            jnp.stack(kp), jnp.stack(vp), jnp.stack(cp),
            jnp.stack(ksm), jnp.stack(vsm), jnp.stack(csm), jnp.stack(svs))
```

```python
import functools
import math

import jax
import jax.numpy as jnp
import numpy as np
from jax import lax
from jax.experimental import pallas as pl
from jax.experimental.pallas import tpu as pltpu

F32 = jnp.float32
BF16 = jnp.bfloat16

D_MODEL = 1024
DEPTH = 2
PAGE_SIZE = 128
H_DIFF = 4
QK_HEAD_DIM = 64
V_HEAD_DIM = 128
HEAD_W = 2 * QK_HEAD_DIM
DIFF_WIDTH = H_DIFF * V_HEAD_DIM
A_GROUPS = 4
A_WIDTH = 512
A_GROUP_DIM = 128
CHUNK = 128
Q_COLS = H_DIFF * HEAD_W
IN_COLS = 3 * Q_COLS + 2 * A_WIDTH
D_FF = 2816
CONV_W = 3
REL_BUCKETS = 32
REL_MAX_EXACT = 16
REL_MAX_DIST = 128
NORM_EPS = 1e-6
SUBLN_EPS = 1e-5
NEG_INF = -1e30
LOG2_E = math.log2(math.e)

V7X_VMEM_LIMIT_BYTES = 56 * 1024 * 1024

TM_IN = 512
TQ = 256
TK = 256
TM_FFN = 512
TF = 256


def _lambda_init(layer):
    return 0.8 - 0.6 * math.exp(-0.3 * layer)


def _bucket_table(dist):
    n = np.maximum(dist, 0)
    nf = np.maximum(n, 1).astype(np.float32)
    large = REL_MAX_EXACT + (np.log(nf / REL_MAX_EXACT) / math.log(REL_MAX_DIST / REL_MAX_EXACT)
                             * (REL_BUCKETS - REL_MAX_EXACT)).astype(np.int32)
    large = np.minimum(large, REL_BUCKETS - 1)
    return np.where(dist >= 0, np.where(n < REL_MAX_EXACT, n, large), -1).astype(np.int32)


def _const_spec(shape):
    nd = len(shape)
    return pl.BlockSpec(shape, lambda *_: (0,) * nd, pipeline_mode=pl.Buffered(1))


def _rms(x, gain, eps):
    y = x * lax.rsqrt(jnp.mean(x * x, axis=-1, keepdims=True) + eps)
    return y * gain


def _nt_dot(a, b):
    return lax.dot_general(a, b, (((1,), (1,)), ((), ())), preferred_element_type=F32)


def _diff_lambda(lamv_ref, lam_init):
    p = lamv_ref[...]
    a = jnp.sum(p[0:1] * p[1:2], axis=-1, keepdims=True)
    b = jnp.sum(p[2:3] * p[3:4], axis=-1, keepdims=True)
    return jnp.exp(a) - jnp.exp(b) + lam_init


def _bias_from_buckets(idx, rb_ref, head, scale):
    far = rb_ref[REL_BUCKETS - 1, head]
    tbl = jnp.full(idx.shape, NEG_INF, F32)
    for b in range(REL_BUCKETS):
        tbl = jnp.where(idx == b, (rb_ref[b, head] - far) * scale, tbl)
    return tbl


def _inproj_kernel(x_ref, g_ref, w_ref, lng_ref, lnb_ref, wsp_ref, bsp_ref, *out_refs,
                   tm, sample, n_aliased):
    out_refs = out_refs[n_aliased:]
    if sample:
        q_ref, kf_ref, vf_ref, oa_ref, vn_ref = out_refs
    else:
        qt_ref, kf_ref, vf_ref, kb_ref, vt_ref, oa_ref = out_refs

    x = x_ref[...]
    xn = _rms(x, g_ref[...], NORM_EPS).astype(BF16)

    def proj(c0, c1):
        return jnp.dot(xn, w_ref[:, c0:c1], preferred_element_type=F32)

    q = proj(0, Q_COLS)
    k = proj(Q_COLS, 2 * Q_COLS)
    v = proj(2 * Q_COLS, 3 * Q_COLS)
    for h in range(H_DIFF):
        kf_ref[pl.ds(h, tm, stride=H_DIFF), :] = k[:, h * HEAD_W:(h + 1) * HEAD_W]
        vf_ref[pl.ds(h, tm, stride=H_DIFF), :] = v[:, h * V_HEAD_DIM:(h + 1) * V_HEAD_DIM]
    if sample:
        q_ref[...] = q * (QK_HEAD_DIM ** -0.5)
    else:
        qt_ref[...] = (q * (QK_HEAD_DIM ** -0.5 * LOG2_E)).T.astype(BF16)
        vt_ref[...] = v.T.astype(BF16)
        kb_ref[...] = k.astype(BF16)

    a0 = 3 * Q_COLS
    u_a = jax.nn.gelu(proj(a0, a0 + A_WIDTH))
    v_a = jax.nn.gelu(proj(a0 + A_WIDTH, a0 + 2 * A_WIDTH))
    xc = v_a - jnp.mean(v_a, axis=-1, keepdims=True)
    vn = xc * lax.rsqrt(jnp.mean(xc * xc, axis=-1, keepdims=True) + NORM_EPS)
    vn = vn * lng_ref[...] + lnb_ref[...]
    if sample:
        vn_ref[...] = vn

    row = lax.broadcasted_iota(jnp.int32, (CHUNK, CHUNK), 0)
    col = lax.broadcasted_iota(jnp.int32, (CHUNK, CHUNK), 1)
    if sample:
        keep = jnp.logical_and((row >> 3) == (col >> 3), (col & 7) <= (row & 7))
    else:
        keep = col <= row
    vnb = vn.astype(BF16)
    for g in range(A_GROUPS):
        wg = jnp.where(keep, wsp_ref[g], 0.0).astype(BF16)
        bg = bsp_ref[g]
        c0, c1 = g * A_GROUP_DIM, (g + 1) * A_GROUP_DIM
        for c in range(tm // CHUNK):
            r0, r1 = c * CHUNK, (c + 1) * CHUNK
            s = jnp.dot(wg, vnb[r0:r1, c0:c1], preferred_element_type=F32) + bg
            oa_ref[r0:r1, c0:c1] = (u_a[r0:r1, c0:c1] * s).astype(oa_ref.dtype)


def _inproj(x, gain, w_in, ln_g, ln_b, wsp, bsp, kv_all, *, layer, sample):
    rows = x.shape[0]
    tm = TM_IN
    row_spec = lambda w: pl.BlockSpec((tm, w), lambda i: (i, 0))
    head_row_spec = pl.BlockSpec((None, tm * H_DIFF, HEAD_W), lambda i: (layer, i, 0))
    col_spec = pl.BlockSpec((Q_COLS, tm), lambda i: (0, i))
    in_specs = [
        row_spec(D_MODEL),
        _const_spec((1, D_MODEL)),
        _const_spec((D_MODEL, IN_COLS)),
        _const_spec((1, A_WIDTH)),
        _const_spec((1, A_WIDTH)),
        _const_spec((A_GROUPS, CHUNK, CHUNK)),
        _const_spec((A_GROUPS, CHUNK, A_GROUP_DIM)),
    ]
    sds = jax.ShapeDtypeStruct
    kv_shape = sds((DEPTH, rows * H_DIFF, HEAD_W), F32)
    aliased = () if kv_all is None else tuple(kv_all)
    n_in = len(in_specs)
    in_specs += [pl.BlockSpec(memory_space=pl.ANY)] * len(aliased)
    aliases = {n_in + i: 1 + i for i in range(len(aliased))}
    if sample:
        out_shape = [sds((rows, Q_COLS), F32), kv_shape, kv_shape,
                     sds((rows, A_WIDTH), F32), sds((rows, A_WIDTH), F32)]
        out_specs = [row_spec(Q_COLS), head_row_spec, head_row_spec,
                     row_spec(A_WIDTH), row_spec(A_WIDTH)]
    else:
        out_shape = [sds((Q_COLS, rows), BF16), kv_shape, kv_shape,
                     sds((rows, Q_COLS), BF16), sds((Q_COLS, rows), BF16),
                     sds((rows, A_WIDTH), BF16)]
        out_specs = [col_spec, head_row_spec, head_row_spec,
                     row_spec(Q_COLS), col_spec, row_spec(A_WIDTH)]
    return pl.pallas_call(
        functools.partial(_inproj_kernel, tm=tm, sample=sample, n_aliased=len(aliased)),
        grid=(rows // tm,),
        in_specs=in_specs,
        out_specs=out_specs,
        out_shape=out_shape,
        input_output_aliases=aliases,
        compiler_params=pltpu.CompilerParams(
            dimension_semantics=("arbitrary",), vmem_limit_bytes=V7X_VMEM_LIMIT_BYTES),
        name="inproj_sample" if sample else "inproj_prompt",
    )(x, gain, w_in, ln_g, ln_b, wsp, bsp, *aliased)


def _prompt_attn_kernel(rb_ref, lamv_ref, idx_ref, sg_ref, qt_ref, k_ref, vt_ref, o_ref,
                        bias_scr, qw_scr, m_scr, l_scr, acc_scr, *, lam_init):
    qi = pl.program_id(1)

    @pl.when(jnp.logical_and(pl.program_id(0) == 0, qi == 0))
    def _():
        for h in range(H_DIFF):
            for t in range(2):
                bias_scr[h, t] = _bias_from_buckets(idx_ref[t], rb_ref, h, LOG2_E)

    drow = lax.broadcasted_iota(jnp.int32, (HEAD_W, TQ), 0)
    first_map = drow < QK_HEAD_DIM
    for h in range(H_DIFF):
        qh = qt_ref[h * HEAD_W:(h + 1) * HEAD_W, :]
        zero = jnp.zeros_like(qh)
        qw_scr[h] = jnp.concatenate(
            [jnp.where(first_map, qh, zero), jnp.where(first_map, zero, qh)], axis=1)
    m_scr[...] = jnp.full_like(m_scr, NEG_INF)
    l_scr[...] = jnp.zeros_like(l_scr)
    acc_scr[...] = jnp.zeros_like(acc_scr)

    def block(j, kind):
        r0 = pl.multiple_of(j * TK, TK)

        def scores(h):
            kj = k_ref[pl.ds(r0, TK), h * HEAD_W:(h + 1) * HEAD_W]
            return jnp.dot(kj, qw_scr[h], preferred_element_type=F32)

        st_next = scores(0)
        for h in range(H_DIFF):
            c0, c1 = h * HEAD_W, (h + 1) * HEAD_W
            st = st_next
            if h + 1 < H_DIFF:
                st_next = scores(h + 1)
            if kind is not None:
                bias = bias_scr[h, kind]
                st = jnp.concatenate([bias, bias], axis=1) + st
            m_old = m_scr[h]
            m_new = jnp.maximum(m_old, jnp.max(st, axis=0, keepdims=True))
            alpha = jnp.exp2(m_old - m_new)
            pt = jnp.exp2(st - m_new)
            l_scr[h] = alpha * l_scr[h] + jnp.sum(pt, axis=0, keepdims=True)
            vtj = vt_ref[c0:c1, pl.ds(r0, TK)]
            acc_scr[h] = alpha * acc_scr[h] + jnp.dot(vtj, pt.astype(BF16),
                                                      preferred_element_type=F32)
            m_scr[h] = m_new

    def far_block(j, carry):
        block(j, None)
        return carry

    def near_block(j, carry):
        block(j, qi - j)
        return carry

    first_near = jnp.maximum(qi - 1, 0)
    lax.fori_loop(0, first_near, far_block, 0)
    lax.fori_loop(first_near, qi + 1, near_block, 0)

    lam = _diff_lambda(lamv_ref, lam_init)
    for h in range(H_DIFF):
        acc = acc_scr[h]
        rl = 1.0 / l_scr[h]
        ot = acc[:, :TQ] * rl[:, :TQ] - lam * (acc[:, TQ:] * rl[:, TQ:])
        ot = ot * lax.rsqrt(jnp.mean(ot * ot, axis=0, keepdims=True) + SUBLN_EPS)
        ot = ot * sg_ref[...] * (1.0 - lam_init)
        o_ref[:, h * V_HEAD_DIM:(h + 1) * V_HEAD_DIM] = ot.T.astype(o_ref.dtype)


def _prompt_attn(qt, k, vt, rel_bias, lamv, subln_gain, *, batch, seq, lam_init):
    nq = seq // TQ
    kk = np.arange(TK)[:, None]
    qq = np.arange(TQ)[None, :]
    idx = jnp.asarray(np.stack([_bucket_table(qq - kk), _bucket_table(TQ + qq - kk)]))
    sg = jnp.broadcast_to(subln_gain.reshape(V_HEAD_DIM, 1), (V_HEAD_DIM, TQ))
    return pl.pallas_call(
        functools.partial(_prompt_attn_kernel, lam_init=lam_init),
        grid=(batch, nq),
        in_specs=[
            pl.BlockSpec(memory_space=pltpu.SMEM),
            _const_spec((4, QK_HEAD_DIM)),
            _const_spec((2, TK, TQ)),
            _const_spec((V_HEAD_DIM, TQ)),
            pl.BlockSpec((Q_COLS, TQ), lambda b, i: (0, b * nq + i)),
            pl.BlockSpec((seq, Q_COLS), lambda b, i: (b, 0)),
            pl.BlockSpec((Q_COLS, seq), lambda b, i: (0, b)),
        ],
        out_specs=pl.BlockSpec((TQ, DIFF_WIDTH), lambda b, i: (b * nq + i, 0)),
        out_shape=jax.ShapeDtypeStruct((batch * seq, DIFF_WIDTH), BF16),
        scratch_shapes=[pltpu.VMEM((H_DIFF, 2, TK, TQ), F32),
                        pltpu.VMEM((H_DIFF, HEAD_W, 2 * TQ), BF16),
                        pltpu.VMEM((H_DIFF, 1, 2 * TQ), F32),
                        pltpu.VMEM((H_DIFF, 1, 2 * TQ), F32),
                        pltpu.VMEM((H_DIFF, V_HEAD_DIM, 2 * TQ), F32)],
        compiler_params=pltpu.CompilerParams(
            dimension_semantics=("arbitrary", "arbitrary"),
            vmem_limit_bytes=V7X_VMEM_LIMIT_BYTES),
        name="prompt_attn",
    )(rel_bias, lamv, idx, sg, qt, k, vt)


def _sample_attn_kernel(pt_ref, rb_ref, lamv_ref, idx_ref, sg_ref, q_ref, kn_ref, vn_ref, *rest,
                        n_pages, dec_seq, lam_init):
    k_refs = rest[:n_pages]
    v_refs = rest[n_pages:2 * n_pages]
    o_ref, s_scr, kpad, vpad, bias_scr = rest[2 * n_pages:]
    del pt_ref
    rows = 2 * dec_seq
    past = n_pages * PAGE_SIZE

    @pl.when(pl.program_id(0) == 0)
    def _():
        for h in range(H_DIFF):
            for t in range(2):
                bias_scr[t, h * rows:(h + 1) * rows, :] = _bias_from_buckets(
                    idx_ref[t], rb_ref, h, 1.0)

    def head_rows(ref, h, n):
        return ref[pl.ds(h, n, stride=H_DIFF), :]

    kpad[...] = jnp.zeros_like(kpad)
    vpad[...] = jnp.zeros_like(vpad)
    for h in range(H_DIFF):
        kpad[h, 0:dec_seq, :] = head_rows(kn_ref, h, dec_seq)
        vpad[h, 0:dec_seq, :] = head_rows(vn_ref, h, dec_seq)

    lane = lax.broadcasted_iota(jnp.int32, (dec_seq, HEAD_W), 1)
    first_map = lane < QK_HEAD_DIM
    for h in range(H_DIFF):
        qh = q_ref[:, h * HEAD_W:(h + 1) * HEAD_W]
        qm = jnp.concatenate([jnp.where(first_map, qh, 0.0), jnp.where(first_map, 0.0, qh)],
                             axis=0).astype(BF16)
        r0, r1 = h * rows, (h + 1) * rows
        for p in range(n_pages):
            s = _nt_dot(qm, head_rows(k_refs[p], h, PAGE_SIZE).astype(BF16))
            if p == n_pages - 1:
                s = s + bias_scr[0, r0:r1, :]
            s_scr[r0:r1, p * PAGE_SIZE:(p + 1) * PAGE_SIZE] = s
        s_scr[r0:r1, past:] = _nt_dot(qm, kpad[h].astype(BF16)) + bias_scr[1, r0:r1, :]

    s = s_scr[...]
    e = jnp.exp(s - jnp.max(s, axis=-1, keepdims=True))
    pn = e / jnp.sum(e, axis=-1, keepdims=True)
    lam = _diff_lambda(lamv_ref, lam_init)
    for h in range(H_DIFF):
        r0 = h * rows
        a = (pn[r0:r0 + dec_seq] - lam * pn[r0 + dec_seq:r0 + rows]).astype(BF16)
        acc = jnp.dot(a[:, past:], vpad[h].astype(BF16), preferred_element_type=F32)
        for p in range(n_pages):
            acc = acc + jnp.dot(a[:, p * PAGE_SIZE:(p + 1) * PAGE_SIZE],
                                head_rows(v_refs[p], h, PAGE_SIZE).astype(BF16),
                                preferred_element_type=F32)
        o_ref[:, h * V_HEAD_DIM:(h + 1) * V_HEAD_DIM] = (
            _rms(acc, sg_ref[...], SUBLN_EPS) * (1.0 - lam_init))


def _sample_attn(q, k_new, v_new, cache_k, cache_v, page_table, rel_bias, lamv, subln_gain,
                 *, layer, dec_seq, lam_init):
    dec_batch, n_pages = page_table.shape
    n_pool = cache_k.shape[1]
    page_rows = PAGE_SIZE * H_DIFF
    ck = cache_k.reshape(DEPTH, n_pool, page_rows, HEAD_W)
    cv = cache_v.reshape(DEPTH, n_pool, page_rows, V_HEAD_DIM)
    past_len = n_pages * PAGE_SIZE
    rows = 2 * dec_seq
    i = (np.arange(rows) % dec_seq)[:, None]
    c = np.arange(PAGE_SIZE)[None, :]
    idx_last = _bucket_table(past_len + i - (past_len - PAGE_SIZE + c))
    idx_new = np.where(c < dec_seq, _bucket_table(i - c), -1).astype(np.int32)
    idx = jnp.asarray(np.stack([idx_last, idx_new]))
    nkeys = past_len + PAGE_SIZE

    def page_spec(p):
        return pl.BlockSpec((None, None, page_rows, HEAD_W),
                            lambda b, pt: (layer, pt[b, p], 0, 0))

    const = lambda shape: pl.BlockSpec(shape, lambda b, pt: (0,) * len(shape))
    row_spec = pl.BlockSpec((dec_seq, Q_COLS), lambda b, pt: (b, 0))
    new_spec = pl.BlockSpec((None, dec_seq * H_DIFF, HEAD_W), lambda b, pt: (layer, b, 0))
    grid_spec = pltpu.PrefetchScalarGridSpec(
        num_scalar_prefetch=1,
        grid=(dec_batch,),
        in_specs=[
            pl.BlockSpec(memory_space=pltpu.SMEM),
            const((4, QK_HEAD_DIM)),
            const((2, rows, PAGE_SIZE)),
            const((1, V_HEAD_DIM)),
            row_spec, new_spec, new_spec,
        ] + [page_spec(p) for p in range(n_pages)] * 2,
        out_specs=row_spec,
        scratch_shapes=[
            pltpu.VMEM((H_DIFF * rows, nkeys), F32),
            pltpu.VMEM((H_DIFF, PAGE_SIZE, HEAD_W), F32),
            pltpu.VMEM((H_DIFF, PAGE_SIZE, V_HEAD_DIM), F32),
            pltpu.VMEM((2, H_DIFF * rows, PAGE_SIZE), F32),
        ],
    )
    return pl.pallas_call(
        functools.partial(_sample_attn_kernel, n_pages=n_pages, dec_seq=dec_seq, lam_init=lam_init),
        grid_spec=grid_spec,
        out_shape=jax.ShapeDtypeStruct((dec_batch * dec_seq, DIFF_WIDTH), F32),
        compiler_params=pltpu.CompilerParams(
            dimension_semantics=("arbitrary",), vmem_limit_bytes=V7X_VMEM_LIMIT_BYTES),
        name="sample_attn",
    )(page_table, rel_bias, lamv, idx, subln_gain, q, k_new, v_new,
      *([ck] * n_pages), *([cv] * n_pages))


def _mix_and_prenorm(x_ref, ob_ref, oa_ref, wo_ref, g_post_ref, g_pre_ref):
    mix = jnp.dot(ob_ref[...].astype(BF16), wo_ref[0:DIFF_WIDTH, :], preferred_element_type=F32)
    mix = mix + jnp.dot(oa_ref[...].astype(BF16), wo_ref[DIFF_WIDTH:, :],
                        preferred_element_type=F32)
    x1 = x_ref[...] + _rms(mix, g_post_ref[...], NORM_EPS)
    return x1, _rms(x1, g_pre_ref[...], NORM_EPS).astype(BF16)


def _conv_gate(cw, cb, h, h1, h2):
    return cb + cw[0:1] * h2 + cw[1:2] * h1 + cw[2:3] * h


def _ffn_prompt_kernel(x_ref, ob_ref, oa_ref, wo_ref, g_post_ref, g_pre_ref, g_out_ref,
                       wu_ref, cw_ref, cb_ref, wd_ref, y_ref, st_ref, hbuf, carry, act,
                       *, tm):
    @pl.when(pl.program_id(1) == 0)
    def _():
        carry[...] = jnp.zeros_like(carry)

    x1, xn2 = _mix_and_prenorm(x_ref, ob_ref, oa_ref, wo_ref, g_post_ref, g_pre_ref)

    for j in range(D_FF // TF):
        for half in range(2):
            c0 = half * D_FF + j * TF
            c1 = c0 + TF
            h = jnp.dot(xn2, wu_ref[:, c0:c1], preferred_element_type=F32)
            hbuf[half, 0:8, :] = carry[:, c0:c1]
            hbuf[half, 8:, :] = h
            carry[:, c0:c1] = h[tm - 8:, :]
        conv = []
        for half in range(2):
            c0 = half * D_FF + j * TF
            c1 = c0 + TF
            conv.append(_conv_gate(cw_ref[:, c0:c1], cb_ref[:, c0:c1], hbuf[half, 8:, :],
                                   hbuf[half, 7:7 + tm, :], hbuf[half, 6:6 + tm, :]))
        act[:, j * TF:(j + 1) * TF] = (jax.nn.silu(conv[0]) * conv[1]).astype(BF16)

    f = jnp.dot(act[...], wd_ref[...], preferred_element_type=F32)
    y_ref[...] = x1 + _rms(f, g_out_ref[...], NORM_EPS)
    st_ref[0] = carry[8 - (CONV_W - 1):, :]


def _ffn_prompt(x, o_b, o_a, w_out, g_post, g_pre, g_out, w_up, conv_w, conv_b, w_down,
                *, batch, seq):
    tm = TM_FFN
    nt = seq // tm
    row_spec = lambda w: pl.BlockSpec((tm, w), lambda b, i: (b * nt + i, 0))
    return pl.pallas_call(
        functools.partial(_ffn_prompt_kernel, tm=tm),
        grid=(batch, nt),
        in_specs=[
            row_spec(D_MODEL), row_spec(DIFF_WIDTH), row_spec(A_WIDTH),
            _const_spec((D_MODEL, D_MODEL)),
            _const_spec((1, D_MODEL)), _const_spec((1, D_MODEL)), _const_spec((1, D_MODEL)),
            _const_spec((D_MODEL, 2 * D_FF)),
            _const_spec((CONV_W, 2 * D_FF)),
            _const_spec((1, 2 * D_FF)),
            _const_spec((D_FF, D_MODEL)),
        ],
        out_specs=[row_spec(D_MODEL),
                   pl.BlockSpec((1, CONV_W - 1, 2 * D_FF), lambda b, i: (b, 0, 0))],
        out_shape=[jax.ShapeDtypeStruct((batch * seq, D_MODEL), F32),
                   jax.ShapeDtypeStruct((batch, CONV_W - 1, 2 * D_FF), F32)],
        scratch_shapes=[pltpu.VMEM((2, tm + 8, TF), F32),
                        pltpu.VMEM((8, 2 * D_FF), F32),
                        pltpu.VMEM((tm, D_FF), BF16)],
        compiler_params=pltpu.CompilerParams(
            dimension_semantics=("arbitrary", "arbitrary"),
            vmem_limit_bytes=V7X_VMEM_LIMIT_BYTES),
        name="ffn_prompt",
    )(x, o_b, o_a, w_out, g_post, g_pre, g_out, w_up, conv_w, conv_b, w_down)


def _ffn_sample_kernel(x_ref, ob_ref, oa_ref, wo_ref, g_post_ref, g_pre_ref, g_out_ref,
                       wug_ref, wuu_ref, cwg_ref, cwu_ref, cbg_ref, cbu_ref, wd_ref,
                       stg_ref, stu_ref, y_ref, sog_ref, sou_ref, x1_scr, xn_scr, acc_scr,
                       *, dec_batch, dec_seq):
    j = pl.program_id(0)
    rows = dec_batch * dec_seq

    @pl.when(j == 0)
    def _():
        x1, xn2 = _mix_and_prenorm(x_ref, ob_ref, oa_ref, wo_ref, g_post_ref, g_pre_ref)
        x1_scr[...] = x1
        xn_scr[...] = xn2
        acc_scr[...] = jnp.zeros_like(acc_scr)

    xn2 = xn_scr[...]
    conv = []
    for wu_ref, cw_ref, cb_ref, st_ref, so_ref in ((wug_ref, cwg_ref, cbg_ref, stg_ref, sog_ref),
                                                   (wuu_ref, cwu_ref, cbu_ref, stu_ref, sou_ref)):
        h = jnp.dot(xn2, wu_ref[...], preferred_element_type=F32)
        s0 = st_ref[0]
        s1 = st_ref[1]
        h1 = jnp.concatenate([s1, h[:rows - dec_batch]], axis=0)
        h2 = jnp.concatenate([s0, s1, h[:rows - 2 * dec_batch]], axis=0)
        conv.append(_conv_gate(cw_ref[...], cb_ref[...], h, h1, h2))
        so_ref[0] = h[rows - 2 * dec_batch:rows - dec_batch]
        so_ref[1] = h[rows - dec_batch:]
    act = (jax.nn.silu(conv[0]) * conv[1]).astype(BF16)
    acc_scr[...] += jnp.dot(act, wd_ref[...], preferred_element_type=F32)

    @pl.when(j == pl.num_programs(0) - 1)
    def _():
        y_ref[...] = x1_scr[...] + _rms(acc_scr[...], g_out_ref[...], NORM_EPS)


def _ffn_sample(x, o_b, o_a, w_out, g_post, g_pre, g_out, w_up, conv_w, conv_b, w_down, state,
                *, dec_batch, dec_seq):
    rows = dec_batch * dec_seq
    nj = D_FF // TF
    ns = CONV_W - 1

    def col_tile(shape, ax, first):
        return pl.BlockSpec(shape, lambda j: tuple(first + j if a == ax else 0
                                                   for a in range(len(shape))))

    gate = functools.partial(col_tile, first=0)
    up = functools.partial(col_tile, first=nj)
    state_out = pl.BlockSpec((ns, dec_batch, TF), lambda j: (0, 0, j))
    sds = jax.ShapeDtypeStruct
    y, st_gate, st_up = pl.pallas_call(
        functools.partial(_ffn_sample_kernel, dec_batch=dec_batch, dec_seq=dec_seq),
        grid=(nj,),
        in_specs=[
            _const_spec((rows, D_MODEL)), _const_spec((rows, DIFF_WIDTH)),
            _const_spec((rows, A_WIDTH)),
            _const_spec((D_MODEL, D_MODEL)),
            _const_spec((1, D_MODEL)), _const_spec((1, D_MODEL)), _const_spec((1, D_MODEL)),
            gate((D_MODEL, TF), 1), up((D_MODEL, TF), 1),
            gate((CONV_W, TF), 1), up((CONV_W, TF), 1),
            gate((1, TF), 1), up((1, TF), 1),
            gate((TF, D_MODEL), 0),
            gate((ns, dec_batch, TF), 2), up((ns, dec_batch, TF), 2),
        ],
        out_specs=[pl.BlockSpec((rows, D_MODEL), lambda j: (0, 0)), state_out, state_out],
        out_shape=[sds((rows, D_MODEL), F32), sds((ns, dec_batch, D_FF), F32),
                   sds((ns, dec_batch, D_FF), F32)],
        scratch_shapes=[pltpu.VMEM((rows, D_MODEL), F32), pltpu.VMEM((rows, D_MODEL), BF16),
                        pltpu.VMEM((rows, D_MODEL), F32)],
        compiler_params=pltpu.CompilerParams(
            dimension_semantics=("arbitrary",), vmem_limit_bytes=V7X_VMEM_LIMIT_BYTES),
        name="ffn_sample",
    )(x, o_b, o_a, w_out, g_post, g_pre, g_out, w_up, w_up, conv_w, conv_w, conv_b, conv_b,
      w_down, state, state)
    return y, jnp.concatenate([st_gate, st_up], axis=-1)


def kernel(x_prompt, x_sample, cache_k, cache_v, state_conv, page_table, rel_bias, norm_mix_pre, norm_mix_post, norm_ffn_pre, norm_ffn_post, w_in, lambda_q1, lambda_k1, lambda_q2, lambda_k2, subln_gain, sgu_ln_gain, sgu_ln_bias, w_spatial, b_spatial, w_out, w_up, conv_w, conv_b, w_down):
    batch, seq, _ = x_prompt.shape
    dec_batch, dec_seq, _ = x_sample.shape
    assert seq % TQ == 0 and seq % TM_FFN == 0 and (batch * seq) % TM_IN == 0
    assert (dec_batch * dec_seq) % TM_IN == 0 and CHUNK % dec_seq == 0 and TQ == TK

    def to_pm(a):
        return a.reshape(dec_batch, dec_seq, -1).transpose(1, 0, 2).reshape(dec_batch * dec_seq, -1)

    def to_bm(a):
        return a.reshape(dec_seq, dec_batch, -1).transpose(1, 0, 2).reshape(dec_batch * dec_seq, -1)

    y_p = x_prompt.reshape(batch * seq, D_MODEL)
    y_s = x_sample.reshape(dec_batch * dec_seq, D_MODEL)
    reps = CHUNK // dec_seq
    cp, csm, svs = [], [], []
    kv_p = kv_s = None
    for l in range(DEPTH):
        lam_init = _lambda_init(l)
        row = lambda a: a[l].reshape(1, -1)
        w_in_l = w_in[l].astype(BF16)
        w_out_l = w_out[l].astype(BF16)
        w_up_l = w_up[l].astype(BF16)
        w_down_l = w_down[l].astype(BF16)
        lamv = jnp.stack([lambda_q1[l], lambda_k1[l], lambda_q2[l], lambda_k2[l]])
        ffn_args = (w_out_l, row(norm_mix_post), row(norm_ffn_pre), row(norm_ffn_post),
                    w_up_l, conv_w[l], row(conv_b), w_down_l)

        bsp_p = jnp.broadcast_to(b_spatial[l][:, :, None], (A_GROUPS, CHUNK, A_GROUP_DIM))
        qt, kf, vf, kb, vt, o_a = _inproj(
            y_p, row(norm_mix_pre), w_in_l, row(sgu_ln_gain), row(sgu_ln_bias),
            w_spatial[l], bsp_p, kv_p, layer=l, sample=False)
        kv_p = (kf, vf)
        o_b = _prompt_attn(qt, kb, vt, rel_bias, lamv, subln_gain[l],
                           batch=batch, seq=seq, lam_init=lam_init)
        y_p, conv_p = _ffn_prompt(y_p, o_b, o_a, *ffn_args, batch=batch, seq=seq)
        cp.append(conv_p)

        wsp_s = jnp.tile(w_spatial[l][:, :dec_seq, :dec_seq], (1, reps, reps))
        bsp_s = jnp.broadcast_to(jnp.tile(b_spatial[l][:, :dec_seq], (1, reps))[:, :, None],
                                 (A_GROUPS, CHUNK, A_GROUP_DIM))
        q, kf, vf, o_a, vn = _inproj(
            y_s, row(norm_mix_pre), w_in_l, row(sgu_ln_gain), row(sgu_ln_bias),
            wsp_s, bsp_s, kv_s, layer=l, sample=True)
        kv_s = (kf, vf)
        o_b = _sample_attn(q, kf, vf, cache_k, cache_v, page_table, rel_bias, lamv,
                           row(subln_gain), layer=l, dec_seq=dec_seq, lam_init=lam_init)
        state = state_conv[l].transpose(1, 0, 2)
        y_pm, conv_s = _ffn_sample(to_pm(y_s), to_pm(o_b), to_pm(o_a), *ffn_args, state,
                                   dec_batch=dec_batch, dec_seq=dec_seq)
        y_s = to_bm(y_pm)
        csm.append(conv_s.transpose(1, 0, 2))
        svs.append(vn.reshape(dec_batch, dec_seq, A_WIDTH))

    kv_shape_p = (DEPTH, batch, seq, H_DIFF, HEAD_W)
    kv_shape_s = (DEPTH, dec_batch, dec_seq, H_DIFF, HEAD_W)
    return (y_p.reshape(batch, seq, D_MODEL), y_s.reshape(dec_batch, dec_seq, D_MODEL),
            kv_p[0].reshape(kv_shape_p), kv_p[1].reshape(kv_shape_p), jnp.stack(cp),
            kv_s[0].reshape(kv_shape_s), kv_s[1].reshape(kv_shape_s),
            jnp.stack(csm), jnp.stack(svs))
```

```python
import functools
import math

import jax
import jax.numpy as jnp
import numpy as np
from jax import lax
from jax.experimental import pallas as pl
from jax.experimental.pallas import tpu as pltpu

F32 = jnp.float32
BF16 = jnp.bfloat16

D_MODEL = 1024
DEPTH = 2
PAGE_SIZE = 128
H_DIFF = 4
QK_HEAD_DIM = 64
V_HEAD_DIM = 128
HEAD_W = 2 * QK_HEAD_DIM
DIFF_WIDTH = H_DIFF * V_HEAD_DIM
A_GROUPS = 4
A_WIDTH = 512
A_GROUP_DIM = 128
CHUNK = 128
LANE = 128
Q_COLS = H_DIFF * HEAD_W
IN_COLS = 3 * Q_COLS + 2 * A_WIDTH
D_FF = 2816
CONV_W = 3
REL_BUCKETS = 32
REL_MAX_EXACT = 16
REL_MAX_DIST = 128
NORM_EPS = 1e-6
SUBLN_EPS = 1e-5
NEG_INF = -1e30
LOG2_E = math.log2(math.e)

V7X_VMEM_LIMIT_BYTES = 56 * 1024 * 1024

TM_IN = 512
TQ = 256
TK = 256
TM_FFN = 512
TS = 256
PITCH = TS // 8 + 8
TF = 256


def _lambda_init(layer):
    return 0.8 - 0.6 * math.exp(-0.3 * layer)


def _bucket_table(dist):
    n = np.maximum(dist, 0)
    nf = np.maximum(n, 1).astype(np.float32)
    large = REL_MAX_EXACT + (np.log(nf / REL_MAX_EXACT) / math.log(REL_MAX_DIST / REL_MAX_EXACT)
                             * (REL_BUCKETS - REL_MAX_EXACT)).astype(np.int32)
    large = np.minimum(large, REL_BUCKETS - 1)
    return np.where(dist >= 0, np.where(n < REL_MAX_EXACT, n, large), -1).astype(np.int32)


def _const_spec(shape):
    nd = len(shape)
    return pl.BlockSpec(shape, lambda *_: (0,) * nd, pipeline_mode=pl.Buffered(1))


def _rms(x, gain, eps):
    y = x * lax.rsqrt(jnp.mean(x * x, axis=-1, keepdims=True) + eps)
    return y * gain


def _nt_dot(a, b):
    return lax.dot_general(a, b, (((1,), (1,)), ((), ())), preferred_element_type=F32)


def _diff_lambda(lamv_ref, lam_init):
    p = lamv_ref[...]
    a = jnp.sum(p[0:1] * p[1:2], axis=-1, keepdims=True)
    b = jnp.sum(p[2:3] * p[3:4], axis=-1, keepdims=True)
    return jnp.exp(a) - jnp.exp(b) + lam_init


def _bias_from_buckets(idx, rb_ref, head, scale):
    far = rb_ref[REL_BUCKETS - 1, head]
    tbl = jnp.full(idx.shape, NEG_INF, F32)
    for b in range(REL_BUCKETS):
        tbl = jnp.where(idx == b, (rb_ref[b, head] - far) * scale, tbl)
    return tbl


def _inproj_kernel(x_ref, g_ref, w_ref, lng_ref, lnb_ref, wsp_ref, bsp_ref, *out_refs,
                   tm, sample, n_aliased):
    out_refs = out_refs[n_aliased:]
    if sample:
        q_ref, kf_ref, vf_ref, oa_ref, vn_ref = out_refs
    else:
        qt_ref, kf_ref, vf_ref, kb_ref, vt_ref, oa_ref = out_refs

    x = x_ref[...]
    xn = _rms(x, g_ref[...], NORM_EPS).astype(BF16)

    def proj(c0, c1):
        return jnp.dot(xn, w_ref[:, c0:c1], preferred_element_type=F32)

    q = proj(0, Q_COLS)
    k = proj(Q_COLS, 2 * Q_COLS)
    v = proj(2 * Q_COLS, 3 * Q_COLS)
    for h in range(H_DIFF):
        kf_ref[pl.ds(h, tm, stride=H_DIFF), :] = k[:, h * HEAD_W:(h + 1) * HEAD_W]
        vf_ref[pl.ds(h, tm, stride=H_DIFF), :] = v[:, h * V_HEAD_DIM:(h + 1) * V_HEAD_DIM]
    if sample:
        q_ref[...] = q * (QK_HEAD_DIM ** -0.5)
    else:
        qt_ref[...] = (q * (QK_HEAD_DIM ** -0.5 * LOG2_E)).T.astype(BF16)
        vt_ref[...] = v.T.astype(BF16)
        kb_ref[...] = k.astype(BF16)

    a0 = 3 * Q_COLS
    u_a = jax.nn.gelu(proj(a0, a0 + A_WIDTH))
    v_a = jax.nn.gelu(proj(a0 + A_WIDTH, a0 + 2 * A_WIDTH))
    xc = v_a - jnp.mean(v_a, axis=-1, keepdims=True)
    vn = xc * lax.rsqrt(jnp.mean(xc * xc, axis=-1, keepdims=True) + NORM_EPS)
    vn = vn * lng_ref[...] + lnb_ref[...]
    if sample:
        vn_ref[...] = vn

    row = lax.broadcasted_iota(jnp.int32, (CHUNK, CHUNK), 0)
    col = lax.broadcasted_iota(jnp.int32, (CHUNK, CHUNK), 1)
    if sample:
        keep = jnp.logical_and((row >> 3) == (col >> 3), (col & 7) <= (row & 7))
    else:
        keep = col <= row
    vnb = vn.astype(BF16)
    for g in range(A_GROUPS):
        wg = jnp.where(keep, wsp_ref[g], 0.0).astype(BF16)
        bg = bsp_ref[g]
        c0, c1 = g * A_GROUP_DIM, (g + 1) * A_GROUP_DIM
        for c in range(tm // CHUNK):
            r0, r1 = c * CHUNK, (c + 1) * CHUNK
            s = jnp.dot(wg, vnb[r0:r1, c0:c1], preferred_element_type=F32) + bg
            oa_ref[r0:r1, c0:c1] = (u_a[r0:r1, c0:c1] * s).astype(oa_ref.dtype)


def _inproj(x, gain, w_in, ln_g, ln_b, wsp, bsp, kv_all, *, layer, sample):
    rows = x.shape[0]
    tm = TM_IN
    row_spec = lambda w: pl.BlockSpec((tm, w), lambda i: (i, 0))
    head_row_spec = pl.BlockSpec((None, tm * H_DIFF, HEAD_W), lambda i: (layer, i, 0))
    col_spec = pl.BlockSpec((Q_COLS, tm), lambda i: (0, i))
    in_specs = [
        row_spec(D_MODEL),
        _const_spec((1, D_MODEL)),
        _const_spec((D_MODEL, IN_COLS)),
        _const_spec((1, A_WIDTH)),
        _const_spec((1, A_WIDTH)),
        _const_spec((A_GROUPS, CHUNK, CHUNK)),
        _const_spec((A_GROUPS, CHUNK, A_GROUP_DIM)),
    ]
    sds = jax.ShapeDtypeStruct
    kv_shape = sds((DEPTH, rows * H_DIFF, HEAD_W), F32)
    aliased = () if kv_all is None else tuple(kv_all)
    n_in = len(in_specs)
    in_specs += [pl.BlockSpec(memory_space=pl.ANY)] * len(aliased)
    aliases = {n_in + i: 1 + i for i in range(len(aliased))}
    if sample:
        out_shape = [sds((rows, Q_COLS), F32), kv_shape, kv_shape,
                     sds((rows, A_WIDTH), F32), sds((rows, A_WIDTH), F32)]
        out_specs = [row_spec(Q_COLS), head_row_spec, head_row_spec,
                     row_spec(A_WIDTH), row_spec(A_WIDTH)]
    else:
        out_shape = [sds((Q_COLS, rows), BF16), kv_shape, kv_shape,
                     sds((rows, Q_COLS), BF16), sds((Q_COLS, rows), BF16),
                     sds((rows, A_WIDTH), BF16)]
        out_specs = [col_spec, head_row_spec, head_row_spec,
                     row_spec(Q_COLS), col_spec, row_spec(A_WIDTH)]
    return pl.pallas_call(
        functools.partial(_inproj_kernel, tm=tm, sample=sample, n_aliased=len(aliased)),
        grid=(rows // tm,),
        in_specs=in_specs,
        out_specs=out_specs,
        out_shape=out_shape,
        input_output_aliases=aliases,
        compiler_params=pltpu.CompilerParams(
            dimension_semantics=("arbitrary",), vmem_limit_bytes=V7X_VMEM_LIMIT_BYTES),
        name="inproj_sample" if sample else "inproj_prompt",
    )(x, gain, w_in, ln_g, ln_b, wsp, bsp, *aliased)


def _prompt_attn_kernel(rb_ref, lamv_ref, idx_ref, sg_ref, qt_ref, k_ref, vt_ref, o_ref,
                        bias_scr, qw_scr, m_scr, l_scr, acc_scr, *, lam_init):
    qi = pl.program_id(1)

    @pl.when(jnp.logical_and(pl.program_id(0) == 0, qi == 0))
    def _():
        for h in range(H_DIFF):
            for t in range(2):
                bias_scr[h, t] = _bias_from_buckets(idx_ref[t], rb_ref, h, LOG2_E)

    drow = lax.broadcasted_iota(jnp.int32, (HEAD_W, TQ), 0)
    first_map = drow < QK_HEAD_DIM
    for h in range(H_DIFF):
        qh = qt_ref[h * HEAD_W:(h + 1) * HEAD_W, :]
        zero = jnp.zeros_like(qh)
        qw_scr[h] = jnp.concatenate(
            [jnp.where(first_map, qh, zero), jnp.where(first_map, zero, qh)], axis=1)
    m_scr[...] = jnp.full_like(m_scr, NEG_INF)
    l_scr[...] = jnp.zeros_like(l_scr)
    acc_scr[...] = jnp.zeros_like(acc_scr)

    def block(j, kind):
        r0 = pl.multiple_of(j * TK, TK)

        def scores(h):
            kj = k_ref[pl.ds(r0, TK), h * HEAD_W:(h + 1) * HEAD_W]
            return jnp.dot(kj, qw_scr[h], preferred_element_type=F32)

        st_next = scores(0)
        for h in range(H_DIFF):
            c0, c1 = h * HEAD_W, (h + 1) * HEAD_W
            st = st_next
            if h + 1 < H_DIFF:
                st_next = scores(h + 1)
            if kind is not None:
                bias = bias_scr[h, kind]
                st = jnp.concatenate([bias, bias], axis=1) + st
            m_old = m_scr[h]
            m_new = jnp.maximum(m_old, jnp.max(st, axis=0, keepdims=True))
            alpha = jnp.exp2(m_old - m_new)
            pt = jnp.exp2(st - m_new)
            l_scr[h] = alpha * l_scr[h] + jnp.sum(pt, axis=0, keepdims=True)
            vtj = vt_ref[c0:c1, pl.ds(r0, TK)]
            acc_scr[h] = alpha * acc_scr[h] + jnp.dot(vtj, pt.astype(BF16),
                                                      preferred_element_type=F32)
            m_scr[h] = m_new

    def far_block(j, carry):
        block(j, None)
        return carry

    def near_block(j, carry):
        block(j, qi - j)
        return carry

    first_near = jnp.maximum(qi - 1, 0)
    lax.fori_loop(0, first_near, far_block, 0)
    lax.fori_loop(first_near, qi + 1, near_block, 0)

    lam = _diff_lambda(lamv_ref, lam_init)
    for h in range(H_DIFF):
        acc = acc_scr[h]
        rl = 1.0 / l_scr[h]
        ot = acc[:, :TQ] * rl[:, :TQ] - lam * (acc[:, TQ:] * rl[:, TQ:])
        ot = ot * lax.rsqrt(jnp.mean(ot * ot, axis=0, keepdims=True) + SUBLN_EPS)
        ot = ot * sg_ref[...] * (1.0 - lam_init)
        o_ref[:, h * V_HEAD_DIM:(h + 1) * V_HEAD_DIM] = ot.T.astype(o_ref.dtype)


def _prompt_attn(qt, k, vt, rel_bias, lamv, subln_gain, *, batch, seq, lam_init):
    nq = seq // TQ
    kk = np.arange(TK)[:, None]
    qq = np.arange(TQ)[None, :]
    idx = jnp.asarray(np.stack([_bucket_table(qq - kk), _bucket_table(TQ + qq - kk)]))
    sg = jnp.broadcast_to(subln_gain.reshape(V_HEAD_DIM, 1), (V_HEAD_DIM, TQ))
    return pl.pallas_call(
        functools.partial(_prompt_attn_kernel, lam_init=lam_init),
        grid=(batch, nq),
        in_specs=[
            pl.BlockSpec(memory_space=pltpu.SMEM),
            _const_spec((4, QK_HEAD_DIM)),
            _const_spec((2, TK, TQ)),
            _const_spec((V_HEAD_DIM, TQ)),
            pl.BlockSpec((Q_COLS, TQ), lambda b, i: (0, b * nq + i)),
            pl.BlockSpec((seq, Q_COLS), lambda b, i: (b, 0)),
            pl.BlockSpec((Q_COLS, seq), lambda b, i: (0, b)),
        ],
        out_specs=pl.BlockSpec((TQ, DIFF_WIDTH), lambda b, i: (b * nq + i, 0)),
        out_shape=jax.ShapeDtypeStruct((batch * seq, DIFF_WIDTH), BF16),
        scratch_shapes=[pltpu.VMEM((H_DIFF, 2, TK, TQ), F32),
                        pltpu.VMEM((H_DIFF, HEAD_W, 2 * TQ), BF16),
                        pltpu.VMEM((H_DIFF, 1, 2 * TQ), F32),
                        pltpu.VMEM((H_DIFF, 1, 2 * TQ), F32),
                        pltpu.VMEM((H_DIFF, V_HEAD_DIM, 2 * TQ), F32)],
        compiler_params=pltpu.CompilerParams(
            dimension_semantics=("arbitrary", "arbitrary"),
            vmem_limit_bytes=V7X_VMEM_LIMIT_BYTES),
        name="prompt_attn",
    )(rel_bias, lamv, idx, sg, qt, k, vt)


def _sample_attn_kernel(pt_ref, rb_ref, lamv_ref, idx_ref, sg_ref, q_ref, kn_ref, vn_ref, *rest,
                        n_pages, dec_seq, lam_init):
    k_refs = rest[:n_pages]
    v_refs = rest[n_pages:2 * n_pages]
    o_ref, s_scr, kpad, vpad, bias_scr = rest[2 * n_pages:]
    del pt_ref
    rows = 2 * dec_seq
    past = n_pages * PAGE_SIZE

    @pl.when(pl.program_id(0) == 0)
    def _():
        for h in range(H_DIFF):
            for t in range(2):
                bias_scr[t, h * rows:(h + 1) * rows, :] = _bias_from_buckets(
                    idx_ref[t], rb_ref, h, 1.0)

    def head_rows(ref, h, n):
        return ref[pl.ds(h, n, stride=H_DIFF), :]

    kpad[...] = jnp.zeros_like(kpad)
    vpad[...] = jnp.zeros_like(vpad)
    for h in range(H_DIFF):
        kpad[h, 0:dec_seq, :] = head_rows(kn_ref, h, dec_seq)
        vpad[h, 0:dec_seq, :] = head_rows(vn_ref, h, dec_seq)

    lane = lax.broadcasted_iota(jnp.int32, (dec_seq, HEAD_W), 1)
    first_map = lane < QK_HEAD_DIM
    for h in range(H_DIFF):
        qh = q_ref[:, h * HEAD_W:(h + 1) * HEAD_W]
        qm = jnp.concatenate([jnp.where(first_map, qh, 0.0), jnp.where(first_map, 0.0, qh)],
                             axis=0).astype(BF16)
        r0, r1 = h * rows, (h + 1) * rows
        for p in range(n_pages):
            s = _nt_dot(qm, head_rows(k_refs[p], h, PAGE_SIZE).astype(BF16))
            if p == n_pages - 1:
                s = s + bias_scr[0, r0:r1, :]
            s_scr[r0:r1, p * PAGE_SIZE:(p + 1) * PAGE_SIZE] = s
        s_scr[r0:r1, past:] = _nt_dot(qm, kpad[h].astype(BF16)) + bias_scr[1, r0:r1, :]

    s = s_scr[...]
    e = jnp.exp(s - jnp.max(s, axis=-1, keepdims=True))
    pn = e / jnp.sum(e, axis=-1, keepdims=True)
    lam = _diff_lambda(lamv_ref, lam_init)
    for h in range(H_DIFF):
        r0 = h * rows
        a = (pn[r0:r0 + dec_seq] - lam * pn[r0 + dec_seq:r0 + rows]).astype(BF16)
        acc = jnp.dot(a[:, past:], vpad[h].astype(BF16), preferred_element_type=F32)
        for p in range(n_pages):
            acc = acc + jnp.dot(a[:, p * PAGE_SIZE:(p + 1) * PAGE_SIZE],
                                head_rows(v_refs[p], h, PAGE_SIZE).astype(BF16),
                                preferred_element_type=F32)
        o_ref[:, h * V_HEAD_DIM:(h + 1) * V_HEAD_DIM] = (
            _rms(acc, sg_ref[...], SUBLN_EPS) * (1.0 - lam_init))


def _sample_attn(q, k_new, v_new, cache_k, cache_v, page_table, rel_bias, lamv, subln_gain,
                 *, layer, dec_seq, lam_init):
    dec_batch, n_pages = page_table.shape
    n_pool = cache_k.shape[1]
    page_rows = PAGE_SIZE * H_DIFF
    ck = cache_k.reshape(DEPTH, n_pool, page_rows, HEAD_W)
    cv = cache_v.reshape(DEPTH, n_pool, page_rows, V_HEAD_DIM)
    past_len = n_pages * PAGE_SIZE
    rows = 2 * dec_seq
    i = (np.arange(rows) % dec_seq)[:, None]
    c = np.arange(PAGE_SIZE)[None, :]
    idx_last = _bucket_table(past_len + i - (past_len - PAGE_SIZE + c))
    idx_new = np.where(c < dec_seq, _bucket_table(i - c), -1).astype(np.int32)
    idx = jnp.asarray(np.stack([idx_last, idx_new]))
    nkeys = past_len + PAGE_SIZE

    def page_spec(p):
        return pl.BlockSpec((None, None, page_rows, HEAD_W),
                            lambda b, pt: (layer, pt[b, p], 0, 0))

    const = lambda shape: pl.BlockSpec(shape, lambda b, pt: (0,) * len(shape))
    row_spec = pl.BlockSpec((dec_seq, Q_COLS), lambda b, pt: (b, 0))
    new_spec = pl.BlockSpec((None, dec_seq * H_DIFF, HEAD_W), lambda b, pt: (layer, b, 0))
    grid_spec = pltpu.PrefetchScalarGridSpec(
        num_scalar_prefetch=1,
        grid=(dec_batch,),
        in_specs=[
            pl.BlockSpec(memory_space=pltpu.SMEM),
            const((4, QK_HEAD_DIM)),
            const((2, rows, PAGE_SIZE)),
            const((1, V_HEAD_DIM)),
            row_spec, new_spec, new_spec,
        ] + [page_spec(p) for p in range(n_pages)] * 2,
        out_specs=row_spec,
        scratch_shapes=[
            pltpu.VMEM((H_DIFF * rows, nkeys), F32),
            pltpu.VMEM((H_DIFF, PAGE_SIZE, HEAD_W), F32),
            pltpu.VMEM((H_DIFF, PAGE_SIZE, V_HEAD_DIM), F32),
            pltpu.VMEM((2, H_DIFF * rows, PAGE_SIZE), F32),
        ],
    )
    return pl.pallas_call(
        functools.partial(_sample_attn_kernel, n_pages=n_pages, dec_seq=dec_seq, lam_init=lam_init),
        grid_spec=grid_spec,
        out_shape=jax.ShapeDtypeStruct((dec_batch * dec_seq, DIFF_WIDTH), F32),
        compiler_params=pltpu.CompilerParams(
            dimension_semantics=("arbitrary",), vmem_limit_bytes=V7X_VMEM_LIMIT_BYTES),
        name="sample_attn",
    )(page_table, rel_bias, lamv, idx, subln_gain, q, k_new, v_new,
      *([ck] * n_pages), *([cv] * n_pages))


def _mix_and_prenorm(x_ref, ob_ref, oa_ref, wo_ref, g_post_ref, g_pre_ref):
    mix = jnp.dot(ob_ref[...].astype(BF16), wo_ref[0:DIFF_WIDTH, :], preferred_element_type=F32)
    mix = mix + jnp.dot(oa_ref[...].astype(BF16), wo_ref[DIFF_WIDTH:, :],
                        preferred_element_type=F32)
    x1 = x_ref[...] + _rms(mix, g_post_ref[...], NORM_EPS)
    return x1, _rms(x1, g_pre_ref[...], NORM_EPS)


def _conv_gate(cw, cb, h, h1, h2):
    return cb + cw[0:1] * h2 + cw[1:2] * h1 + cw[2:3] * h


def _ffn_prompt_kernel(x_ref, ob_ref, oa_ref, wo_ref, g_post_ref, g_pre_ref, g_out_ref,
                       wu_ref, cw_ref, cb_ref, wd_ref, y_ref, st_ref, perm, carry, act,
                       *, tm):
    nsub = tm // TS
    nv = TS // 8
    nj = D_FF // TF
    ncb = D_MODEL // LANE

    @pl.when(pl.program_id(1) == 0)
    def _():
        carry[...] = jnp.zeros_like(carry)

    def pre(t):
        rows = slice(t * TS, (t + 1) * TS)
        x1, xn2 = _mix_and_prenorm(x_ref.at[rows], ob_ref.at[rows], oa_ref.at[rows], wo_ref,
                                   g_post_ref, g_pre_ref)
        y_ref[rows, :] = x1
        for c in range(ncb):
            for s in range(8):
                perm[t, c, s * PITCH:s * PITCH + nv, :] = xn2[s * nv:(s + 1) * nv,
                                                              c * LANE:(c + 1) * LANE]
        return jnp.concatenate(
            [jnp.concatenate([perm[t, c, pl.ds(v, 8, stride=PITCH), :] for c in range(ncb)],
                             axis=1)
             for v in range(nv)], axis=0).astype(BF16)

    first_row = lax.broadcasted_iota(jnp.int32, (8, TF), 0) == 0

    def conv_ffn(t, xp):
        def up(j):
            return [jnp.dot(xp, wu_ref[:, half * D_FF + j * TF:half * D_FF + (j + 1) * TF],
                            preferred_element_type=F32) for half in range(2)]

        h_next = up(0)
        for j in range(nj):
            h_cur = h_next
            if j + 1 < nj:
                h_next = up(j + 1)
            conv = []
            for half in range(2):
                c0 = half * D_FF + j * TF
                c1 = c0 + TF
                h = h_cur[half]
                prev = carry[:, c0:c1]
                head = [jnp.where(first_row, pltpu.roll(prev[8 * i:8 * i + 8], 1, 0),
                                  pltpu.roll(h[TS - 16 + 8 * i:TS - 8 + 8 * i], 1, 0))
                        for i in range(2)]
                h1 = jnp.concatenate([head[1], h[:TS - 8]], axis=0)
                h2 = jnp.concatenate([head[0], head[1], h[:TS - 16]], axis=0)
                carry[:, c0:c1] = h[TS - 16:]
                conv.append(_conv_gate(cw_ref[:, c0:c1], cb_ref[:, c0:c1], h, h1, h2))
            act[t, :, j * TF:(j + 1) * TF] = (jax.nn.silu(conv[0]) * conv[1]).astype(BF16)

    def post(t):
        rows = slice(t * TS, (t + 1) * TS)
        fp = jnp.dot(act[t], wd_ref[...], preferred_element_type=F32)
        for v in range(nv):
            for c in range(ncb):
                perm[t, c, pl.ds(v, 8, stride=PITCH), :] = fp[8 * v:8 * v + 8,
                                                              c * LANE:(c + 1) * LANE]
        f = jnp.concatenate(
            [jnp.concatenate([perm[t, c, s * PITCH:s * PITCH + nv, :] for s in range(8)], axis=0)
             for c in range(ncb)], axis=1)
        y_ref[rows, :] = y_ref[rows, :] + _rms(f, g_out_ref[...], NORM_EPS)

    xps = [pre(t) for t in range(nsub)]
    for t in range(nsub):
        conv_ffn(t, xps[t])
        post(t)
    st_ref[0, 0:1, :] = carry[7:8, :]
    st_ref[0, 1:2, :] = carry[15:16, :]


def _ffn_prompt(x, o_b, o_a, w_out, g_post, g_pre, g_out, w_up, conv_w, conv_b, w_down,
                *, batch, seq):
    tm = TM_FFN
    nt = seq // tm
    row_spec = lambda w: pl.BlockSpec((tm, w), lambda b, i: (b * nt + i, 0))
    return pl.pallas_call(
        functools.partial(_ffn_prompt_kernel, tm=tm),
        grid=(batch, nt),
        in_specs=[
            row_spec(D_MODEL), row_spec(DIFF_WIDTH), row_spec(A_WIDTH),
            _const_spec((D_MODEL, D_MODEL)),
            _const_spec((1, D_MODEL)), _const_spec((1, D_MODEL)), _const_spec((1, D_MODEL)),
            _const_spec((D_MODEL, 2 * D_FF)),
            _const_spec((CONV_W, 2 * D_FF)),
            _const_spec((1, 2 * D_FF)),
            _const_spec((D_FF, D_MODEL)),
        ],
        out_specs=[row_spec(D_MODEL),
                   pl.BlockSpec((1, CONV_W - 1, 2 * D_FF), lambda b, i: (b, 0, 0))],
        out_shape=[jax.ShapeDtypeStruct((batch * seq, D_MODEL), F32),
                   jax.ShapeDtypeStruct((batch, CONV_W - 1, 2 * D_FF), F32)],
        scratch_shapes=[pltpu.VMEM((tm // TS, D_MODEL // LANE, 8 * PITCH, LANE), F32),
                        pltpu.VMEM((16, 2 * D_FF), F32),
                        pltpu.VMEM((tm // TS, TS, D_FF), BF16)],
        compiler_params=pltpu.CompilerParams(
            dimension_semantics=("arbitrary", "arbitrary"),
            vmem_limit_bytes=V7X_VMEM_LIMIT_BYTES),
        name="ffn_prompt",
    )(x, o_b, o_a, w_out, g_post, g_pre, g_out, w_up, conv_w, conv_b, w_down)


def _ffn_sample_kernel(x_ref, ob_ref, oa_ref, wo_ref, g_post_ref, g_pre_ref, g_out_ref,
                       wug_ref, wuu_ref, cwg_ref, cwu_ref, cbg_ref, cbu_ref, wd_ref,
                       stg_ref, stu_ref, y_ref, sog_ref, sou_ref, x1_scr, xn_scr, acc_scr,
                       *, dec_batch, dec_seq):
    j = pl.program_id(0)
    rows = dec_batch * dec_seq

    @pl.when(j == 0)
    def _():
        x1, xn2 = _mix_and_prenorm(x_ref, ob_ref, oa_ref, wo_ref, g_post_ref, g_pre_ref)
        x1_scr[...] = x1
        xn_scr[...] = xn2.astype(BF16)
        acc_scr[...] = jnp.zeros_like(acc_scr)

    xn2 = xn_scr[...]
    conv = []
    for wu_ref, cw_ref, cb_ref, st_ref, so_ref in ((wug_ref, cwg_ref, cbg_ref, stg_ref, sog_ref),
                                                   (wuu_ref, cwu_ref, cbu_ref, stu_ref, sou_ref)):
        h = jnp.dot(xn2, wu_ref[...], preferred_element_type=F32)
        s0 = st_ref[0]
        s1 = st_ref[1]
        h1 = jnp.concatenate([s1, h[:rows - dec_batch]], axis=0)
        h2 = jnp.concatenate([s0, s1, h[:rows - 2 * dec_batch]], axis=0)
        conv.append(_conv_gate(cw_ref[...], cb_ref[...], h, h1, h2))
        so_ref[0] = h[rows - 2 * dec_batch:rows - dec_batch]
        so_ref[1] = h[rows - dec_batch:]
    act = (jax.nn.silu(conv[0]) * conv[1]).astype(BF16)
    acc_scr[...] += jnp.dot(act, wd_ref[...], preferred_element_type=F32)

    @pl.when(j == pl.num_programs(0) - 1)
    def _():
        y_ref[...] = x1_scr[...] + _rms(acc_scr[...], g_out_ref[...], NORM_EPS)


def _ffn_sample(x, o_b, o_a, w_out, g_post, g_pre, g_out, w_up, conv_w, conv_b, w_down, state,
                *, dec_batch, dec_seq):
    rows = dec_batch * dec_seq
    nj = D_FF // TF
    ns = CONV_W - 1

    def col_tile(shape, ax, first):
        return pl.BlockSpec(shape, lambda j: tuple(first + j if a == ax else 0
                                                   for a in range(len(shape))))

    gate = functools.partial(col_tile, first=0)
    up = functools.partial(col_tile, first=nj)
    state_out = pl.BlockSpec((ns, dec_batch, TF), lambda j: (0, 0, j))
    sds = jax.ShapeDtypeStruct
    y, st_gate, st_up = pl.pallas_call(
        functools.partial(_ffn_sample_kernel, dec_batch=dec_batch, dec_seq=dec_seq),
        grid=(nj,),
        in_specs=[
            _const_spec((rows, D_MODEL)), _const_spec((rows, DIFF_WIDTH)),
            _const_spec((rows, A_WIDTH)),
            _const_spec((D_MODEL, D_MODEL)),
            _const_spec((1, D_MODEL)), _const_spec((1, D_MODEL)), _const_spec((1, D_MODEL)),
            gate((D_MODEL, TF), 1), up((D_MODEL, TF), 1),
            gate((CONV_W, TF), 1), up((CONV_W, TF), 1),
            gate((1, TF), 1), up((1, TF), 1),
            gate((TF, D_MODEL), 0),
            gate((ns, dec_batch, TF), 2), up((ns, dec_batch, TF), 2),
        ],
        out_specs=[pl.BlockSpec((rows, D_MODEL), lambda j: (0, 0)), state_out, state_out],
        out_shape=[sds((rows, D_MODEL), F32), sds((ns, dec_batch, D_FF), F32),
                   sds((ns, dec_batch, D_FF), F32)],
        scratch_shapes=[pltpu.VMEM((rows, D_MODEL), F32), pltpu.VMEM((rows, D_MODEL), BF16),
                        pltpu.VMEM((rows, D_MODEL), F32)],
        compiler_params=pltpu.CompilerParams(
            dimension_semantics=("arbitrary",), vmem_limit_bytes=V7X_VMEM_LIMIT_BYTES),
        name="ffn_sample",
    )(x, o_b, o_a, w_out, g_post, g_pre, g_out, w_up, w_up, conv_w, conv_w, conv_b, conv_b,
      w_down, state, state)
    return y, jnp.concatenate([st_gate, st_up], axis=-1)


def kernel(x_prompt, x_sample, cache_k, cache_v, state_conv, page_table, rel_bias, norm_mix_pre, norm_mix_post, norm_ffn_pre, norm_ffn_post, w_in, lambda_q1, lambda_k1, lambda_q2, lambda_k2, subln_gain, sgu_ln_gain, sgu_ln_bias, w_spatial, b_spatial, w_out, w_up, conv_w, conv_b, w_down):
    batch, seq, _ = x_prompt.shape
    dec_batch, dec_seq, _ = x_sample.shape
    assert seq % TQ == 0 and seq % TM_FFN == 0 and (batch * seq) % TM_IN == 0
    assert (dec_batch * dec_seq) % TM_IN == 0 and CHUNK % dec_seq == 0 and TQ == TK

    def to_pm(a):
        return a.reshape(dec_batch, dec_seq, -1).transpose(1, 0, 2).reshape(dec_batch * dec_seq, -1)

    def to_bm(a):
        return a.reshape(dec_seq, dec_batch, -1).transpose(1, 0, 2).reshape(dec_batch * dec_seq, -1)

    y_p = x_prompt.reshape(batch * seq, D_MODEL)
    y_s = x_sample.reshape(dec_batch * dec_seq, D_MODEL)
    reps = CHUNK // dec_seq
    cp, csm, svs = [], [], []
    kv_p = kv_s = None
    for l in range(DEPTH):
        lam_init = _lambda_init(l)
        row = lambda a: a[l].reshape(1, -1)
        w_in_l = w_in[l].astype(BF16)
        w_out_l = w_out[l].astype(BF16)
        w_up_l = w_up[l].astype(BF16)
        w_down_l = w_down[l].astype(BF16)
        lamv = jnp.stack([lambda_q1[l], lambda_k1[l], lambda_q2[l], lambda_k2[l]])
        ffn_args = (w_out_l, row(norm_mix_post), row(norm_ffn_pre), row(norm_ffn_post),
                    w_up_l, conv_w[l], row(conv_b), w_down_l)

        bsp_p = jnp.broadcast_to(b_spatial[l][:, :, None], (A_GROUPS, CHUNK, A_GROUP_DIM))
        qt, kf, vf, kb, vt, o_a = _inproj(
            y_p, row(norm_mix_pre), w_in_l, row(sgu_ln_gain), row(sgu_ln_bias),
            w_spatial[l], bsp_p, kv_p, layer=l, sample=False)
        kv_p = (kf, vf)
        o_b = _prompt_attn(qt, kb, vt, rel_bias, lamv, subln_gain[l],
                           batch=batch, seq=seq, lam_init=lam_init)
        y_p, conv_p = _ffn_prompt(y_p, o_b, o_a, *ffn_args, batch=batch, seq=seq)
        cp.append(conv_p)

        wsp_s = jnp.tile(w_spatial[l][:, :dec_seq, :dec_seq], (1, reps, reps))
        bsp_s = jnp.broadcast_to(jnp.tile(b_spatial[l][:, :dec_seq], (1, reps))[:, :, None],
                                 (A_GROUPS, CHUNK, A_GROUP_DIM))
        q, kf, vf, o_a, vn = _inproj(
            y_s, row(norm_mix_pre), w_in_l, row(sgu_ln_gain), row(sgu_ln_bias),
            wsp_s, bsp_s, kv_s, layer=l, sample=True)
        kv_s = (kf, vf)
        o_b = _sample_attn(q, kf, vf, cache_k, cache_v, page_table, rel_bias, lamv,
                           row(subln_gain), layer=l, dec_seq=dec_seq, lam_init=lam_init)
        state = state_conv[l].transpose(1, 0, 2)
        y_pm, conv_s = _ffn_sample(to_pm(y_s), to_pm(o_b), to_pm(o_a), *ffn_args, state,
                                   dec_batch=dec_batch, dec_seq=dec_seq)
        y_s = to_bm(y_pm)
        csm.append(conv_s.transpose(1, 0, 2))
        svs.append(vn.reshape(dec_batch, dec_seq, A_WIDTH))

    kv_shape_p = (DEPTH, batch, seq, H_DIFF, HEAD_W)
    kv_shape_s = (DEPTH, dec_batch, dec_seq, H_DIFF, HEAD_W)
    return (y_p.reshape(batch, seq, D_MODEL), y_s.reshape(dec_batch, dec_seq, D_MODEL),
            kv_p[0].reshape(kv_shape_p), kv_p[1].reshape(kv_shape_p), jnp.stack(cp),
            kv_s[0].reshape(kv_shape_s), kv_s[1].reshape(kv_shape_s),
            jnp.stack(csm), jnp.stack(svs))
```

```python
import functools
import math

import jax
import jax.numpy as jnp
import numpy as np
from jax import lax
from jax.experimental import pallas as pl
from jax.experimental.pallas import tpu as pltpu

F32 = jnp.float32
BF16 = jnp.bfloat16

D_MODEL = 1024
DEPTH = 2
PAGE_SIZE = 128
H_DIFF = 4
QK_HEAD_DIM = 64
V_HEAD_DIM = 128
HEAD_W = 2 * QK_HEAD_DIM
DIFF_WIDTH = H_DIFF * V_HEAD_DIM
A_GROUPS = 4
A_WIDTH = 512
A_GROUP_DIM = 128
CHUNK = 128
LANE = 128
Q_COLS = H_DIFF * HEAD_W
IN_COLS = 3 * Q_COLS + 2 * A_WIDTH
D_FF = 2816
CONV_W = 3
REL_BUCKETS = 32
REL_MAX_EXACT = 16
REL_MAX_DIST = 128
NORM_EPS = 1e-6
SUBLN_EPS = 1e-5
NEG_INF = -1e30
LOG2_E = math.log2(math.e)

V7X_VMEM_LIMIT_BYTES = 56 * 1024 * 1024

TM_IN = 1024
TQ = 256
TK = 256
DENOM_ROWS = 16
TM_FFN = 512
TS = 256
PITCH = TS // 8 + 8
TF = 256


def _lambda_init(layer):
    return 0.8 - 0.6 * math.exp(-0.3 * layer)


def _bucket_table(dist):
    n = np.maximum(dist, 0)
    nf = np.maximum(n, 1).astype(np.float32)
    large = REL_MAX_EXACT + (np.log(nf / REL_MAX_EXACT) / math.log(REL_MAX_DIST / REL_MAX_EXACT)
                             * (REL_BUCKETS - REL_MAX_EXACT)).astype(np.int32)
    large = np.minimum(large, REL_BUCKETS - 1)
    return np.where(dist >= 0, np.where(n < REL_MAX_EXACT, n, large), -1).astype(np.int32)


def _const_spec(shape):
    nd = len(shape)
    return pl.BlockSpec(shape, lambda *_: (0,) * nd, pipeline_mode=pl.Buffered(1))


def _layer_spec(shape, layer):
    nd = len(shape)
    return pl.BlockSpec((None,) + tuple(shape), lambda *_: (layer,) + (0,) * nd,
                        pipeline_mode=pl.Buffered(1))


def _rms(x, gain, eps):
    y = x * lax.rsqrt(jnp.mean(x * x, axis=-1, keepdims=True) + eps)
    return y * gain


def _nt_dot(a, b):
    return lax.dot_general(a, b, (((1,), (1,)), ((), ())), preferred_element_type=F32)


def _diff_lambda(lamv_ref, lam_init):
    p = lamv_ref[...]
    a = jnp.sum(p[0:1] * p[1:2], axis=-1, keepdims=True)
    b = jnp.sum(p[2:3] * p[3:4], axis=-1, keepdims=True)
    return jnp.exp(a) - jnp.exp(b) + lam_init


def _bias_from_buckets(idx, rb_ref, head, scale):
    far = rb_ref[REL_BUCKETS - 1, head]
    tbl = jnp.full(idx.shape, NEG_INF, F32)
    for b in range(REL_BUCKETS):
        tbl = jnp.where(idx == b, (rb_ref[b, head] - far) * scale, tbl)
    return tbl


def _inproj_kernel(x_ref, g_ref, w_ref, lng_ref, lnb_ref, wsp_ref, bsp_ref, *out_refs,
                   tm, sample, n_aliased):
    out_refs = out_refs[n_aliased:]
    if sample:
        q_ref, kf_ref, vf_ref, oa_ref, vn_ref = out_refs
    else:
        qt_ref, kf_ref, vf_ref, kb_ref, vt_ref, oa_ref = out_refs

    nsub = tm // TS
    a0 = 3 * Q_COLS

    def norm(t):
        return _rms(x_ref[t * TS:(t + 1) * TS, :], g_ref[...], NORM_EPS).astype(BF16)

    def proj(xn, c0, c1):
        return jnp.dot(xn, w_ref[:, c0:c1], preferred_element_type=F32)

    def qkv(t, xn):
        rows = slice(t * TS, (t + 1) * TS)
        q = proj(xn, 0, Q_COLS)
        k = proj(xn, Q_COLS, 2 * Q_COLS)
        v = proj(xn, 2 * Q_COLS, 3 * Q_COLS)
        for h in range(H_DIFF):
            head_rows = pl.ds(t * TS * H_DIFF + h, TS, stride=H_DIFF)
            kf_ref[head_rows, :] = k[:, h * HEAD_W:(h + 1) * HEAD_W]
            vf_ref[head_rows, :] = v[:, h * V_HEAD_DIM:(h + 1) * V_HEAD_DIM]
        if sample:
            q_ref[rows, :] = q * (QK_HEAD_DIM ** -0.5)
        else:
            qt_ref[:, rows] = (q * (QK_HEAD_DIM ** -0.5 * LOG2_E)).T.astype(BF16)
            vt_ref[:, rows] = v.T.astype(BF16)
            kb_ref[rows, :] = k.astype(BF16)

    def a_proj(xn):
        return proj(xn, a0, a0 + A_WIDTH), proj(xn, a0 + A_WIDTH, a0 + 2 * A_WIDTH)

    row = lax.broadcasted_iota(jnp.int32, (CHUNK, CHUNK), 0)
    col = lax.broadcasted_iota(jnp.int32, (CHUNK, CHUNK), 1)
    if sample:
        keep = jnp.logical_and((row >> 3) == (col >> 3), (col & 7) <= (row & 7))
    else:
        keep = col <= row
    w_mix = [jnp.where(keep, wsp_ref[g], 0.0).astype(BF16) for g in range(A_GROUPS)]

    def gate(t, u_raw, v_raw):
        u_a = jax.nn.gelu(u_raw)
        v_a = jax.nn.gelu(v_raw)
        xc = v_a - jnp.mean(v_a, axis=-1, keepdims=True)
        vn = xc * lax.rsqrt(jnp.mean(xc * xc, axis=-1, keepdims=True) + NORM_EPS)
        vn = vn * lng_ref[...] + lnb_ref[...]
        if sample:
            vn_ref[t * TS:(t + 1) * TS, :] = vn
        vnb = vn.astype(BF16)
        for g in range(A_GROUPS):
            c0, c1 = g * A_GROUP_DIM, (g + 1) * A_GROUP_DIM
            for c in range(TS // CHUNK):
                r0, r1 = c * CHUNK, (c + 1) * CHUNK
                s = jnp.dot(w_mix[g], vnb[r0:r1, c0:c1], preferred_element_type=F32) + bsp_ref[g]
                oa_ref[t * TS + r0:t * TS + r1, c0:c1] = (u_a[r0:r1, c0:c1] * s).astype(
                    oa_ref.dtype)

    xns = [norm(t) for t in range(nsub)]
    pending = None
    for t in range(nsub):
        qkv(t, xns[t])
        a_cur = a_proj(xns[t])
        if pending is not None:
            gate(t - 1, *pending)
        pending = a_cur
    gate(nsub - 1, *pending)


def _inproj(x, gain, w_in, ln_g, ln_b, wsp, bsp, kv_all, *, layer, sample):
    rows = x.shape[0]
    tm = TM_IN
    row_spec = lambda w: pl.BlockSpec((tm, w), lambda i: (i, 0))
    head_row_spec = pl.BlockSpec((None, tm * H_DIFF, HEAD_W), lambda i: (layer, i, 0))
    col_spec = pl.BlockSpec((Q_COLS, tm), lambda i: (0, i))
    in_specs = [
        row_spec(D_MODEL),
        _layer_spec((1, D_MODEL), layer),
        _layer_spec((D_MODEL, IN_COLS), layer),
        _layer_spec((1, A_WIDTH), layer),
        _layer_spec((1, A_WIDTH), layer),
        _layer_spec((A_GROUPS, CHUNK, CHUNK), layer),
        _layer_spec((A_GROUPS, CHUNK, A_GROUP_DIM), layer),
    ]
    sds = jax.ShapeDtypeStruct
    kv_shape = sds((DEPTH, rows * H_DIFF, HEAD_W), F32)
    aliased = () if kv_all is None else tuple(kv_all)
    n_in = len(in_specs)
    in_specs += [pl.BlockSpec(memory_space=pl.ANY)] * len(aliased)
    aliases = {n_in + i: 1 + i for i in range(len(aliased))}
    if sample:
        out_shape = [sds((rows, Q_COLS), F32), kv_shape, kv_shape,
                     sds((rows, A_WIDTH), F32), sds((rows, A_WIDTH), F32)]
        out_specs = [row_spec(Q_COLS), head_row_spec, head_row_spec,
                     row_spec(A_WIDTH), row_spec(A_WIDTH)]
    else:
        out_shape = [sds((Q_COLS, rows), BF16), kv_shape, kv_shape,
                     sds((rows, Q_COLS), BF16), sds((Q_COLS, rows), BF16),
                     sds((rows, A_WIDTH), BF16)]
        out_specs = [col_spec, head_row_spec, head_row_spec,
                     row_spec(Q_COLS), col_spec, row_spec(A_WIDTH)]
    return pl.pallas_call(
        functools.partial(_inproj_kernel, tm=tm, sample=sample, n_aliased=len(aliased)),
        grid=(rows // tm,),
        in_specs=in_specs,
        out_specs=out_specs,
        out_shape=out_shape,
        input_output_aliases=aliases,
        compiler_params=pltpu.CompilerParams(
            dimension_semantics=("arbitrary",), vmem_limit_bytes=V7X_VMEM_LIMIT_BYTES),
        name="inproj_sample" if sample else "inproj_prompt",
    )(x, gain, w_in, ln_g, ln_b, wsp, bsp, *aliased)


def _prompt_attn_kernel(rb_ref, lamv_ref, idx_ref, sg_ref, qt_ref, k_ref, vt_ref, o_ref,
                        bias_scr, qw_scr, m_scr, acc_scr, *, lam_init):
    qi = pl.program_id(1)

    @pl.when(jnp.logical_and(pl.program_id(0) == 0, qi == 0))
    def _():
        for h in range(H_DIFF):
            for t in range(2):
                bias_scr[h, t] = _bias_from_buckets(idx_ref[t], rb_ref, h, LOG2_E)

    drow = lax.broadcasted_iota(jnp.int32, (HEAD_W, TQ), 0)
    first_map = drow < QK_HEAD_DIM
    for h in range(H_DIFF):
        qh = qt_ref[h * HEAD_W:(h + 1) * HEAD_W, :]
        zero = jnp.zeros_like(qh)
        qw_scr[h] = jnp.concatenate(
            [jnp.where(first_map, qh, zero), jnp.where(first_map, zero, qh)], axis=1)
    m_scr[...] = jnp.full_like(m_scr, NEG_INF)
    acc_scr[...] = jnp.zeros_like(acc_scr)
    ones_rows = jnp.ones((DENOM_ROWS, TK), BF16)

    def block(j, kind):
        r0 = pl.multiple_of(j * TK, TK)

        def scores(h):
            kj = k_ref[pl.ds(r0, TK), h * HEAD_W:(h + 1) * HEAD_W]
            return jnp.dot(kj, qw_scr[h], preferred_element_type=F32)

        st_next = scores(0)
        for h in range(H_DIFF):
            c0, c1 = h * HEAD_W, (h + 1) * HEAD_W
            st = st_next
            if h + 1 < H_DIFF:
                st_next = scores(h + 1)
            if kind is not None:
                bias = bias_scr[h, kind]
                st = jnp.concatenate([bias, bias], axis=1) + st
            m_old = m_scr[h]
            m_new = jnp.maximum(m_old, jnp.max(st, axis=0, keepdims=True))
            alpha = jnp.exp2(m_old - m_new)
            pt = jnp.exp2(st - m_new)
            vtj = jnp.concatenate([vt_ref[c0:c1, pl.ds(r0, TK)], ones_rows], axis=0)
            acc_scr[h] = alpha * acc_scr[h] + jnp.dot(vtj, pt.astype(BF16),
                                                      preferred_element_type=F32)
            m_scr[h] = m_new

    def far_block(j, carry):
        block(j, None)
        return carry

    def near_block(j, carry):
        block(j, qi - j)
        return carry

    first_near = jnp.maximum(qi - 1, 0)
    lax.fori_loop(0, first_near, far_block, 0)
    lax.fori_loop(first_near, qi + 1, near_block, 0)

    lam = _diff_lambda(lamv_ref, lam_init)
    for h in range(H_DIFF):
        acc = acc_scr[h, 0:V_HEAD_DIM, :]
        rl = 1.0 / acc_scr[h, V_HEAD_DIM:V_HEAD_DIM + 1, :]
        ot = acc[:, :TQ] * rl[:, :TQ] - lam * (acc[:, TQ:] * rl[:, TQ:])
        ot = ot * lax.rsqrt(jnp.mean(ot * ot, axis=0, keepdims=True) + SUBLN_EPS)
        ot = ot * sg_ref[...] * (1.0 - lam_init)
        o_ref[:, h * V_HEAD_DIM:(h + 1) * V_HEAD_DIM] = ot.T.astype(o_ref.dtype)


def _prompt_attn(qt, k, vt, rel_bias, lamv, sg, *, layer, batch, seq, lam_init):
    nq = seq // TQ
    kk = np.arange(TK)[:, None]
    qq = np.arange(TQ)[None, :]
    idx = jnp.asarray(np.stack([_bucket_table(qq - kk), _bucket_table(TQ + qq - kk)]))
    return pl.pallas_call(
        functools.partial(_prompt_attn_kernel, lam_init=lam_init),
        grid=(batch, nq),
        in_specs=[
            pl.BlockSpec(memory_space=pltpu.SMEM),
            _layer_spec((4, QK_HEAD_DIM), layer),
            _const_spec((2, TK, TQ)),
            _layer_spec((V_HEAD_DIM, TQ), layer),
            pl.BlockSpec((Q_COLS, TQ), lambda b, i: (0, b * nq + i)),
            pl.BlockSpec((seq, Q_COLS), lambda b, i: (b, 0)),
            pl.BlockSpec((Q_COLS, seq), lambda b, i: (0, b)),
        ],
        out_specs=pl.BlockSpec((TQ, DIFF_WIDTH), lambda b, i: (b * nq + i, 0)),
        out_shape=jax.ShapeDtypeStruct((batch * seq, DIFF_WIDTH), BF16),
        scratch_shapes=[pltpu.VMEM((H_DIFF, 2, TK, TQ), F32),
                        pltpu.VMEM((H_DIFF, HEAD_W, 2 * TQ), BF16),
                        pltpu.VMEM((H_DIFF, 1, 2 * TQ), F32),
                        pltpu.VMEM((H_DIFF, V_HEAD_DIM + DENOM_ROWS, 2 * TQ), F32)],
        compiler_params=pltpu.CompilerParams(
            dimension_semantics=("arbitrary", "arbitrary"),
            vmem_limit_bytes=V7X_VMEM_LIMIT_BYTES),
        name="prompt_attn",
    )(rel_bias, lamv, idx, sg, qt, k, vt)


def _sample_attn_kernel(pt_ref, rb_ref, lamv_ref, idx_ref, sg_ref, q_ref, kn_ref, vn_ref, *rest,
                        n_pages, dec_seq, lam_init):
    k_refs = rest[:n_pages]
    v_refs = rest[n_pages:2 * n_pages]
    o_ref, s_scr, kpad, vpad, bias_scr = rest[2 * n_pages:]
    del pt_ref
    rows = 2 * dec_seq
    past = n_pages * PAGE_SIZE

    @pl.when(pl.program_id(0) == 0)
    def _():
        for h in range(H_DIFF):
            for t in range(2):
                bias_scr[t, h * rows:(h + 1) * rows, :] = _bias_from_buckets(
                    idx_ref[t], rb_ref, h, 1.0)

    def head_rows(ref, h, n):
        return ref[pl.ds(h, n, stride=H_DIFF), :]

    kpad[...] = jnp.zeros_like(kpad)
    vpad[...] = jnp.zeros_like(vpad)
    for h in range(H_DIFF):
        kpad[h, 0:dec_seq, :] = head_rows(kn_ref, h, dec_seq)
        vpad[h, 0:dec_seq, :] = head_rows(vn_ref, h, dec_seq)

    lane = lax.broadcasted_iota(jnp.int32, (dec_seq, HEAD_W), 1)
    first_map = lane < QK_HEAD_DIM
    for h in range(H_DIFF):
        qh = q_ref[:, h * HEAD_W:(h + 1) * HEAD_W]
        qm = jnp.concatenate([jnp.where(first_map, qh, 0.0), jnp.where(first_map, 0.0, qh)],
                             axis=0).astype(BF16)
        r0, r1 = h * rows, (h + 1) * rows
        for p in range(0, n_pages, 2):
            k2 = jnp.concatenate([head_rows(k_refs[p], h, PAGE_SIZE),
                                  head_rows(k_refs[p + 1], h, PAGE_SIZE)], axis=0).astype(BF16)
            s_scr[r0:r1, p * PAGE_SIZE:(p + 2) * PAGE_SIZE] = _nt_dot(qm, k2)
        s_scr[r0:r1, past - PAGE_SIZE:past] += bias_scr[0, r0:r1, :]
        s_scr[r0:r1, past:] = _nt_dot(qm, kpad[h].astype(BF16)) + bias_scr[1, r0:r1, :]

    s = s_scr[...]
    e = jnp.exp(s - jnp.max(s, axis=-1, keepdims=True))
    pn = e / jnp.sum(e, axis=-1, keepdims=True)
    lam = _diff_lambda(lamv_ref, lam_init)
    for h in range(H_DIFF):
        r0 = h * rows
        a = (pn[r0:r0 + dec_seq] - lam * pn[r0 + dec_seq:r0 + rows]).astype(BF16)
        acc = jnp.dot(a[:, past:], vpad[h].astype(BF16), preferred_element_type=F32)
        for p in range(n_pages):
            acc = acc + jnp.dot(a[:, p * PAGE_SIZE:(p + 1) * PAGE_SIZE],
                                head_rows(v_refs[p], h, PAGE_SIZE).astype(BF16),
                                preferred_element_type=F32)
        o_ref[:, h * V_HEAD_DIM:(h + 1) * V_HEAD_DIM] = (
            _rms(acc, sg_ref[...], SUBLN_EPS) * (1.0 - lam_init))


def _sample_attn(q, k_new, v_new, cache_k, cache_v, page_table, rel_bias, lamv, subln_gain,
                 *, layer, dec_seq, lam_init):
    dec_batch, n_pages = page_table.shape
    assert n_pages % 2 == 0
    n_pool = cache_k.shape[1]
    page_rows = PAGE_SIZE * H_DIFF
    ck = cache_k.reshape(DEPTH, n_pool, page_rows, HEAD_W)
    cv = cache_v.reshape(DEPTH, n_pool, page_rows, V_HEAD_DIM)
    past_len = n_pages * PAGE_SIZE
    rows = 2 * dec_seq
    i = (np.arange(rows) % dec_seq)[:, None]
    c = np.arange(PAGE_SIZE)[None, :]
    idx_last = _bucket_table(past_len + i - (past_len - PAGE_SIZE + c))
    idx_new = np.where(c < dec_seq, _bucket_table(i - c), -1).astype(np.int32)
    idx = jnp.asarray(np.stack([idx_last, idx_new]))
    nkeys = past_len + PAGE_SIZE

    def page_spec(p):
        return pl.BlockSpec((None, None, page_rows, HEAD_W),
                            lambda b, pt: (layer, pt[b, p], 0, 0))

    const = lambda shape: pl.BlockSpec(shape, lambda b, pt: (0,) * len(shape))
    per_layer = lambda shape: pl.BlockSpec((None,) + shape,
                                           lambda b, pt: (layer,) + (0,) * len(shape))
    row_spec = pl.BlockSpec((dec_seq, Q_COLS), lambda b, pt: (b, 0))
    new_spec = pl.BlockSpec((None, dec_seq * H_DIFF, HEAD_W), lambda b, pt: (layer, b, 0))
    grid_spec = pltpu.PrefetchScalarGridSpec(
        num_scalar_prefetch=1,
        grid=(dec_batch,),
        in_specs=[
            pl.BlockSpec(memory_space=pltpu.SMEM),
            per_layer((4, QK_HEAD_DIM)),
            const((2, rows, PAGE_SIZE)),
            per_layer((1, V_HEAD_DIM)),
            row_spec, new_spec, new_spec,
        ] + [page_spec(p) for p in range(n_pages)] * 2,
        out_specs=row_spec,
        scratch_shapes=[
            pltpu.VMEM((H_DIFF * rows, nkeys), F32),
            pltpu.VMEM((H_DIFF, PAGE_SIZE, HEAD_W), F32),
            pltpu.VMEM((H_DIFF, PAGE_SIZE, V_HEAD_DIM), F32),
            pltpu.VMEM((2, H_DIFF * rows, PAGE_SIZE), F32),
        ],
    )
    return pl.pallas_call(
        functools.partial(_sample_attn_kernel, n_pages=n_pages, dec_seq=dec_seq, lam_init=lam_init),
        grid_spec=grid_spec,
        out_shape=jax.ShapeDtypeStruct((dec_batch * dec_seq, DIFF_WIDTH), F32),
        compiler_params=pltpu.CompilerParams(
            dimension_semantics=("arbitrary",), vmem_limit_bytes=V7X_VMEM_LIMIT_BYTES),
        name="sample_attn",
    )(page_table, rel_bias, lamv, idx, subln_gain, q, k_new, v_new,
      *([ck] * n_pages), *([cv] * n_pages))


def _mix_and_prenorm(x_ref, ob_ref, oa_ref, wo_ref, g_post_ref, g_pre_ref):
    mix = jnp.dot(ob_ref[...].astype(BF16), wo_ref[0:DIFF_WIDTH, :], preferred_element_type=F32)
    mix = mix + jnp.dot(oa_ref[...].astype(BF16), wo_ref[DIFF_WIDTH:, :],
                        preferred_element_type=F32)
    x1 = x_ref[...] + _rms(mix, g_post_ref[...], NORM_EPS)
    return x1, _rms(x1, g_pre_ref[...], NORM_EPS)


def _conv_gate(cw, cb, h, h1, h2):
    return cb + cw[0:1] * h2 + cw[1:2] * h1 + cw[2:3] * h


def _ffn_prompt_kernel(x_ref, ob_ref, oa_ref, wo_ref, g_post_ref, g_pre_ref, g_out_ref,
                       wu_ref, cw_ref, cb_ref, wd_ref, y_ref, st_ref, perm, carry, act,
                       *, tm):
    nsub = tm // TS
    nv = TS // 8
    nj = D_FF // TF
    ncb = D_MODEL // LANE

    @pl.when(pl.program_id(1) == 0)
    def _():
        carry[...] = jnp.zeros_like(carry)

    def pre(t):
        rows = slice(t * TS, (t + 1) * TS)
        x1, xn2 = _mix_and_prenorm(x_ref.at[rows], ob_ref.at[rows], oa_ref.at[rows], wo_ref,
                                   g_post_ref, g_pre_ref)
        y_ref[rows, :] = x1
        for c in range(ncb):
            for s in range(8):
                perm[t, c, s * PITCH:s * PITCH + nv, :] = xn2[s * nv:(s + 1) * nv,
                                                              c * LANE:(c + 1) * LANE]
        return jnp.concatenate(
            [jnp.concatenate([perm[t, c, pl.ds(v, 8, stride=PITCH), :] for c in range(ncb)],
                             axis=1)
             for v in range(nv)], axis=0).astype(BF16)

    first_row = lax.broadcasted_iota(jnp.int32, (8, TF), 0) == 0

    def conv_ffn(t, xp):
        def up(j):
            return [jnp.dot(xp, wu_ref[:, half * D_FF + j * TF:half * D_FF + (j + 1) * TF],
                            preferred_element_type=F32) for half in range(2)]

        h_next = up(0)
        for j in range(nj):
            h_cur = h_next
            if j + 1 < nj:
                h_next = up(j + 1)
            conv = []
            for half in range(2):
                c0 = half * D_FF + j * TF
                c1 = c0 + TF
                h = h_cur[half]
                prev = carry[:, c0:c1]
                head = [jnp.where(first_row, pltpu.roll(prev[8 * i:8 * i + 8], 1, 0),
                                  pltpu.roll(h[TS - 16 + 8 * i:TS - 8 + 8 * i], 1, 0))
                        for i in range(2)]
                h1 = jnp.concatenate([head[1], h[:TS - 8]], axis=0)
                h2 = jnp.concatenate([head[0], head[1], h[:TS - 16]], axis=0)
                carry[:, c0:c1] = h[TS - 16:]
                conv.append(_conv_gate(cw_ref[:, c0:c1], cb_ref[:, c0:c1], h, h1, h2))
            act[t, :, j * TF:(j + 1) * TF] = (jax.nn.silu(conv[0]) * conv[1]).astype(BF16)

    def post(t):
        rows = slice(t * TS, (t + 1) * TS)
        fp = jnp.dot(act[t], wd_ref[...], preferred_element_type=F32)
        for v in range(nv):
            for c in range(ncb):
                perm[t, c, pl.ds(v, 8, stride=PITCH), :] = fp[8 * v:8 * v + 8,
                                                              c * LANE:(c + 1) * LANE]
        f = jnp.concatenate(
            [jnp.concatenate([perm[t, c, s * PITCH:s * PITCH + nv, :] for s in range(8)], axis=0)
             for c in range(ncb)], axis=1)
        y_ref[rows, :] = y_ref[rows, :] + _rms(f, g_out_ref[...], NORM_EPS)

    xps = [pre(t) for t in range(nsub)]
    for t in range(nsub):
        conv_ffn(t, xps[t])
        post(t)
    st_ref[0, 0:1, :] = carry[7:8, :]
    st_ref[0, 1:2, :] = carry[15:16, :]


def _ffn_prompt(x, o_b, o_a, w_out, g_post, g_pre, g_out, w_up, conv_w, conv_b, w_down,
                *, layer, batch, seq):
    tm = TM_FFN
    nt = seq // tm
    row_spec = lambda w: pl.BlockSpec((tm, w), lambda b, i: (b * nt + i, 0))
    per_layer = functools.partial(_layer_spec, layer=layer)
    return pl.pallas_call(
        functools.partial(_ffn_prompt_kernel, tm=tm),
        grid=(batch, nt),
        in_specs=[
            row_spec(D_MODEL), row_spec(DIFF_WIDTH), row_spec(A_WIDTH),
            per_layer((D_MODEL, D_MODEL)),
            per_layer((1, D_MODEL)), per_layer((1, D_MODEL)), per_layer((1, D_MODEL)),
            per_layer((D_MODEL, 2 * D_FF)),
            per_layer((CONV_W, 2 * D_FF)),
            per_layer((1, 2 * D_FF)),
            per_layer((D_FF, D_MODEL)),
        ],
        out_specs=[row_spec(D_MODEL),
                   pl.BlockSpec((1, CONV_W - 1, 2 * D_FF), lambda b, i: (b, 0, 0))],
        out_shape=[jax.ShapeDtypeStruct((batch * seq, D_MODEL), F32),
                   jax.ShapeDtypeStruct((batch, CONV_W - 1, 2 * D_FF), F32)],
        scratch_shapes=[pltpu.VMEM((tm // TS, D_MODEL // LANE, 8 * PITCH, LANE), F32),
                        pltpu.VMEM((16, 2 * D_FF), F32),
                        pltpu.VMEM((tm // TS, TS, D_FF), BF16)],
        compiler_params=pltpu.CompilerParams(
            dimension_semantics=("arbitrary", "arbitrary"),
            vmem_limit_bytes=V7X_VMEM_LIMIT_BYTES),
        name="ffn_prompt",
    )(x, o_b, o_a, w_out, g_post, g_pre, g_out, w_up, conv_w, conv_b, w_down)


def _ffn_sample_kernel(x_ref, ob_ref, oa_ref, wo_ref, g_post_ref, g_pre_ref, g_out_ref,
                       wug_ref, wuu_ref, cwg_ref, cwu_ref, cbg_ref, cbu_ref, wd_ref,
                       stg_ref, stu_ref, y_ref, sog_ref, sou_ref, x1_scr, xn_scr, acc_scr,
                       *, dec_batch, dec_seq):
    j = pl.program_id(0)
    rows = dec_batch * dec_seq

    @pl.when(j == 0)
    def _():
        x1, xn2 = _mix_and_prenorm(x_ref, ob_ref, oa_ref, wo_ref, g_post_ref, g_pre_ref)
        x1_scr[...] = x1
        xn_scr[...] = xn2.astype(BF16)
        acc_scr[...] = jnp.zeros_like(acc_scr)

    xn2 = xn_scr[...]
    conv = []
    for wu_ref, cw_ref, cb_ref, st_ref, so_ref in ((wug_ref, cwg_ref, cbg_ref, stg_ref, sog_ref),
                                                   (wuu_ref, cwu_ref, cbu_ref, stu_ref, sou_ref)):
        h = jnp.dot(xn2, wu_ref[...], preferred_element_type=F32)
        s0 = st_ref[0]
        s1 = st_ref[1]
        h1 = jnp.concatenate([s1, h[:rows - dec_batch]], axis=0)
        h2 = jnp.concatenate([s0, s1, h[:rows - 2 * dec_batch]], axis=0)
        conv.append(_conv_gate(cw_ref[...], cb_ref[...], h, h1, h2))
        so_ref[0] = h[rows - 2 * dec_batch:rows - dec_batch]
        so_ref[1] = h[rows - dec_batch:]
    act = (jax.nn.silu(conv[0]) * conv[1]).astype(BF16)
    acc_scr[...] += jnp.dot(act, wd_ref[...], preferred_element_type=F32)

    @pl.when(j == pl.num_programs(0) - 1)
    def _():
        y_ref[...] = x1_scr[...] + _rms(acc_scr[...], g_out_ref[...], NORM_EPS)


def _ffn_sample(x, o_b, o_a, w_out, g_post, g_pre, g_out, w_up, conv_w, conv_b, w_down, state,
                *, layer, dec_batch, dec_seq):
    rows = dec_batch * dec_seq
    nj = D_FF // TF
    ns = CONV_W - 1

    def col_tile(shape, ax, first):
        return pl.BlockSpec((None,) + shape,
                            lambda j: (layer,) + tuple(first + j if a == ax else 0
                                                       for a in range(len(shape))))

    gate = functools.partial(col_tile, first=0)
    up = functools.partial(col_tile, first=nj)
    per_layer = functools.partial(_layer_spec, layer=layer)
    state_out = pl.BlockSpec((ns, dec_batch, TF), lambda j: (0, 0, j))
    sds = jax.ShapeDtypeStruct
    y, st_gate, st_up = pl.pallas_call(
        functools.partial(_ffn_sample_kernel, dec_batch=dec_batch, dec_seq=dec_seq),
        grid=(nj,),
        in_specs=[
            _const_spec((rows, D_MODEL)), _const_spec((rows, DIFF_WIDTH)),
            _const_spec((rows, A_WIDTH)),
            per_layer((D_MODEL, D_MODEL)),
            per_layer((1, D_MODEL)), per_layer((1, D_MODEL)), per_layer((1, D_MODEL)),
            gate((D_MODEL, TF), 1), up((D_MODEL, TF), 1),
            gate((CONV_W, TF), 1), up((CONV_W, TF), 1),
            gate((1, TF), 1), up((1, TF), 1),
            gate((TF, D_MODEL), 0),
            gate((ns, dec_batch, TF), 2), up((ns, dec_batch, TF), 2),
        ],
        out_specs=[pl.BlockSpec((rows, D_MODEL), lambda j: (0, 0)), state_out, state_out],
        out_shape=[sds((rows, D_MODEL), F32), sds((ns, dec_batch, D_FF), F32),
                   sds((ns, dec_batch, D_FF), F32)],
        scratch_shapes=[pltpu.VMEM((rows, D_MODEL), F32), pltpu.VMEM((rows, D_MODEL), BF16),
                        pltpu.VMEM((rows, D_MODEL), F32)],
        compiler_params=pltpu.CompilerParams(
            dimension_semantics=("arbitrary",), vmem_limit_bytes=V7X_VMEM_LIMIT_BYTES),
        name="ffn_sample",
    )(x, o_b, o_a, w_out, g_post, g_pre, g_out, w_up, w_up, conv_w, conv_w, conv_b, conv_b,
      w_down, state, state)
    return y, jnp.concatenate([st_gate, st_up], axis=-1)


def kernel(x_prompt, x_sample, cache_k, cache_v, state_conv, page_table, rel_bias, norm_mix_pre, norm_mix_post, norm_ffn_pre, norm_ffn_post, w_in, lambda_q1, lambda_k1, lambda_q2, lambda_k2, subln_gain, sgu_ln_gain, sgu_ln_bias, w_spatial, b_spatial, w_out, w_up, conv_w, conv_b, w_down):
    batch, seq, _ = x_prompt.shape
    dec_batch, dec_seq, _ = x_sample.shape
    assert seq % TQ == 0 and seq % TM_FFN == 0 and (batch * seq) % TM_IN == 0
    assert (dec_batch * dec_seq) % TM_IN == 0 and CHUNK % dec_seq == 0 and TQ == TK

    def to_pm(a):
        return a.reshape(dec_batch, dec_seq, -1).transpose(1, 0, 2).reshape(dec_batch * dec_seq, -1)

    def to_bm(a):
        return a.reshape(dec_seq, dec_batch, -1).transpose(1, 0, 2).reshape(dec_batch * dec_seq, -1)

    y_p = x_prompt.reshape(batch * seq, D_MODEL)
    y_s = x_sample.reshape(dec_batch * dec_seq, D_MODEL)
    reps = CHUNK // dec_seq

    rows3 = lambda a: a.reshape(DEPTH, 1, -1)
    w_in_b, w_out_b = w_in.astype(BF16), w_out.astype(BF16)
    w_up_b, w_down_b = w_up.astype(BF16), w_down.astype(BF16)
    lamv = jnp.stack([lambda_q1, lambda_k1, lambda_q2, lambda_k2], axis=1)
    sg_rows = jnp.broadcast_to(subln_gain[:, :, None], (DEPTH, V_HEAD_DIM, TQ))
    ffn_args = (w_out_b, rows3(norm_mix_post), rows3(norm_ffn_pre), rows3(norm_ffn_post),
                w_up_b, conv_w, rows3(conv_b), w_down_b)
    in_args = (rows3(norm_mix_pre), w_in_b, rows3(sgu_ln_gain), rows3(sgu_ln_bias))
    gate_shape = (DEPTH, A_GROUPS, CHUNK, A_GROUP_DIM)
    bsp_p = jnp.broadcast_to(b_spatial[:, :, :, None], gate_shape)
    wsp_s = jnp.tile(w_spatial[:, :, :dec_seq, :dec_seq], (1, 1, reps, reps))
    bsp_s = jnp.broadcast_to(jnp.tile(b_spatial[:, :, :dec_seq], (1, 1, reps))[:, :, :, None],
                             gate_shape)
    state = state_conv.transpose(0, 2, 1, 3)

    cp, csm, svs = [], [], []
    kv_p = kv_s = None
    for l in range(DEPTH):
        lam_init = _lambda_init(l)

        qt, kf, vf, kb, vt, o_a = _inproj(y_p, *in_args, w_spatial, bsp_p, kv_p,
                                          layer=l, sample=False)
        kv_p = (kf, vf)
        o_b = _prompt_attn(qt, kb, vt, rel_bias, lamv, sg_rows,
                           layer=l, batch=batch, seq=seq, lam_init=lam_init)
        y_p, conv_p = _ffn_prompt(y_p, o_b, o_a, *ffn_args, layer=l, batch=batch, seq=seq)
        cp.append(conv_p)

        q, kf, vf, o_a, vn = _inproj(y_s, *in_args, wsp_s, bsp_s, kv_s, layer=l, sample=True)
        kv_s = (kf, vf)
        o_b = _sample_attn(q, kf, vf, cache_k, cache_v, page_table, rel_bias, lamv,
                           rows3(subln_gain), layer=l, dec_seq=dec_seq, lam_init=lam_init)
        y_pm, conv_s = _ffn_sample(to_pm(y_s), to_pm(o_b), to_pm(o_a), *ffn_args, state,
                                   layer=l, dec_batch=dec_batch, dec_seq=dec_seq)
        y_s = to_bm(y_pm)
        csm.append(conv_s.transpose(1, 0, 2))
        svs.append(vn.reshape(dec_batch, dec_seq, A_WIDTH))

    kv_shape_p = (DEPTH, batch, seq, H_DIFF, HEAD_W)
    kv_shape_s = (DEPTH, dec_batch, dec_seq, H_DIFF, HEAD_W)
    return (y_p.reshape(batch, seq, D_MODEL), y_s.reshape(dec_batch, dec_seq, D_MODEL),
            kv_p[0].reshape(kv_shape_p), kv_p[1].reshape(kv_shape_p), jnp.stack(cp),
            kv_s[0].reshape(kv_shape_s), kv_s[1].reshape(kv_shape_s),
            jnp.stack(csm), jnp.stack(svs))
```

```python
import functools
import math

import jax
import jax.numpy as jnp
import numpy as np
from jax import lax
from jax.experimental import pallas as pl
from jax.experimental.pallas import tpu as pltpu

F32 = jnp.float32
BF16 = jnp.bfloat16

D_MODEL = 1024
DEPTH = 2
PAGE_SIZE = 128
H_DIFF = 4
QK_HEAD_DIM = 64
V_HEAD_DIM = 128
HEAD_W = 2 * QK_HEAD_DIM
DIFF_WIDTH = H_DIFF * V_HEAD_DIM
A_GROUPS = 4
A_WIDTH = 512
A_GROUP_DIM = 128
CHUNK = 128
LANE = 128
Q_COLS = H_DIFF * HEAD_W
IN_COLS = 3 * Q_COLS + 2 * A_WIDTH
D_FF = 2816
CONV_W = 3
REL_BUCKETS = 32
REL_MAX_EXACT = 16
REL_MAX_DIST = 128
NORM_EPS = 1e-6
SUBLN_EPS = 1e-5
NEG_INF = -1e30
LOG2_E = math.log2(math.e)

V7X_VMEM_LIMIT_BYTES = 56 * 1024 * 1024

TM_IN = 1024
TQ = 256
TK = 256
DENOM_ROWS = 16
TM_FFN = 512
TS = 256
PITCH = TS // 8 + 8
TF = 256


def _lambda_init(layer):
    return 0.8 - 0.6 * math.exp(-0.3 * layer)


def _bucket_table(dist):
    n = np.maximum(dist, 0)
    nf = np.maximum(n, 1).astype(np.float32)
    large = REL_MAX_EXACT + (np.log(nf / REL_MAX_EXACT) / math.log(REL_MAX_DIST / REL_MAX_EXACT)
                             * (REL_BUCKETS - REL_MAX_EXACT)).astype(np.int32)
    large = np.minimum(large, REL_BUCKETS - 1)
    return np.where(dist >= 0, np.where(n < REL_MAX_EXACT, n, large), -1).astype(np.int32)


def _const_spec(shape):
    nd = len(shape)
    return pl.BlockSpec(shape, lambda *_: (0,) * nd, pipeline_mode=pl.Buffered(1))


def _layer_spec(shape, layer):
    nd = len(shape)
    return pl.BlockSpec((None,) + tuple(shape), lambda *_: (layer,) + (0,) * nd,
                        pipeline_mode=pl.Buffered(1))


def _rms(x, gain, eps):
    y = x * lax.rsqrt(jnp.mean(x * x, axis=-1, keepdims=True) + eps)
    return y * gain


def _nt_dot(a, b):
    return lax.dot_general(a, b, (((1,), (1,)), ((), ())), preferred_element_type=F32)


def _diff_lambda(lamv_ref, lam_init):
    p = lamv_ref[...]
    a = jnp.sum(p[0:1] * p[1:2], axis=-1, keepdims=True)
    b = jnp.sum(p[2:3] * p[3:4], axis=-1, keepdims=True)
    return jnp.exp(a) - jnp.exp(b) + lam_init


def _bias_from_buckets(idx, rb_ref, head, scale):
    far = rb_ref[REL_BUCKETS - 1, head]
    tbl = jnp.full(idx.shape, NEG_INF, F32)
    for b in range(REL_BUCKETS):
        tbl = jnp.where(idx == b, (rb_ref[b, head] - far) * scale, tbl)
    return tbl


def _inproj_kernel(x_ref, g_ref, w_ref, lng_ref, lnb_ref, wsp_ref, bsp_ref, *out_refs,
                   tm, sample, n_aliased):
    out_refs = out_refs[n_aliased:]
    if sample:
        q_ref, kf_ref, vf_ref, oa_ref, vn_ref = out_refs
    else:
        qt_ref, kf_ref, vf_ref, kb_ref, vt_ref, oa_ref = out_refs

    nsub = tm // TS
    a0 = 3 * Q_COLS

    def norm(t):
        return _rms(x_ref[t * TS:(t + 1) * TS, :], g_ref[...], NORM_EPS).astype(BF16)

    def proj(xn, c0, c1):
        return jnp.dot(xn, w_ref[:, c0:c1], preferred_element_type=F32)

    def qkv(t, xn):
        rows = slice(t * TS, (t + 1) * TS)
        q = proj(xn, 0, Q_COLS)
        k = proj(xn, Q_COLS, 2 * Q_COLS)
        v = proj(xn, 2 * Q_COLS, 3 * Q_COLS)
        for h in range(H_DIFF):
            head_rows = pl.ds(t * TS * H_DIFF + h, TS, stride=H_DIFF)
            kf_ref[head_rows, :] = k[:, h * HEAD_W:(h + 1) * HEAD_W]
            vf_ref[head_rows, :] = v[:, h * V_HEAD_DIM:(h + 1) * V_HEAD_DIM]
        if sample:
            q_ref[rows, :] = q * (QK_HEAD_DIM ** -0.5)
        else:
            qt_ref[:, rows] = (q * (QK_HEAD_DIM ** -0.5 * LOG2_E)).T.astype(BF16)
            vt_ref[:, rows] = v.T.astype(BF16)
            kb_ref[rows, :] = k.astype(BF16)

    def a_proj(xn):
        return proj(xn, a0, a0 + A_WIDTH), proj(xn, a0 + A_WIDTH, a0 + 2 * A_WIDTH)

    row = lax.broadcasted_iota(jnp.int32, (CHUNK, CHUNK), 0)
    col = lax.broadcasted_iota(jnp.int32, (CHUNK, CHUNK), 1)
    if sample:
        keep = jnp.logical_and((row >> 3) == (col >> 3), (col & 7) <= (row & 7))
    else:
        keep = col <= row
    w_mix = [jnp.where(keep, wsp_ref[g], 0.0).astype(BF16) for g in range(A_GROUPS)]

    def gate(t, u_raw, v_raw):
        u_a = jax.nn.gelu(u_raw)
        v_a = jax.nn.gelu(v_raw)
        xc = v_a - jnp.mean(v_a, axis=-1, keepdims=True)
        vn = xc * lax.rsqrt(jnp.mean(xc * xc, axis=-1, keepdims=True) + NORM_EPS)
        vn = vn * lng_ref[...] + lnb_ref[...]
        if sample:
            vn_ref[t * TS:(t + 1) * TS, :] = vn
        vnb = vn.astype(BF16)
        for g in range(A_GROUPS):
            c0, c1 = g * A_GROUP_DIM, (g + 1) * A_GROUP_DIM
            for c in range(TS // CHUNK):
                r0, r1 = c * CHUNK, (c + 1) * CHUNK
                s = jnp.dot(w_mix[g], vnb[r0:r1, c0:c1], preferred_element_type=F32) + bsp_ref[g]
                oa_ref[t * TS + r0:t * TS + r1, c0:c1] = (u_a[r0:r1, c0:c1] * s).astype(
                    oa_ref.dtype)

    xns = [norm(t) for t in range(nsub)]
    pending = None
    for t in range(nsub):
        qkv(t, xns[t])
        a_cur = a_proj(xns[t])
        if pending is not None:
            gate(t - 1, *pending)
        pending = a_cur
    gate(nsub - 1, *pending)


def _inproj(x, gain, w_in, ln_g, ln_b, wsp, bsp, kv_all, *, layer, sample):
    rows = x.shape[0]
    tm = TM_IN
    row_spec = lambda w: pl.BlockSpec((tm, w), lambda i: (i, 0))
    head_row_spec = pl.BlockSpec((None, tm * H_DIFF, HEAD_W), lambda i: (layer, i, 0))
    col_spec = pl.BlockSpec((Q_COLS, tm), lambda i: (0, i))
    in_specs = [
        row_spec(D_MODEL),
        _layer_spec((1, D_MODEL), layer),
        _layer_spec((D_MODEL, IN_COLS), layer),
        _layer_spec((1, A_WIDTH), layer),
        _layer_spec((1, A_WIDTH), layer),
        _layer_spec((A_GROUPS, CHUNK, CHUNK), layer),
        _layer_spec((A_GROUPS, CHUNK, A_GROUP_DIM), layer),
    ]
    sds = jax.ShapeDtypeStruct
    kv_shape = sds((DEPTH, rows * H_DIFF, HEAD_W), F32)
    aliased = () if kv_all is None else tuple(kv_all)
    n_in = len(in_specs)
    in_specs += [pl.BlockSpec(memory_space=pl.ANY)] * len(aliased)
    aliases = {n_in + i: 1 + i for i in range(len(aliased))}
    if sample:
        out_shape = [sds((rows, Q_COLS), F32), kv_shape, kv_shape,
                     sds((rows, A_WIDTH), F32), sds((rows, A_WIDTH), F32)]
        out_specs = [row_spec(Q_COLS), head_row_spec, head_row_spec,
                     row_spec(A_WIDTH), row_spec(A_WIDTH)]
    else:
        out_shape = [sds((Q_COLS, rows), BF16), kv_shape, kv_shape,
                     sds((rows, Q_COLS), BF16), sds((Q_COLS, rows), BF16),
                     sds((rows, A_WIDTH), BF16)]
        out_specs = [col_spec, head_row_spec, head_row_spec,
                     row_spec(Q_COLS), col_spec, row_spec(A_WIDTH)]
    return pl.pallas_call(
        functools.partial(_inproj_kernel, tm=tm, sample=sample, n_aliased=len(aliased)),
        grid=(rows // tm,),
        in_specs=in_specs,
        out_specs=out_specs,
        out_shape=out_shape,
        input_output_aliases=aliases,
        compiler_params=pltpu.CompilerParams(
            dimension_semantics=("arbitrary",), vmem_limit_bytes=V7X_VMEM_LIMIT_BYTES),
        name="inproj_sample" if sample else "inproj_prompt",
    )(x, gain, w_in, ln_g, ln_b, wsp, bsp, *aliased)


def _prompt_attn_kernel(rb_ref, lamv_ref, idx_ref, sg_ref, qt_ref, k_ref, vt_ref, o_ref,
                        bias_scr, qw_scr, m_scr, acc_scr, *, lam_init):
    qi = pl.program_id(1)

    @pl.when(jnp.logical_and(pl.program_id(0) == 0, qi == 0))
    def _():
        for h in range(H_DIFF):
            for t in range(2):
                bias_scr[h, t] = _bias_from_buckets(idx_ref[t], rb_ref, h, LOG2_E)

    drow = lax.broadcasted_iota(jnp.int32, (HEAD_W, TQ), 0)
    first_map = drow < QK_HEAD_DIM
    for h in range(H_DIFF):
        qh = qt_ref[h * HEAD_W:(h + 1) * HEAD_W, :]
        zero = jnp.zeros_like(qh)
        qw_scr[h] = jnp.concatenate(
            [jnp.where(first_map, qh, zero), jnp.where(first_map, zero, qh)], axis=1)
    m_scr[...] = jnp.full_like(m_scr, NEG_INF)
    acc_scr[...] = jnp.zeros_like(acc_scr)
    ones_rows = jnp.ones((DENOM_ROWS, TK), BF16)

    def key_tiles(tiles):
        units = [(pl.multiple_of(j * TK, TK), kind, h) for j, kind in tiles for h in range(H_DIFF)]

        def scores(u):
            r0, _, h = u
            kj = k_ref[pl.ds(r0, TK), h * HEAD_W:(h + 1) * HEAD_W]
            return jnp.dot(kj, qw_scr[h], preferred_element_type=F32)

        def softmax(u, st):
            _, kind, h = u
            if kind is not None:
                bias = bias_scr[h, kind]
                st = jnp.concatenate([bias, bias], axis=1) + st
            m_old = m_scr[h]
            m_new = jnp.maximum(m_old, jnp.max(st, axis=0, keepdims=True))
            m_scr[h] = m_new
            return jnp.exp2(m_old - m_new), jnp.exp2(st - m_new).astype(BF16)

        def accumulate(u, alpha, pt):
            r0, _, h = u
            vtj = jnp.concatenate([vt_ref[h * HEAD_W:(h + 1) * HEAD_W, pl.ds(r0, TK)],
                                   ones_rows], axis=0)
            acc_scr[h] = alpha * acc_scr[h] + jnp.dot(vtj, pt, preferred_element_type=F32)

        n = len(units)
        st = {0: scores(units[0]), 1: scores(units[1])}
        p = {}
        for i in range(n):
            p[i] = softmax(units[i], st.pop(i))
            if i + 2 < n:
                st[i + 2] = scores(units[i + 2])
            if i >= 1:
                accumulate(units[i - 1], *p.pop(i - 1))
        accumulate(units[n - 1], *p.pop(n - 1))

    n_far = jnp.maximum(qi - 1, 0)

    def far_pair(jj, carry):
        key_tiles([(2 * jj, None), (2 * jj + 1, None)])
        return carry

    lax.fori_loop(0, n_far // 2, far_pair, 0)

    @pl.when(n_far % 2 == 1)
    def _():
        key_tiles([(n_far - 1, None)])

    @pl.when(qi >= 1)
    def _():
        key_tiles([(qi - 1, 1), (qi, 0)])

    @pl.when(qi == 0)
    def _():
        key_tiles([(qi, 0)])

    lam = _diff_lambda(lamv_ref, lam_init)
    for h in range(H_DIFF):
        acc = acc_scr[h, 0:V_HEAD_DIM, :]
        rl = 1.0 / acc_scr[h, V_HEAD_DIM:V_HEAD_DIM + 1, :]
        ot = acc[:, :TQ] * rl[:, :TQ] - lam * (acc[:, TQ:] * rl[:, TQ:])
        ot = ot * lax.rsqrt(jnp.mean(ot * ot, axis=0, keepdims=True) + SUBLN_EPS)
        ot = ot * sg_ref[...] * (1.0 - lam_init)
        o_ref[:, h * V_HEAD_DIM:(h + 1) * V_HEAD_DIM] = ot.T.astype(o_ref.dtype)


def _prompt_attn(qt, k, vt, rel_bias, lamv, sg, *, layer, batch, seq, lam_init):
    nq = seq // TQ
    kk = np.arange(TK)[:, None]
    qq = np.arange(TQ)[None, :]
    idx = jnp.asarray(np.stack([_bucket_table(qq - kk), _bucket_table(TQ + qq - kk)]))
    return pl.pallas_call(
        functools.partial(_prompt_attn_kernel, lam_init=lam_init),
        grid=(batch, nq),
        in_specs=[
            pl.BlockSpec(memory_space=pltpu.SMEM),
            _layer_spec((4, QK_HEAD_DIM), layer),
            _const_spec((2, TK, TQ)),
            _layer_spec((V_HEAD_DIM, TQ), layer),
            pl.BlockSpec((Q_COLS, TQ), lambda b, i: (0, b * nq + i)),
            pl.BlockSpec((seq, Q_COLS), lambda b, i: (b, 0)),
            pl.BlockSpec((Q_COLS, seq), lambda b, i: (0, b)),
        ],
        out_specs=pl.BlockSpec((TQ, DIFF_WIDTH), lambda b, i: (b * nq + i, 0)),
        out_shape=jax.ShapeDtypeStruct((batch * seq, DIFF_WIDTH), BF16),
        scratch_shapes=[pltpu.VMEM((H_DIFF, 2, TK, TQ), F32),
                        pltpu.VMEM((H_DIFF, HEAD_W, 2 * TQ), BF16),
                        pltpu.VMEM((H_DIFF, 1, 2 * TQ), F32),
                        pltpu.VMEM((H_DIFF, V_HEAD_DIM + DENOM_ROWS, 2 * TQ), F32)],
        compiler_params=pltpu.CompilerParams(
            dimension_semantics=("arbitrary", "arbitrary"),
            vmem_limit_bytes=V7X_VMEM_LIMIT_BYTES),
        name="prompt_attn",
    )(rel_bias, lamv, idx, sg, qt, k, vt)


def _sample_attn_kernel(pt_ref, rb_ref, lamv_ref, idx_ref, sg_ref, q_ref, kn_ref, vn_ref, *rest,
                        n_pages, dec_seq, lam_init):
    k_refs = rest[:n_pages]
    v_refs = rest[n_pages:2 * n_pages]
    o_ref, s_scr, kpad, vpad, bias_scr = rest[2 * n_pages:]
    del pt_ref
    rows = 2 * dec_seq
    past = n_pages * PAGE_SIZE

    @pl.when(pl.program_id(0) == 0)
    def _():
        for h in range(H_DIFF):
            for t in range(2):
                bias_scr[t, h * rows:(h + 1) * rows, :] = _bias_from_buckets(
                    idx_ref[t], rb_ref, h, 1.0)

    def head_rows(ref, h, n):
        return ref[pl.ds(h, n, stride=H_DIFF), :]

    kpad[...] = jnp.zeros_like(kpad)
    vpad[...] = jnp.zeros_like(vpad)
    for h in range(H_DIFF):
        kpad[h, 0:dec_seq, :] = head_rows(kn_ref, h, dec_seq)
        vpad[h, 0:dec_seq, :] = head_rows(vn_ref, h, dec_seq)

    lane = lax.broadcasted_iota(jnp.int32, (dec_seq, HEAD_W), 1)
    first_map = lane < QK_HEAD_DIM
    for h in range(H_DIFF):
        qh = q_ref[:, h * HEAD_W:(h + 1) * HEAD_W]
        qm = jnp.concatenate([jnp.where(first_map, qh, 0.0), jnp.where(first_map, 0.0, qh)],
                             axis=0).astype(BF16)
        r0, r1 = h * rows, (h + 1) * rows
        for p in range(0, n_pages, 2):
            k2 = jnp.concatenate([head_rows(k_refs[p], h, PAGE_SIZE),
                                  head_rows(k_refs[p + 1], h, PAGE_SIZE)], axis=0).astype(BF16)
            s_scr[r0:r1, p * PAGE_SIZE:(p + 2) * PAGE_SIZE] = _nt_dot(qm, k2)
        s_scr[r0:r1, past - PAGE_SIZE:past] += bias_scr[0, r0:r1, :]
        s_scr[r0:r1, past:] = _nt_dot(qm, kpad[h].astype(BF16)) + bias_scr[1, r0:r1, :]

    s = s_scr[...]
    e = jnp.exp(s - jnp.max(s, axis=-1, keepdims=True))
    pn = e / jnp.sum(e, axis=-1, keepdims=True)
    lam = _diff_lambda(lamv_ref, lam_init)
    for h in range(H_DIFF):
        r0 = h * rows
        a = (pn[r0:r0 + dec_seq] - lam * pn[r0 + dec_seq:r0 + rows]).astype(BF16)
        acc = jnp.dot(a[:, past:], vpad[h].astype(BF16), preferred_element_type=F32)
        for p in range(n_pages):
            acc = acc + jnp.dot(a[:, p * PAGE_SIZE:(p + 1) * PAGE_SIZE],
                                head_rows(v_refs[p], h, PAGE_SIZE).astype(BF16),
                                preferred_element_type=F32)
        o_ref[:, h * V_HEAD_DIM:(h + 1) * V_HEAD_DIM] = (
            _rms(acc, sg_ref[...], SUBLN_EPS) * (1.0 - lam_init))


def _sample_attn(q, k_new, v_new, cache_k, cache_v, page_table, rel_bias, lamv, subln_gain,
                 *, layer, dec_seq, lam_init):
    dec_batch, n_pages = page_table.shape
    assert n_pages % 2 == 0
    n_pool = cache_k.shape[1]
    page_rows = PAGE_SIZE * H_DIFF
    ck = cache_k.reshape(DEPTH, n_pool, page_rows, HEAD_W)
    cv = cache_v.reshape(DEPTH, n_pool, page_rows, V_HEAD_DIM)
    past_len = n_pages * PAGE_SIZE
    rows = 2 * dec_seq
    i = (np.arange(rows) % dec_seq)[:, None]
    c = np.arange(PAGE_SIZE)[None, :]
    idx_last = _bucket_table(past_len + i - (past_len - PAGE_SIZE + c))
    idx_new = np.where(c < dec_seq, _bucket_table(i - c), -1).astype(np.int32)
    idx = jnp.asarray(np.stack([idx_last, idx_new]))
    nkeys = past_len + PAGE_SIZE

    def page_spec(p):
        return pl.BlockSpec((None, None, page_rows, HEAD_W),
                            lambda b, pt: (layer, pt[b, p], 0, 0))

    const = lambda shape: pl.BlockSpec(shape, lambda b, pt: (0,) * len(shape))
    per_layer = lambda shape: pl.BlockSpec((None,) + shape,
                                           lambda b, pt: (layer,) + (0,) * len(shape))
    row_spec = pl.BlockSpec((dec_seq, Q_COLS), lambda b, pt: (b, 0))
    new_spec = pl.BlockSpec((None, dec_seq * H_DIFF, HEAD_W), lambda b, pt: (layer, b, 0))
    grid_spec = pltpu.PrefetchScalarGridSpec(
        num_scalar_prefetch=1,
        grid=(dec_batch,),
        in_specs=[
            pl.BlockSpec(memory_space=pltpu.SMEM),
            per_layer((4, QK_HEAD_DIM)),
            const((2, rows, PAGE_SIZE)),
            per_layer((1, V_HEAD_DIM)),
            row_spec, new_spec, new_spec,
        ] + [page_spec(p) for p in range(n_pages)] * 2,
        out_specs=row_spec,
        scratch_shapes=[
            pltpu.VMEM((H_DIFF * rows, nkeys), F32),
            pltpu.VMEM((H_DIFF, PAGE_SIZE, HEAD_W), F32),
            pltpu.VMEM((H_DIFF, PAGE_SIZE, V_HEAD_DIM), F32),
            pltpu.VMEM((2, H_DIFF * rows, PAGE_SIZE), F32),
        ],
    )
    return pl.pallas_call(
        functools.partial(_sample_attn_kernel, n_pages=n_pages, dec_seq=dec_seq, lam_init=lam_init),
        grid_spec=grid_spec,
        out_shape=jax.ShapeDtypeStruct((dec_batch * dec_seq, DIFF_WIDTH), F32),
        compiler_params=pltpu.CompilerParams(
            dimension_semantics=("arbitrary",), vmem_limit_bytes=V7X_VMEM_LIMIT_BYTES),
        name="sample_attn",
    )(page_table, rel_bias, lamv, idx, subln_gain, q, k_new, v_new,
      *([ck] * n_pages), *([cv] * n_pages))


def _mix_and_prenorm(x_ref, ob_ref, oa_ref, wo_ref, g_post_ref, g_pre_ref):
    mix = jnp.dot(ob_ref[...].astype(BF16), wo_ref[0:DIFF_WIDTH, :], preferred_element_type=F32)
    mix = mix + jnp.dot(oa_ref[...].astype(BF16), wo_ref[DIFF_WIDTH:, :],
                        preferred_element_type=F32)
    x1 = x_ref[...] + _rms(mix, g_post_ref[...], NORM_EPS)
    return x1, _rms(x1, g_pre_ref[...], NORM_EPS)


def _conv_gate(cw, cb, h, h1, h2):
    return cb + cw[0:1] * h2 + cw[1:2] * h1 + cw[2:3] * h


def _ffn_prompt_kernel(x_ref, ob_ref, oa_ref, wo_ref, g_post_ref, g_pre_ref, g_out_ref,
                       wu_ref, cw_ref, cb_ref, wd_ref, y_ref, st_ref, perm, carry, act,
                       *, tm):
    nsub = tm // TS
    nv = TS // 8
    nj = D_FF // TF
    ncb = D_MODEL // LANE

    @pl.when(pl.program_id(1) == 0)
    def _():
        carry[...] = jnp.zeros_like(carry)

    def pre(t):
        rows = slice(t * TS, (t + 1) * TS)
        x1, xn2 = _mix_and_prenorm(x_ref.at[rows], ob_ref.at[rows], oa_ref.at[rows], wo_ref,
                                   g_post_ref, g_pre_ref)
        y_ref[rows, :] = x1
        for c in range(ncb):
            for s in range(8):
                perm[t, c, s * PITCH:s * PITCH + nv, :] = xn2[s * nv:(s + 1) * nv,
                                                              c * LANE:(c + 1) * LANE]
        return jnp.concatenate(
            [jnp.concatenate([perm[t, c, pl.ds(v, 8, stride=PITCH), :] for c in range(ncb)],
                             axis=1)
             for v in range(nv)], axis=0).astype(BF16)

    first_row = lax.broadcasted_iota(jnp.int32, (8, TF), 0) == 0

    def conv_ffn(t, xp):
        def up(j):
            return [jnp.dot(xp, wu_ref[:, half * D_FF + j * TF:half * D_FF + (j + 1) * TF],
                            preferred_element_type=F32) for half in range(2)]

        h_next = up(0)
        for j in range(nj):
            h_cur = h_next
            if j + 1 < nj:
                h_next = up(j + 1)
            conv = []
            for half in range(2):
                c0 = half * D_FF + j * TF
                c1 = c0 + TF
                h = h_cur[half]
                prev = carry[:, c0:c1]
                head = [jnp.where(first_row, pltpu.roll(prev[8 * i:8 * i + 8], 1, 0),
                                  pltpu.roll(h[TS - 16 + 8 * i:TS - 8 + 8 * i], 1, 0))
                        for i in range(2)]
                h1 = jnp.concatenate([head[1], h[:TS - 8]], axis=0)
                h2 = jnp.concatenate([head[0], head[1], h[:TS - 16]], axis=0)
                carry[:, c0:c1] = h[TS - 16:]
                conv.append(_conv_gate(cw_ref[:, c0:c1], cb_ref[:, c0:c1], h, h1, h2))
            act[t, :, j * TF:(j + 1) * TF] = (jax.nn.silu(conv[0]) * conv[1]).astype(BF16)

    def post(t):
        rows = slice(t * TS, (t + 1) * TS)
        fp = jnp.dot(act[t], wd_ref[...], preferred_element_type=F32)
        for v in range(nv):
            for c in range(ncb):
                perm[t, c, pl.ds(v, 8, stride=PITCH), :] = fp[8 * v:8 * v + 8,
                                                              c * LANE:(c + 1) * LANE]
        f = jnp.concatenate(
            [jnp.concatenate([perm[t, c, s * PITCH:s * PITCH + nv, :] for s in range(8)], axis=0)
             for c in range(ncb)], axis=1)
        y_ref[rows, :] = y_ref[rows, :] + _rms(f, g_out_ref[...], NORM_EPS)

    xps = [pre(t) for t in range(nsub)]
    for t in range(nsub):
        conv_ffn(t, xps[t])
        post(t)
    st_ref[0, 0:1, :] = carry[7:8, :]
    st_ref[0, 1:2, :] = carry[15:16, :]


def _ffn_prompt(x, o_b, o_a, w_out, g_post, g_pre, g_out, w_up, conv_w, conv_b, w_down,
                *, layer, batch, seq):
    tm = TM_FFN
    nt = seq // tm
    row_spec = lambda w: pl.BlockSpec((tm, w), lambda b, i: (b * nt + i, 0))
    per_layer = functools.partial(_layer_spec, layer=layer)
    return pl.pallas_call(
        functools.partial(_ffn_prompt_kernel, tm=tm),
        grid=(batch, nt),
        in_specs=[
            row_spec(D_MODEL), row_spec(DIFF_WIDTH), row_spec(A_WIDTH),
            per_layer((D_MODEL, D_MODEL)),
            per_layer((1, D_MODEL)), per_layer((1, D_MODEL)), per_layer((1, D_MODEL)),
            per_layer((D_MODEL, 2 * D_FF)),
            per_layer((CONV_W, 2 * D_FF)),
            per_layer((1, 2 * D_FF)),
            per_layer((D_FF, D_MODEL)),
        ],
        out_specs=[row_spec(D_MODEL),
                   pl.BlockSpec((1, CONV_W - 1, 2 * D_FF), lambda b, i: (b, 0, 0))],
        out_shape=[jax.ShapeDtypeStruct((batch * seq, D_MODEL), F32),
                   jax.ShapeDtypeStruct((batch, CONV_W - 1, 2 * D_FF), F32)],
        scratch_shapes=[pltpu.VMEM((tm // TS, D_MODEL // LANE, 8 * PITCH, LANE), F32),
                        pltpu.VMEM((16, 2 * D_FF), F32),
                        pltpu.VMEM((tm // TS, TS, D_FF), BF16)],
        compiler_params=pltpu.CompilerParams(
            dimension_semantics=("arbitrary", "arbitrary"),
            vmem_limit_bytes=V7X_VMEM_LIMIT_BYTES),
        name="ffn_prompt",
    )(x, o_b, o_a, w_out, g_post, g_pre, g_out, w_up, conv_w, conv_b, w_down)


def _ffn_sample_kernel(x_ref, ob_ref, oa_ref, wo_ref, g_post_ref, g_pre_ref, g_out_ref,
                       wug_ref, wuu_ref, cwg_ref, cwu_ref, cbg_ref, cbu_ref, wd_ref,
                       stg_ref, stu_ref, y_ref, sog_ref, sou_ref, x1_scr, xn_scr, acc_scr,
                       *, dec_batch, dec_seq):
    j = pl.program_id(0)
    rows = dec_batch * dec_seq

    @pl.when(j == 0)
    def _():
        x1, xn2 = _mix_and_prenorm(x_ref, ob_ref, oa_ref, wo_ref, g_post_ref, g_pre_ref)
        x1_scr[...] = x1
        xn_scr[...] = xn2.astype(BF16)
        acc_scr[...] = jnp.zeros_like(acc_scr)

    xn2 = xn_scr[...]
    conv = []
    for wu_ref, cw_ref, cb_ref, st_ref, so_ref in ((wug_ref, cwg_ref, cbg_ref, stg_ref, sog_ref),
                                                   (wuu_ref, cwu_ref, cbu_ref, stu_ref, sou_ref)):
        h = jnp.dot(xn2, wu_ref[...], preferred_element_type=F32)
        s0 = st_ref[0]
        s1 = st_ref[1]
        h1 = jnp.concatenate([s1, h[:rows - dec_batch]], axis=0)
        h2 = jnp.concatenate([s0, s1, h[:rows - 2 * dec_batch]], axis=0)
        conv.append(_conv_gate(cw_ref[...], cb_ref[...], h, h1, h2))
        so_ref[0] = h[rows - 2 * dec_batch:rows - dec_batch]
        so_ref[1] = h[rows - dec_batch:]
    act = (jax.nn.silu(conv[0]) * conv[1]).astype(BF16)
    acc_scr[...] += jnp.dot(act, wd_ref[...], preferred_element_type=F32)

    @pl.when(j == pl.num_programs(0) - 1)
    def _():
        y_ref[...] = x1_scr[...] + _rms(acc_scr[...], g_out_ref[...], NORM_EPS)


def _ffn_sample(x, o_b, o_a, w_out, g_post, g_pre, g_out, w_up, conv_w, conv_b, w_down, state,
                *, layer, dec_batch, dec_seq):
    rows = dec_batch * dec_seq
    nj = D_FF // TF
    ns = CONV_W - 1

    def col_tile(shape, ax, first):
        return pl.BlockSpec((None,) + shape,
                            lambda j: (layer,) + tuple(first + j if a == ax else 0
                                                       for a in range(len(shape))))

    gate = functools.partial(col_tile, first=0)
    up = functools.partial(col_tile, first=nj)
    per_layer = functools.partial(_layer_spec, layer=layer)
    state_out = pl.BlockSpec((ns, dec_batch, TF), lambda j: (0, 0, j))
    sds = jax.ShapeDtypeStruct
    y, st_gate, st_up = pl.pallas_call(
        functools.partial(_ffn_sample_kernel, dec_batch=dec_batch, dec_seq=dec_seq),
        grid=(nj,),
        in_specs=[
            _const_spec((rows, D_MODEL)), _const_spec((rows, DIFF_WIDTH)),
            _const_spec((rows, A_WIDTH)),
            per_layer((D_MODEL, D_MODEL)),
            per_layer((1, D_MODEL)), per_layer((1, D_MODEL)), per_layer((1, D_MODEL)),
            gate((D_MODEL, TF), 1), up((D_MODEL, TF), 1),
            gate((CONV_W, TF), 1), up((CONV_W, TF), 1),
            gate((1, TF), 1), up((1, TF), 1),
            gate((TF, D_MODEL), 0),
            gate((ns, dec_batch, TF), 2), up((ns, dec_batch, TF), 2),
        ],
        out_specs=[pl.BlockSpec((rows, D_MODEL), lambda j: (0, 0)), state_out, state_out],
        out_shape=[sds((rows, D_MODEL), F32), sds((ns, dec_batch, D_FF), F32),
                   sds((ns, dec_batch, D_FF), F32)],
        scratch_shapes=[pltpu.VMEM((rows, D_MODEL), F32), pltpu.VMEM((rows, D_MODEL), BF16),
                        pltpu.VMEM((rows, D_MODEL), F32)],
        compiler_params=pltpu.CompilerParams(
            dimension_semantics=("arbitrary",), vmem_limit_bytes=V7X_VMEM_LIMIT_BYTES),
        name="ffn_sample",
    )(x, o_b, o_a, w_out, g_post, g_pre, g_out, w_up, w_up, conv_w, conv_w, conv_b, conv_b,
      w_down, state, state)
    return y, jnp.concatenate([st_gate, st_up], axis=-1)


def kernel(x_prompt, x_sample, cache_k, cache_v, state_conv, page_table, rel_bias, norm_mix_pre, norm_mix_post, norm_ffn_pre, norm_ffn_post, w_in, lambda_q1, lambda_k1, lambda_q2, lambda_k2, subln_gain, sgu_ln_gain, sgu_ln_bias, w_spatial, b_spatial, w_out, w_up, conv_w, conv_b, w_down):
    batch, seq, _ = x_prompt.shape
    dec_batch, dec_seq, _ = x_sample.shape
    assert seq % TQ == 0 and seq % TM_FFN == 0 and (batch * seq) % TM_IN == 0
    assert (dec_batch * dec_seq) % TM_IN == 0 and CHUNK % dec_seq == 0 and TQ == TK

    def to_pm(a):
        return a.reshape(dec_batch, dec_seq, -1).transpose(1, 0, 2).reshape(dec_batch * dec_seq, -1)

    def to_bm(a):
        return a.reshape(dec_seq, dec_batch, -1).transpose(1, 0, 2).reshape(dec_batch * dec_seq, -1)

    y_p = x_prompt.reshape(batch * seq, D_MODEL)
    y_s = x_sample.reshape(dec_batch * dec_seq, D_MODEL)
    reps = CHUNK // dec_seq

    rows3 = lambda a: a.reshape(DEPTH, 1, -1)
    w_in_b, w_out_b = w_in.astype(BF16), w_out.astype(BF16)
    w_up_b, w_down_b = w_up.astype(BF16), w_down.astype(BF16)
    lamv = jnp.stack([lambda_q1, lambda_k1, lambda_q2, lambda_k2], axis=1)
    sg_rows = jnp.broadcast_to(subln_gain[:, :, None], (DEPTH, V_HEAD_DIM, TQ))
    ffn_args = (w_out_b, rows3(norm_mix_post), rows3(norm_ffn_pre), rows3(norm_ffn_post),
                w_up_b, conv_w, rows3(conv_b), w_down_b)
    in_args = (rows3(norm_mix_pre), w_in_b, rows3(sgu_ln_gain), rows3(sgu_ln_bias))
    gate_shape = (DEPTH, A_GROUPS, CHUNK, A_GROUP_DIM)
    bsp_p = jnp.broadcast_to(b_spatial[:, :, :, None], gate_shape)
    wsp_s = jnp.tile(w_spatial[:, :, :dec_seq, :dec_seq], (1, 1, reps, reps))
    bsp_s = jnp.broadcast_to(jnp.tile(b_spatial[:, :, :dec_seq], (1, 1, reps))[:, :, :, None],
                             gate_shape)
    state = state_conv.transpose(0, 2, 1, 3)

    cp, csm, svs = [], [], []
    kv_p = kv_s = None
    for l in range(DEPTH):
        lam_init = _lambda_init(l)

        qt, kf, vf, kb, vt, o_a = _inproj(y_p, *in_args, w_spatial, bsp_p, kv_p,
                                          layer=l, sample=False)
        kv_p = (kf, vf)
        o_b = _prompt_attn(qt, kb, vt, rel_bias, lamv, sg_rows,
                           layer=l, batch=batch, seq=seq, lam_init=lam_init)
        y_p, conv_p = _ffn_prompt(y_p, o_b, o_a, *ffn_args, layer=l, batch=batch, seq=seq)
        cp.append(conv_p)

        q, kf, vf, o_a, vn = _inproj(y_s, *in_args, wsp_s, bsp_s, kv_s, layer=l, sample=True)
        kv_s = (kf, vf)
        o_b = _sample_attn(q, kf, vf, cache_k, cache_v, page_table, rel_bias, lamv,
                           rows3(subln_gain), layer=l, dec_seq=dec_seq, lam_init=lam_init)
        y_pm, conv_s = _ffn_sample(to_pm(y_s), to_pm(o_b), to_pm(o_a), *ffn_args, state,
                                   layer=l, dec_batch=dec_batch, dec_seq=dec_seq)
        y_s = to_bm(y_pm)
        csm.append(conv_s.transpose(1, 0, 2))
        svs.append(vn.reshape(dec_batch, dec_seq, A_WIDTH))

    kv_shape_p = (DEPTH, batch, seq, H_DIFF, HEAD_W)
    kv_shape_s = (DEPTH, dec_batch, dec_seq, H_DIFF, HEAD_W)
    return (y_p.reshape(batch, seq, D_MODEL), y_s.reshape(dec_batch, dec_seq, D_MODEL),
            kv_p[0].reshape(kv_shape_p), kv_p[1].reshape(kv_shape_p), jnp.stack(cp),
            kv_s[0].reshape(kv_shape_s), kv_s[1].reshape(kv_shape_s),
            jnp.stack(csm), jnp.stack(svs))
```

```python
import functools
import math

import jax
import jax.numpy as jnp
import numpy as np
from jax import lax
from jax.experimental import pallas as pl
from jax.experimental.pallas import tpu as pltpu

F32 = jnp.float32
BF16 = jnp.bfloat16

D_MODEL = 1024
DEPTH = 2
PAGE_SIZE = 128
H_DIFF = 4
QK_HEAD_DIM = 64
V_HEAD_DIM = 128
HEAD_W = 2 * QK_HEAD_DIM
DIFF_WIDTH = H_DIFF * V_HEAD_DIM
A_GROUPS = 4
A_WIDTH = 512
A_GROUP_DIM = 128
CHUNK = 128
LANE = 128
Q_COLS = H_DIFF * HEAD_W
IN_COLS = 3 * Q_COLS + 2 * A_WIDTH
D_FF = 2816
CONV_W = 3
REL_BUCKETS = 32
REL_MAX_EXACT = 16
REL_MAX_DIST = 128
NORM_EPS = 1e-6
SUBLN_EPS = 1e-5
NEG_INF = -1e30
LOG2_E = math.log2(math.e)

V7X_VMEM_LIMIT_BYTES = 56 * 1024 * 1024

TM_IN = 1024
TQ = 256
TK = 256
DENOM_ROWS = 16
TM_FFN = 512
TS = 256
PITCH = TS // 8 + 8
TF = 256


def _lambda_init(layer):
    return 0.8 - 0.6 * math.exp(-0.3 * layer)


def _bucket_table(dist):
    n = np.maximum(dist, 0)
    nf = np.maximum(n, 1).astype(np.float32)
    large = REL_MAX_EXACT + (np.log(nf / REL_MAX_EXACT) / math.log(REL_MAX_DIST / REL_MAX_EXACT)
                             * (REL_BUCKETS - REL_MAX_EXACT)).astype(np.int32)
    large = np.minimum(large, REL_BUCKETS - 1)
    return np.where(dist >= 0, np.where(n < REL_MAX_EXACT, n, large), -1).astype(np.int32)


def _const_spec(shape):
    nd = len(shape)
    return pl.BlockSpec(shape, lambda *_: (0,) * nd, pipeline_mode=pl.Buffered(1))


def _layer_spec(shape, layer):
    nd = len(shape)
    return pl.BlockSpec((None,) + tuple(shape), lambda *_: (layer,) + (0,) * nd,
                        pipeline_mode=pl.Buffered(1))


def _rms(x, gain, eps):
    y = x * lax.rsqrt(jnp.mean(x * x, axis=-1, keepdims=True) + eps)
    return y * gain


def _nt_dot(a, b):
    return lax.dot_general(a, b, (((1,), (1,)), ((), ())), preferred_element_type=F32)


def _diff_lambda(lamv_ref, lam_init):
    p = lamv_ref[...]
    a = jnp.sum(p[0:1] * p[1:2], axis=-1, keepdims=True)
    b = jnp.sum(p[2:3] * p[3:4], axis=-1, keepdims=True)
    return jnp.exp(a) - jnp.exp(b) + lam_init


def _bias_from_buckets(idx, rb_ref, head, scale):
    far = rb_ref[REL_BUCKETS - 1, head]
    tbl = jnp.full(idx.shape, NEG_INF, F32)
    for b in range(REL_BUCKETS):
        tbl = jnp.where(idx == b, (rb_ref[b, head] - far) * scale, tbl)
    return tbl


def _inproj_kernel(x_ref, g_ref, w_ref, lng_ref, lnb_ref, wsp_ref, bsp_ref, *out_refs,
                   tm, sample, n_aliased):
    out_refs = out_refs[n_aliased:]
    if sample:
        q_ref, kf_ref, vf_ref, oa_ref, vn_ref = out_refs
    else:
        qt_ref, kf_ref, vf_ref, kb_ref, vt_ref, oa_ref = out_refs

    nsub = tm // TS
    a0 = 3 * Q_COLS

    def norm(t):
        return _rms(x_ref[t * TS:(t + 1) * TS, :], g_ref[...], NORM_EPS).astype(BF16)

    def proj(xn, c0, c1):
        return jnp.dot(xn, w_ref[:, c0:c1], preferred_element_type=F32)

    def qkv(t, xn):
        rows = slice(t * TS, (t + 1) * TS)
        q = proj(xn, 0, Q_COLS)
        k = proj(xn, Q_COLS, 2 * Q_COLS)
        v = proj(xn, 2 * Q_COLS, 3 * Q_COLS)
        for h in range(H_DIFF):
            head_rows = pl.ds(t * TS * H_DIFF + h, TS, stride=H_DIFF)
            for d in range(kf_ref.shape[0]):
                kf_ref[d, head_rows, :] = k[:, h * HEAD_W:(h + 1) * HEAD_W]
                vf_ref[d, head_rows, :] = v[:, h * V_HEAD_DIM:(h + 1) * V_HEAD_DIM]
        if sample:
            q_ref[rows, :] = q * (QK_HEAD_DIM ** -0.5)
        else:
            qt_ref[:, rows] = (q * (QK_HEAD_DIM ** -0.5 * LOG2_E)).T.astype(BF16)
            vt_ref[:, rows] = v.T.astype(BF16)
            kb_ref[rows, :] = k.astype(BF16)

    def a_proj(xn):
        return proj(xn, a0, a0 + A_WIDTH), proj(xn, a0 + A_WIDTH, a0 + 2 * A_WIDTH)

    row = lax.broadcasted_iota(jnp.int32, (CHUNK, CHUNK), 0)
    col = lax.broadcasted_iota(jnp.int32, (CHUNK, CHUNK), 1)
    if sample:
        keep = jnp.logical_and((row >> 3) == (col >> 3), (col & 7) <= (row & 7))
    else:
        keep = col <= row
    w_mix = [jnp.where(keep, wsp_ref[g], 0.0).astype(BF16) for g in range(A_GROUPS)]

    def gate(t, u_raw, v_raw):
        u_a = jax.nn.gelu(u_raw)
        v_a = jax.nn.gelu(v_raw)
        xc = v_a - jnp.mean(v_a, axis=-1, keepdims=True)
        vn = xc * lax.rsqrt(jnp.mean(xc * xc, axis=-1, keepdims=True) + NORM_EPS)
        vn = vn * lng_ref[...] + lnb_ref[...]
        if sample:
            vn_ref[t * TS:(t + 1) * TS, :] = vn
        vnb = vn.astype(BF16)
        for g in range(A_GROUPS):
            c0, c1 = g * A_GROUP_DIM, (g + 1) * A_GROUP_DIM
            for c in range(TS // CHUNK):
                r0, r1 = c * CHUNK, (c + 1) * CHUNK
                s = jnp.dot(w_mix[g], vnb[r0:r1, c0:c1], preferred_element_type=F32) + bsp_ref[g]
                oa_ref[t * TS + r0:t * TS + r1, c0:c1] = (u_a[r0:r1, c0:c1] * s).astype(
                    oa_ref.dtype)

    xns = [norm(t) for t in range(nsub)]
    pending = None
    for t in range(nsub):
        qkv(t, xns[t])
        a_cur = a_proj(xns[t])
        if pending is not None:
            gate(t - 1, *pending)
        pending = a_cur
    gate(nsub - 1, *pending)


def _inproj(x, gain, w_in, ln_g, ln_b, wsp, bsp, kv_all, *, layer, sample):
    rows = x.shape[0]
    tm = TM_IN
    row_spec = lambda w: pl.BlockSpec((tm, w), lambda i: (i, 0))
    assert kv_all is not None or layer == 0
    n_slabs = DEPTH if kv_all is None else 1
    head_row_spec = pl.BlockSpec((n_slabs, tm * H_DIFF, HEAD_W),
                                 lambda i: (layer // n_slabs, i, 0))
    col_spec = pl.BlockSpec((Q_COLS, tm), lambda i: (0, i))
    in_specs = [
        row_spec(D_MODEL),
        _layer_spec((1, D_MODEL), layer),
        _layer_spec((D_MODEL, IN_COLS), layer),
        _layer_spec((1, A_WIDTH), layer),
        _layer_spec((1, A_WIDTH), layer),
        _layer_spec((A_GROUPS, CHUNK, CHUNK), layer),
        _layer_spec((A_GROUPS, CHUNK, A_GROUP_DIM), layer),
    ]
    sds = jax.ShapeDtypeStruct
    kv_shape = sds((DEPTH, rows * H_DIFF, HEAD_W), F32)
    aliased = () if kv_all is None else tuple(kv_all)
    n_in = len(in_specs)
    in_specs += [pl.BlockSpec(memory_space=pl.ANY)] * len(aliased)
    aliases = {n_in + i: 1 + i for i in range(len(aliased))}
    if sample:
        out_shape = [sds((rows, Q_COLS), F32), kv_shape, kv_shape,
                     sds((rows, A_WIDTH), F32), sds((rows, A_WIDTH), F32)]
        out_specs = [row_spec(Q_COLS), head_row_spec, head_row_spec,
                     row_spec(A_WIDTH), row_spec(A_WIDTH)]
    else:
        out_shape = [sds((Q_COLS, rows), BF16), kv_shape, kv_shape,
                     sds((rows, Q_COLS), BF16), sds((Q_COLS, rows), BF16),
                     sds((rows, A_WIDTH), BF16)]
        out_specs = [col_spec, head_row_spec, head_row_spec,
                     row_spec(Q_COLS), col_spec, row_spec(A_WIDTH)]
    return pl.pallas_call(
        functools.partial(_inproj_kernel, tm=tm, sample=sample, n_aliased=len(aliased)),
        grid=(rows // tm,),
        in_specs=in_specs,
        out_specs=out_specs,
        out_shape=out_shape,
        input_output_aliases=aliases,
        compiler_params=pltpu.CompilerParams(
            dimension_semantics=("arbitrary",), vmem_limit_bytes=V7X_VMEM_LIMIT_BYTES),
        name="inproj_sample" if sample else "inproj_prompt",
    )(x, gain, w_in, ln_g, ln_b, wsp, bsp, *aliased)


def _prompt_attn_kernel(rb_ref, lamv_ref, idx_ref, sg_ref, qt_ref, k_ref, vt_ref, o_ref,
                        bias_scr, qw_scr, m_scr, acc_scr, *, lam_init):
    qi = pl.program_id(1)

    @pl.when(jnp.logical_and(pl.program_id(0) == 0, qi == 0))
    def _():
        for h in range(H_DIFF):
            for t in range(2):
                bias_scr[h, t] = _bias_from_buckets(idx_ref[t], rb_ref, h, LOG2_E)

    drow = lax.broadcasted_iota(jnp.int32, (HEAD_W, TQ), 0)
    first_map = drow < QK_HEAD_DIM
    for h in range(H_DIFF):
        qh = qt_ref[h * HEAD_W:(h + 1) * HEAD_W, :]
        zero = jnp.zeros_like(qh)
        qw_scr[h] = jnp.concatenate(
            [jnp.where(first_map, qh, zero), jnp.where(first_map, zero, qh)], axis=1)
    m_scr[...] = jnp.full_like(m_scr, NEG_INF)
    acc_scr[...] = jnp.zeros_like(acc_scr)
    ones_rows = jnp.ones((DENOM_ROWS, TK), BF16)

    def key_tiles(tiles):
        units = [(pl.multiple_of(j * TK, TK), kind, h) for j, kind in tiles for h in range(H_DIFF)]

        def scores(u):
            r0, _, h = u
            kj = k_ref[pl.ds(r0, TK), h * HEAD_W:(h + 1) * HEAD_W]
            return jnp.dot(kj, qw_scr[h], preferred_element_type=F32)

        def softmax(u, st):
            _, kind, h = u
            if kind is not None:
                bias = bias_scr[h, kind]
                st = jnp.concatenate([bias, bias], axis=1) + st
            m_old = m_scr[h]
            m_new = jnp.maximum(m_old, jnp.max(st, axis=0, keepdims=True))
            m_scr[h] = m_new
            return jnp.exp2(m_old - m_new), jnp.exp2(st - m_new).astype(BF16)

        def accumulate(u, alpha, pt):
            r0, _, h = u
            vtj = jnp.concatenate([vt_ref[h * HEAD_W:(h + 1) * HEAD_W, pl.ds(r0, TK)],
                                   ones_rows], axis=0)
            acc_scr[h] = alpha * acc_scr[h] + jnp.dot(vtj, pt, preferred_element_type=F32)

        n = len(units)
        st = {0: scores(units[0]), 1: scores(units[1])}
        p = {}
        for i in range(n):
            p[i] = softmax(units[i], st.pop(i))
            if i + 2 < n:
                st[i + 2] = scores(units[i + 2])
            if i >= 1:
                accumulate(units[i - 1], *p.pop(i - 1))
        accumulate(units[n - 1], *p.pop(n - 1))

    n_far = jnp.maximum(qi - 1, 0)

    def far_pair(jj, carry):
        key_tiles([(2 * jj, None), (2 * jj + 1, None)])
        return carry

    lax.fori_loop(0, n_far // 2, far_pair, 0)

    @pl.when(n_far % 2 == 1)
    def _():
        key_tiles([(n_far - 1, None)])

    @pl.when(qi >= 1)
    def _():
        key_tiles([(qi - 1, 1), (qi, 0)])

    @pl.when(qi == 0)
    def _():
        key_tiles([(qi, 0)])

    lam = _diff_lambda(lamv_ref, lam_init)
    for h in range(H_DIFF):
        acc = acc_scr[h, 0:V_HEAD_DIM, :]
        rl = 1.0 / acc_scr[h, V_HEAD_DIM:V_HEAD_DIM + 1, :]
        ot = acc[:, :TQ] * rl[:, :TQ] - lam * (acc[:, TQ:] * rl[:, TQ:])
        ot = ot * lax.rsqrt(jnp.mean(ot * ot, axis=0, keepdims=True) + SUBLN_EPS)
        ot = ot * sg_ref[...] * (1.0 - lam_init)
        o_ref[:, h * V_HEAD_DIM:(h + 1) * V_HEAD_DIM] = ot.T.astype(o_ref.dtype)


def _prompt_attn(qt, k, vt, rel_bias, lamv, sg, *, layer, batch, seq, lam_init):
    nq = seq // TQ
    kk = np.arange(TK)[:, None]
    qq = np.arange(TQ)[None, :]
    idx = jnp.asarray(np.stack([_bucket_table(qq - kk), _bucket_table(TQ + qq - kk)]))
    return pl.pallas_call(
        functools.partial(_prompt_attn_kernel, lam_init=lam_init),
        grid=(batch, nq),
        in_specs=[
            pl.BlockSpec(memory_space=pltpu.SMEM),
            _layer_spec((4, QK_HEAD_DIM), layer),
            _const_spec((2, TK, TQ)),
            _layer_spec((V_HEAD_DIM, TQ), layer),
            pl.BlockSpec((Q_COLS, TQ), lambda b, i: (0, b * nq + i)),
            pl.BlockSpec((seq, Q_COLS), lambda b, i: (b, 0)),
            pl.BlockSpec((Q_COLS, seq), lambda b, i: (0, b)),
        ],
        out_specs=pl.BlockSpec((TQ, DIFF_WIDTH), lambda b, i: (b * nq + i, 0)),
        out_shape=jax.ShapeDtypeStruct((batch * seq, DIFF_WIDTH), BF16),
        scratch_shapes=[pltpu.VMEM((H_DIFF, 2, TK, TQ), F32),
                        pltpu.VMEM((H_DIFF, HEAD_W, 2 * TQ), BF16),
                        pltpu.VMEM((H_DIFF, 1, 2 * TQ), F32),
                        pltpu.VMEM((H_DIFF, V_HEAD_DIM + DENOM_ROWS, 2 * TQ), F32)],
        compiler_params=pltpu.CompilerParams(
            dimension_semantics=("arbitrary", "arbitrary"),
            vmem_limit_bytes=V7X_VMEM_LIMIT_BYTES),
        name="prompt_attn",
    )(rel_bias, lamv, idx, sg, qt, k, vt)


def _sample_attn_kernel(pt_ref, rb_ref, lamv_ref, idx_ref, sg_ref, q_ref, kn_ref, vn_ref,
                        ck_hbm, cv_hbm, o_ref, kbuf, vbuf, sem, s_scr, kpad, vpad, bias_scr,
                        *, layer, n_pages, dec_seq, lam_init):
    b = pl.program_id(0)
    nb = pl.num_programs(0)
    slot = lax.rem(b, 2)
    rows = 2 * dec_seq
    past = n_pages * PAGE_SIZE

    def page_copies(seq, sl):
        copies = []
        for p in range(n_pages):
            page = pt_ref[seq, p]
            copies.append(pltpu.make_async_copy(ck_hbm.at[layer, page], kbuf.at[sl, p],
                                                sem.at[sl, 0]))
            copies.append(pltpu.make_async_copy(cv_hbm.at[layer, page], vbuf.at[sl, p],
                                                sem.at[sl, 1]))
        return copies

    @pl.when(b == 0)
    def _():
        for cp in page_copies(0, 0):
            cp.start()

    @pl.when(b + 1 < nb)
    def _():
        for cp in page_copies(b + 1, 1 - slot):
            cp.start()

    for cp in page_copies(b, slot):
        cp.wait()

    @pl.when(b == 0)
    def _():
        for h in range(H_DIFF):
            for t in range(2):
                bias_scr[t, h * rows:(h + 1) * rows, :] = _bias_from_buckets(
                    idx_ref[t], rb_ref, h, 1.0)

    def head_rows(ref, h, n):
        return ref[pl.ds(h, n, stride=H_DIFF), :]

    def page_head(buf, p, h):
        return buf[slot, p, pl.ds(h, PAGE_SIZE, stride=H_DIFF), :]

    kpad[...] = jnp.zeros_like(kpad)
    vpad[...] = jnp.zeros_like(vpad)
    for h in range(H_DIFF):
        kpad[h, 0:dec_seq, :] = head_rows(kn_ref, h, dec_seq)
        vpad[h, 0:dec_seq, :] = head_rows(vn_ref, h, dec_seq)

    lane = lax.broadcasted_iota(jnp.int32, (dec_seq, HEAD_W), 1)
    first_map = lane < QK_HEAD_DIM
    for h in range(H_DIFF):
        qh = q_ref[:, h * HEAD_W:(h + 1) * HEAD_W]
        qm = jnp.concatenate([jnp.where(first_map, qh, 0.0), jnp.where(first_map, 0.0, qh)],
                             axis=0).astype(BF16)
        r0, r1 = h * rows, (h + 1) * rows
        for p in range(0, n_pages, 2):
            k2 = jnp.concatenate([page_head(kbuf, p, h), page_head(kbuf, p + 1, h)],
                                 axis=0).astype(BF16)
            s_scr[r0:r1, p * PAGE_SIZE:(p + 2) * PAGE_SIZE] = _nt_dot(qm, k2)
        s_scr[r0:r1, past - PAGE_SIZE:past] += bias_scr[0, r0:r1, :]
        s_scr[r0:r1, past:] = _nt_dot(qm, kpad[h].astype(BF16)) + bias_scr[1, r0:r1, :]

    s = s_scr[...]
    e = jnp.exp(s - jnp.max(s, axis=-1, keepdims=True))
    pn = e / jnp.sum(e, axis=-1, keepdims=True)
    lam = _diff_lambda(lamv_ref, lam_init)
    for h in range(H_DIFF):
        r0 = h * rows
        a = (pn[r0:r0 + dec_seq] - lam * pn[r0 + dec_seq:r0 + rows]).astype(BF16)
        acc = jnp.dot(a[:, past:], vpad[h].astype(BF16), preferred_element_type=F32)
        for p in range(n_pages):
            acc = acc + jnp.dot(a[:, p * PAGE_SIZE:(p + 1) * PAGE_SIZE],
                                page_head(vbuf, p, h).astype(BF16),
                                preferred_element_type=F32)
        o_ref[:, h * V_HEAD_DIM:(h + 1) * V_HEAD_DIM] = (
            _rms(acc, sg_ref[...], SUBLN_EPS) * (1.0 - lam_init))


def _sample_attn(q, k_new, v_new, cache_k, cache_v, page_table, rel_bias, lamv, subln_gain,
                 *, layer, dec_seq, lam_init):
    dec_batch, n_pages = page_table.shape
    assert n_pages % 2 == 0
    n_pool = cache_k.shape[1]
    page_rows = PAGE_SIZE * H_DIFF
    ck = cache_k.reshape(DEPTH, n_pool, page_rows, HEAD_W)
    cv = cache_v.reshape(DEPTH, n_pool, page_rows, V_HEAD_DIM)
    past_len = n_pages * PAGE_SIZE
    rows = 2 * dec_seq
    i = (np.arange(rows) % dec_seq)[:, None]
    c = np.arange(PAGE_SIZE)[None, :]
    idx_last = _bucket_table(past_len + i - (past_len - PAGE_SIZE + c))
    idx_new = np.where(c < dec_seq, _bucket_table(i - c), -1).astype(np.int32)
    idx = jnp.asarray(np.stack([idx_last, idx_new]))
    nkeys = past_len + PAGE_SIZE

    const = lambda shape: pl.BlockSpec(shape, lambda b, pt: (0,) * len(shape))
    per_layer = lambda shape: pl.BlockSpec((None,) + shape,
                                           lambda b, pt: (layer,) + (0,) * len(shape))
    row_spec = pl.BlockSpec((dec_seq, Q_COLS), lambda b, pt: (b, 0))
    new_spec = pl.BlockSpec((None, dec_seq * H_DIFF, HEAD_W), lambda b, pt: (layer, b, 0))
    grid_spec = pltpu.PrefetchScalarGridSpec(
        num_scalar_prefetch=1,
        grid=(dec_batch,),
        in_specs=[
            pl.BlockSpec(memory_space=pltpu.SMEM),
            per_layer((4, QK_HEAD_DIM)),
            const((2, rows, PAGE_SIZE)),
            per_layer((1, V_HEAD_DIM)),
            row_spec, new_spec, new_spec,
            pl.BlockSpec(memory_space=pl.ANY), pl.BlockSpec(memory_space=pl.ANY),
        ],
        out_specs=row_spec,
        scratch_shapes=[
            pltpu.VMEM((2, n_pages, page_rows, HEAD_W), F32),
            pltpu.VMEM((2, n_pages, page_rows, V_HEAD_DIM), F32),
            pltpu.SemaphoreType.DMA((2, 2)),
            pltpu.VMEM((H_DIFF * rows, nkeys), F32),
            pltpu.VMEM((H_DIFF, PAGE_SIZE, HEAD_W), F32),
            pltpu.VMEM((H_DIFF, PAGE_SIZE, V_HEAD_DIM), F32),
            pltpu.VMEM((2, H_DIFF * rows, PAGE_SIZE), F32),
        ],
    )
    return pl.pallas_call(
        functools.partial(_sample_attn_kernel, layer=layer, n_pages=n_pages, dec_seq=dec_seq,
                          lam_init=lam_init),
        grid_spec=grid_spec,
        out_shape=jax.ShapeDtypeStruct((dec_batch * dec_seq, DIFF_WIDTH), F32),
        compiler_params=pltpu.CompilerParams(
            dimension_semantics=("arbitrary",), vmem_limit_bytes=V7X_VMEM_LIMIT_BYTES),
        name="sample_attn",
    )(page_table, rel_bias, lamv, idx, subln_gain, q, k_new, v_new, ck, cv)


def _mix_and_prenorm(x_ref, ob_ref, oa_ref, wo_ref, g_post_ref, g_pre_ref):
    mix = jnp.dot(ob_ref[...].astype(BF16), wo_ref[0:DIFF_WIDTH, :], preferred_element_type=F32)
    mix = mix + jnp.dot(oa_ref[...].astype(BF16), wo_ref[DIFF_WIDTH:, :],
                        preferred_element_type=F32)
    x1 = x_ref[...] + _rms(mix, g_post_ref[...], NORM_EPS)
    return x1, _rms(x1, g_pre_ref[...], NORM_EPS)


def _conv_gate(cw, cb, h, h1, h2):
    return cb + cw[0:1] * h2 + cw[1:2] * h1 + cw[2:3] * h


def _ffn_prompt_kernel(x_ref, ob_ref, oa_ref, wo_ref, g_post_ref, g_pre_ref, g_out_ref,
                       wu_ref, cw_ref, cb_ref, wd_ref, y_ref, st_ref, perm, carry, act,
                       *, tm):
    nsub = tm // TS
    nv = TS // 8
    nj = D_FF // TF
    ncb = D_MODEL // LANE

    @pl.when(pl.program_id(1) == 0)
    def _():
        carry[...] = jnp.zeros_like(carry)

    def pre(t):
        rows = slice(t * TS, (t + 1) * TS)
        x1, xn2 = _mix_and_prenorm(x_ref.at[rows], ob_ref.at[rows], oa_ref.at[rows], wo_ref,
                                   g_post_ref, g_pre_ref)
        y_ref[rows, :] = x1
        for c in range(ncb):
            for s in range(8):
                perm[t, c, s * PITCH:s * PITCH + nv, :] = xn2[s * nv:(s + 1) * nv,
                                                              c * LANE:(c + 1) * LANE]
        return jnp.concatenate(
            [jnp.concatenate([perm[t, c, pl.ds(v, 8, stride=PITCH), :] for c in range(ncb)],
                             axis=1)
             for v in range(nv)], axis=0).astype(BF16)

    first_row = lax.broadcasted_iota(jnp.int32, (8, TF), 0) == 0

    def conv_ffn(t, xp):
        def up(j):
            return [jnp.dot(xp, wu_ref[:, half * D_FF + j * TF:half * D_FF + (j + 1) * TF],
                            preferred_element_type=F32) for half in range(2)]

        h_next = up(0)
        for j in range(nj):
            h_cur = h_next
            if j + 1 < nj:
                h_next = up(j + 1)
            conv = []
            for half in range(2):
                c0 = half * D_FF + j * TF
                c1 = c0 + TF
                h = h_cur[half]
                prev = carry[:, c0:c1]
                head = [jnp.where(first_row, pltpu.roll(prev[8 * i:8 * i + 8], 1, 0),
                                  pltpu.roll(h[TS - 16 + 8 * i:TS - 8 + 8 * i], 1, 0))
                        for i in range(2)]
                h1 = jnp.concatenate([head[1], h[:TS - 8]], axis=0)
                h2 = jnp.concatenate([head[0], head[1], h[:TS - 16]], axis=0)
                carry[:, c0:c1] = h[TS - 16:]
                conv.append(_conv_gate(cw_ref[:, c0:c1], cb_ref[:, c0:c1], h, h1, h2))
            act[t, :, j * TF:(j + 1) * TF] = (jax.nn.silu(conv[0]) * conv[1]).astype(BF16)

    def post(t):
        rows = slice(t * TS, (t + 1) * TS)
        fp = jnp.dot(act[t], wd_ref[...], preferred_element_type=F32)
        for v in range(nv):
            for c in range(ncb):
                perm[t, c, pl.ds(v, 8, stride=PITCH), :] = fp[8 * v:8 * v + 8,
                                                              c * LANE:(c + 1) * LANE]
        f = jnp.concatenate(
            [jnp.concatenate([perm[t, c, s * PITCH:s * PITCH + nv, :] for s in range(8)], axis=0)
             for c in range(ncb)], axis=1)
        y_ref[rows, :] = y_ref[rows, :] + _rms(f, g_out_ref[...], NORM_EPS)

    xps = [pre(t) for t in range(nsub)]
    for t in range(nsub):
        conv_ffn(t, xps[t])
        post(t)
    st_ref[0, 0:1, :] = carry[7:8, :]
    st_ref[0, 1:2, :] = carry[15:16, :]


def _ffn_prompt(x, o_b, o_a, w_out, g_post, g_pre, g_out, w_up, conv_w, conv_b, w_down,
                *, layer, batch, seq):
    tm = TM_FFN
    nt = seq // tm
    row_spec = lambda w: pl.BlockSpec((tm, w), lambda b, i: (b * nt + i, 0))
    per_layer = functools.partial(_layer_spec, layer=layer)
    return pl.pallas_call(
        functools.partial(_ffn_prompt_kernel, tm=tm),
        grid=(batch, nt),
        in_specs=[
            row_spec(D_MODEL), row_spec(DIFF_WIDTH), row_spec(A_WIDTH),
            per_layer((D_MODEL, D_MODEL)),
            per_layer((1, D_MODEL)), per_layer((1, D_MODEL)), per_layer((1, D_MODEL)),
            per_layer((D_MODEL, 2 * D_FF)),
            per_layer((CONV_W, 2 * D_FF)),
            per_layer((1, 2 * D_FF)),
            per_layer((D_FF, D_MODEL)),
        ],
        out_specs=[row_spec(D_MODEL),
                   pl.BlockSpec((1, CONV_W - 1, 2 * D_FF), lambda b, i: (b, 0, 0))],
        out_shape=[jax.ShapeDtypeStruct((batch * seq, D_MODEL), F32),
                   jax.ShapeDtypeStruct((batch, CONV_W - 1, 2 * D_FF), F32)],
        scratch_shapes=[pltpu.VMEM((tm // TS, D_MODEL // LANE, 8 * PITCH, LANE), F32),
                        pltpu.VMEM((16, 2 * D_FF), F32),
                        pltpu.VMEM((tm // TS, TS, D_FF), BF16)],
        compiler_params=pltpu.CompilerParams(
            dimension_semantics=("arbitrary", "arbitrary"),
            vmem_limit_bytes=V7X_VMEM_LIMIT_BYTES),
        name="ffn_prompt",
    )(x, o_b, o_a, w_out, g_post, g_pre, g_out, w_up, conv_w, conv_b, w_down)


def _ffn_sample_kernel(x_ref, ob_ref, oa_ref, wo_ref, g_post_ref, g_pre_ref, g_out_ref,
                       wug_ref, wuu_ref, cwg_ref, cwu_ref, cbg_ref, cbu_ref, wd_ref,
                       stg_ref, stu_ref, y_ref, sog_ref, sou_ref, x1_scr, xn_scr, acc_scr,
                       *, dec_batch, dec_seq):
    j = pl.program_id(0)
    rows = dec_batch * dec_seq

    @pl.when(j == 0)
    def _():
        x1, xn2 = _mix_and_prenorm(x_ref, ob_ref, oa_ref, wo_ref, g_post_ref, g_pre_ref)
        x1_scr[...] = x1
        xn_scr[...] = xn2.astype(BF16)
        acc_scr[...] = jnp.zeros_like(acc_scr)

    xn2 = xn_scr[...]
    conv = []
    for wu_ref, cw_ref, cb_ref, st_ref, so_ref in ((wug_ref, cwg_ref, cbg_ref, stg_ref, sog_ref),
                                                   (wuu_ref, cwu_ref, cbu_ref, stu_ref, sou_ref)):
        h = jnp.dot(xn2, wu_ref[...], preferred_element_type=F32)
        s0 = st_ref[0]
        s1 = st_ref[1]
        h1 = jnp.concatenate([s1, h[:rows - dec_batch]], axis=0)
        h2 = jnp.concatenate([s0, s1, h[:rows - 2 * dec_batch]], axis=0)
        conv.append(_conv_gate(cw_ref[...], cb_ref[...], h, h1, h2))
        so_ref[0] = h[rows - 2 * dec_batch:rows - dec_batch]
        so_ref[1] = h[rows - dec_batch:]
    act = (jax.nn.silu(conv[0]) * conv[1]).astype(BF16)
    acc_scr[...] += jnp.dot(act, wd_ref[...], preferred_element_type=F32)

    @pl.when(j == pl.num_programs(0) - 1)
    def _():
        y_ref[...] = x1_scr[...] + _rms(acc_scr[...], g_out_ref[...], NORM_EPS)


def _ffn_sample(x, o_b, o_a, w_out, g_post, g_pre, g_out, w_up, conv_w, conv_b, w_down, state,
                *, layer, dec_batch, dec_seq):
    rows = dec_batch * dec_seq
    nj = D_FF // TF
    ns = CONV_W - 1

    def col_tile(shape, ax, first):
        return pl.BlockSpec((None,) + shape,
                            lambda j: (layer,) + tuple(first + j if a == ax else 0
                                                       for a in range(len(shape))))

    gate = functools.partial(col_tile, first=0)
    up = functools.partial(col_tile, first=nj)
    per_layer = functools.partial(_layer_spec, layer=layer)
    state_out = pl.BlockSpec((ns, dec_batch, TF), lambda j: (0, 0, j))
    sds = jax.ShapeDtypeStruct
    y, st_gate, st_up = pl.pallas_call(
        functools.partial(_ffn_sample_kernel, dec_batch=dec_batch, dec_seq=dec_seq),
        grid=(nj,),
        in_specs=[
            _const_spec((rows, D_MODEL)), _const_spec((rows, DIFF_WIDTH)),
            _const_spec((rows, A_WIDTH)),
            per_layer((D_MODEL, D_MODEL)),
            per_layer((1, D_MODEL)), per_layer((1, D_MODEL)), per_layer((1, D_MODEL)),
            gate((D_MODEL, TF), 1), up((D_MODEL, TF), 1),
            gate((CONV_W, TF), 1), up((CONV_W, TF), 1),
            gate((1, TF), 1), up((1, TF), 1),
            gate((TF, D_MODEL), 0),
            gate((ns, dec_batch, TF), 2), up((ns, dec_batch, TF), 2),
        ],
        out_specs=[pl.BlockSpec((rows, D_MODEL), lambda j: (0, 0)), state_out, state_out],
        out_shape=[sds((rows, D_MODEL), F32), sds((ns, dec_batch, D_FF), F32),
                   sds((ns, dec_batch, D_FF), F32)],
        scratch_shapes=[pltpu.VMEM((rows, D_MODEL), F32), pltpu.VMEM((rows, D_MODEL), BF16),
                        pltpu.VMEM((rows, D_MODEL), F32)],
        compiler_params=pltpu.CompilerParams(
            dimension_semantics=("arbitrary",), vmem_limit_bytes=V7X_VMEM_LIMIT_BYTES),
        name="ffn_sample",
    )(x, o_b, o_a, w_out, g_post, g_pre, g_out, w_up, w_up, conv_w, conv_w, conv_b, conv_b,
      w_down, state, state)
    return y, jnp.concatenate([st_gate, st_up], axis=-1)


def kernel(x_prompt, x_sample, cache_k, cache_v, state_conv, page_table, rel_bias, norm_mix_pre, norm_mix_post, norm_ffn_pre, norm_ffn_post, w_in, lambda_q1, lambda_k1, lambda_q2, lambda_k2, subln_gain, sgu_ln_gain, sgu_ln_bias, w_spatial, b_spatial, w_out, w_up, conv_w, conv_b, w_down):
    batch, seq, _ = x_prompt.shape
    dec_batch, dec_seq, _ = x_sample.shape
    assert seq % TQ == 0 and seq % TM_FFN == 0 and (batch * seq) % TM_IN == 0
    assert (dec_batch * dec_seq) % TM_IN == 0 and CHUNK % dec_seq == 0 and TQ == TK

    def to_pm(a):
        return a.reshape(dec_batch, dec_seq, -1).transpose(1, 0, 2).reshape(dec_batch * dec_seq, -1)

    def to_bm(a):
        return a.reshape(dec_seq, dec_batch, -1).transpose(1, 0, 2).reshape(dec_batch * dec_seq, -1)

    y_p = x_prompt.reshape(batch * seq, D_MODEL)
    y_s = x_sample.reshape(dec_batch * dec_seq, D_MODEL)
    reps = CHUNK // dec_seq

    rows3 = lambda a: a.reshape(DEPTH, 1, -1)
    w_in_b, w_out_b = w_in.astype(BF16), w_out.astype(BF16)
    w_up_b, w_down_b = w_up.astype(BF16), w_down.astype(BF16)
    lamv = jnp.stack([lambda_q1, lambda_k1, lambda_q2, lambda_k2], axis=1)
    sg_rows = jnp.broadcast_to(subln_gain[:, :, None], (DEPTH, V_HEAD_DIM, TQ))
    ffn_args = (w_out_b, rows3(norm_mix_post), rows3(norm_ffn_pre), rows3(norm_ffn_post),
                w_up_b, conv_w, rows3(conv_b), w_down_b)
    in_args = (rows3(norm_mix_pre), w_in_b, rows3(sgu_ln_gain), rows3(sgu_ln_bias))
    gate_shape = (DEPTH, A_GROUPS, CHUNK, A_GROUP_DIM)
    bsp_p = jnp.broadcast_to(b_spatial[:, :, :, None], gate_shape)
    wsp_s = jnp.tile(w_spatial[:, :, :dec_seq, :dec_seq], (1, 1, reps, reps))
    bsp_s = jnp.broadcast_to(jnp.tile(b_spatial[:, :, :dec_seq], (1, 1, reps))[:, :, :, None],
                             gate_shape)
    state = state_conv.transpose(0, 2, 1, 3)

    cp, csm, svs = [], [], []
    kv_p = kv_s = None
    for l in range(DEPTH):
        lam_init = _lambda_init(l)

        qt, kf, vf, kb, vt, o_a = _inproj(y_p, *in_args, w_spatial, bsp_p, kv_p,
                                          layer=l, sample=False)
        kv_p = (kf, vf)
        o_b = _prompt_attn(qt, kb, vt, rel_bias, lamv, sg_rows,
                           layer=l, batch=batch, seq=seq, lam_init=lam_init)
        y_p, conv_p = _ffn_prompt(y_p, o_b, o_a, *ffn_args, layer=l, batch=batch, seq=seq)
        cp.append(conv_p)

        q, kf, vf, o_a, vn = _inproj(y_s, *in_args, wsp_s, bsp_s, kv_s, layer=l, sample=True)
        kv_s = (kf, vf)
        o_b = _sample_attn(q, kf, vf, cache_k, cache_v, page_table, rel_bias, lamv,
                           rows3(subln_gain), layer=l, dec_seq=dec_seq, lam_init=lam_init)
        y_pm, conv_s = _ffn_sample(to_pm(y_s), to_pm(o_b), to_pm(o_a), *ffn_args, state,
                                   layer=l, dec_batch=dec_batch, dec_seq=dec_seq)
        y_s = to_bm(y_pm)
        csm.append(conv_s.transpose(1, 0, 2))
        svs.append(vn.reshape(dec_batch, dec_seq, A_WIDTH))

    kv_shape_p = (DEPTH, batch, seq, H_DIFF, HEAD_W)
    kv_shape_s = (DEPTH, dec_batch, dec_seq, H_DIFF, HEAD_W)
    return (y_p.reshape(batch, seq, D_MODEL), y_s.reshape(dec_batch, dec_seq, D_MODEL),
            kv_p[0].reshape(kv_shape_p), kv_p[1].reshape(kv_shape_p), jnp.stack(cp),
            kv_s[0].reshape(kv_shape_s), kv_s[1].reshape(kv_shape_s),
            jnp.stack(csm), jnp.stack(svs))
```

```python
import functools
import math

import jax
import jax.numpy as jnp
import numpy as np
from jax import lax
from jax.experimental import pallas as pl
from jax.experimental.pallas import tpu as pltpu

F32 = jnp.float32
BF16 = jnp.bfloat16

D_MODEL = 1024
DEPTH = 2
PAGE_SIZE = 128
H_DIFF = 4
QK_HEAD_DIM = 64
V_HEAD_DIM = 128
HEAD_W = 2 * QK_HEAD_DIM
DIFF_WIDTH = H_DIFF * V_HEAD_DIM
A_GROUPS = 4
A_WIDTH = 512
A_GROUP_DIM = 128
CHUNK = 128
LANE = 128
Q_COLS = H_DIFF * HEAD_W
IN_COLS = 3 * Q_COLS + 2 * A_WIDTH
D_FF = 2816
CONV_W = 3
REL_BUCKETS = 32
REL_MAX_EXACT = 16
REL_MAX_DIST = 128
NORM_EPS = 1e-6
SUBLN_EPS = 1e-5
NEG_INF = -1e30
LOG2_E = math.log2(math.e)

V7X_VMEM_LIMIT_BYTES = 56 * 1024 * 1024

TM_IN = 1024
TQ = 256
TK = 256
DENOM_ROWS = 16
SPS = 2
TM_FFN = 512
TS = 256
PITCH = TS // 8 + 8
TF = 256


def _lambda_init(layer):
    return 0.8 - 0.6 * math.exp(-0.3 * layer)


def _bucket_table(dist):
    n = np.maximum(dist, 0)
    nf = np.maximum(n, 1).astype(np.float32)
    large = REL_MAX_EXACT + (np.log(nf / REL_MAX_EXACT) / math.log(REL_MAX_DIST / REL_MAX_EXACT)
                             * (REL_BUCKETS - REL_MAX_EXACT)).astype(np.int32)
    large = np.minimum(large, REL_BUCKETS - 1)
    return np.where(dist >= 0, np.where(n < REL_MAX_EXACT, n, large), -1).astype(np.int32)


def _const_spec(shape):
    nd = len(shape)
    return pl.BlockSpec(shape, lambda *_: (0,) * nd, pipeline_mode=pl.Buffered(1))


def _layer_spec(shape, layer):
    nd = len(shape)
    return pl.BlockSpec((None,) + tuple(shape), lambda *_: (layer,) + (0,) * nd,
                        pipeline_mode=pl.Buffered(1))


def _rms(x, gain, eps):
    y = x * lax.rsqrt(jnp.mean(x * x, axis=-1, keepdims=True) + eps)
    return y * gain


def _nt_dot(a, b):
    return lax.dot_general(a, b, (((1,), (1,)), ((), ())), preferred_element_type=F32)


def _diff_lambda(lamv_ref, lam_init):
    p = lamv_ref[...]
    a = jnp.sum(p[0:1] * p[1:2], axis=-1, keepdims=True)
    b = jnp.sum(p[2:3] * p[3:4], axis=-1, keepdims=True)
    return jnp.exp(a) - jnp.exp(b) + lam_init


def _bias_from_buckets(idx, rb_ref, head, scale):
    far = rb_ref[REL_BUCKETS - 1, head]
    tbl = jnp.full(idx.shape, NEG_INF, F32)
    for b in range(REL_BUCKETS):
        tbl = jnp.where(idx == b, (rb_ref[b, head] - far) * scale, tbl)
    return tbl


def _inproj_kernel(x_ref, g_ref, w_ref, lng_ref, lnb_ref, wsp_ref, bsp_ref, *out_refs,
                   tm, sample, n_aliased):
    out_refs = out_refs[n_aliased:]
    if sample:
        q_ref, kf_ref, vf_ref, oa_ref, vn_ref = out_refs
    else:
        qt_ref, kf_ref, vf_ref, kb_ref, vt_ref, oa_ref = out_refs

    nsub = tm // TS
    a0 = 3 * Q_COLS

    def norm(t):
        return _rms(x_ref[t * TS:(t + 1) * TS, :], g_ref[...], NORM_EPS).astype(BF16)

    def proj(xn, c0, c1):
        return jnp.dot(xn, w_ref[:, c0:c1], preferred_element_type=F32)

    def qkv(t, xn):
        rows = slice(t * TS, (t + 1) * TS)
        q = proj(xn, 0, Q_COLS)
        k = proj(xn, Q_COLS, 2 * Q_COLS)
        v = proj(xn, 2 * Q_COLS, 3 * Q_COLS)
        for h in range(H_DIFF):
            head_rows = pl.ds(t * TS * H_DIFF + h, TS, stride=H_DIFF)
            for d in range(kf_ref.shape[0]):
                kf_ref[d, head_rows, :] = k[:, h * HEAD_W:(h + 1) * HEAD_W]
                vf_ref[d, head_rows, :] = v[:, h * V_HEAD_DIM:(h + 1) * V_HEAD_DIM]
        if sample:
            q_ref[rows, :] = q * (QK_HEAD_DIM ** -0.5)
        else:
            qt_ref[:, rows] = (q * (QK_HEAD_DIM ** -0.5 * LOG2_E)).T.astype(BF16)
            vt_ref[:, rows] = v.T.astype(BF16)
            kb_ref[rows, :] = k.astype(BF16)

    def a_proj(xn):
        return proj(xn, a0, a0 + A_WIDTH), proj(xn, a0 + A_WIDTH, a0 + 2 * A_WIDTH)

    row = lax.broadcasted_iota(jnp.int32, (CHUNK, CHUNK), 0)
    col = lax.broadcasted_iota(jnp.int32, (CHUNK, CHUNK), 1)
    if sample:
        keep = jnp.logical_and((row >> 3) == (col >> 3), (col & 7) <= (row & 7))
    else:
        keep = col <= row
    w_mix = [jnp.where(keep, wsp_ref[g], 0.0).astype(BF16) for g in range(A_GROUPS)]

    def gate(t, u_raw, v_raw):
        u_a = jax.nn.gelu(u_raw)
        v_a = jax.nn.gelu(v_raw)
        xc = v_a - jnp.mean(v_a, axis=-1, keepdims=True)
        vn = xc * lax.rsqrt(jnp.mean(xc * xc, axis=-1, keepdims=True) + NORM_EPS)
        vn = vn * lng_ref[...] + lnb_ref[...]
        if sample:
            vn_ref[t * TS:(t + 1) * TS, :] = vn
        vnb = vn.astype(BF16)
        for g in range(A_GROUPS):
            c0, c1 = g * A_GROUP_DIM, (g + 1) * A_GROUP_DIM
            for c in range(TS // CHUNK):
                r0, r1 = c * CHUNK, (c + 1) * CHUNK
                s = jnp.dot(w_mix[g], vnb[r0:r1, c0:c1], preferred_element_type=F32) + bsp_ref[g]
                oa_ref[t * TS + r0:t * TS + r1, c0:c1] = (u_a[r0:r1, c0:c1] * s).astype(
                    oa_ref.dtype)

    xns = [norm(t) for t in range(nsub)]
    pending = None
    for t in range(nsub):
        qkv(t, xns[t])
        a_cur = a_proj(xns[t])
        if pending is not None:
            gate(t - 1, *pending)
        pending = a_cur
    gate(nsub - 1, *pending)


def _inproj(x, gain, w_in, ln_g, ln_b, wsp, bsp, kv_all, *, layer, sample):
    rows = x.shape[0]
    tm = TM_IN
    row_spec = lambda w: pl.BlockSpec((tm, w), lambda i: (i, 0))
    assert kv_all is not None or layer == 0
    n_slabs = DEPTH if kv_all is None else 1
    head_row_spec = pl.BlockSpec((n_slabs, tm * H_DIFF, HEAD_W),
                                 lambda i: (layer // n_slabs, i, 0))
    col_spec = pl.BlockSpec((Q_COLS, tm), lambda i: (0, i))
    in_specs = [
        row_spec(D_MODEL),
        _layer_spec((1, D_MODEL), layer),
        _layer_spec((D_MODEL, IN_COLS), layer),
        _layer_spec((1, A_WIDTH), layer),
        _layer_spec((1, A_WIDTH), layer),
        _layer_spec((A_GROUPS, CHUNK, CHUNK), layer),
        _layer_spec((A_GROUPS, CHUNK, A_GROUP_DIM), layer),
    ]
    sds = jax.ShapeDtypeStruct
    kv_shape = sds((DEPTH, rows * H_DIFF, HEAD_W), F32)
    aliased = () if kv_all is None else tuple(kv_all)
    n_in = len(in_specs)
    in_specs += [pl.BlockSpec(memory_space=pl.ANY)] * len(aliased)
    aliases = {n_in + i: 1 + i for i in range(len(aliased))}
    if sample:
        out_shape = [sds((rows, Q_COLS), F32), kv_shape, kv_shape,
                     sds((rows, A_WIDTH), F32), sds((rows, A_WIDTH), F32)]
        out_specs = [row_spec(Q_COLS), head_row_spec, head_row_spec,
                     row_spec(A_WIDTH), row_spec(A_WIDTH)]
    else:
        out_shape = [sds((Q_COLS, rows), BF16), kv_shape, kv_shape,
                     sds((rows, Q_COLS), BF16), sds((Q_COLS, rows), BF16),
                     sds((rows, A_WIDTH), BF16)]
        out_specs = [col_spec, head_row_spec, head_row_spec,
                     row_spec(Q_COLS), col_spec, row_spec(A_WIDTH)]
    return pl.pallas_call(
        functools.partial(_inproj_kernel, tm=tm, sample=sample, n_aliased=len(aliased)),
        grid=(rows // tm,),
        in_specs=in_specs,
        out_specs=out_specs,
        out_shape=out_shape,
        input_output_aliases=aliases,
        compiler_params=pltpu.CompilerParams(
            dimension_semantics=("arbitrary",), vmem_limit_bytes=V7X_VMEM_LIMIT_BYTES),
        name="inproj_sample" if sample else "inproj_prompt",
    )(x, gain, w_in, ln_g, ln_b, wsp, bsp, *aliased)


def _prompt_attn_step(first, qi, rb_ref, lamv_ref, idx_ref, sg_ref, qt_ref, k_ref, vt_ref, o_ref,
                      bias_scr, qw_scr, m_scr, acc_scr, *, lam_init):
    @pl.when(first)
    def _():
        for h in range(H_DIFF):
            for t in range(2):
                bias_scr[h, t] = _bias_from_buckets(idx_ref[t], rb_ref, h, LOG2_E)

    drow = lax.broadcasted_iota(jnp.int32, (HEAD_W, TQ), 0)
    first_map = drow < QK_HEAD_DIM
    for h in range(H_DIFF):
        qh = qt_ref[h * HEAD_W:(h + 1) * HEAD_W, :]
        zero = jnp.zeros_like(qh)
        qw_scr[h] = jnp.concatenate(
            [jnp.where(first_map, qh, zero), jnp.where(first_map, zero, qh)], axis=1)
    m_scr[...] = jnp.full_like(m_scr, NEG_INF)
    acc_scr[...] = jnp.zeros_like(acc_scr)
    ones_rows = jnp.ones((DENOM_ROWS, TK), BF16)

    def key_tiles(tiles):
        units = [(pl.multiple_of(j * TK, TK), kind, h) for j, kind in tiles for h in range(H_DIFF)]

        def scores(u):
            r0, _, h = u
            kj = k_ref[pl.ds(r0, TK), h * HEAD_W:(h + 1) * HEAD_W]
            return jnp.dot(kj, qw_scr[h], preferred_element_type=F32)

        def softmax(u, st):
            _, kind, h = u
            if kind is not None:
                bias = bias_scr[h, kind]
                st = jnp.concatenate([bias, bias], axis=1) + st
            m_old = m_scr[h]
            m_new = jnp.maximum(m_old, jnp.max(st, axis=0, keepdims=True))
            m_scr[h] = m_new
            return jnp.exp2(m_old - m_new), jnp.exp2(st - m_new).astype(BF16)

        def accumulate(u, alpha, pt):
            r0, _, h = u
            vtj = jnp.concatenate([vt_ref[h * HEAD_W:(h + 1) * HEAD_W, pl.ds(r0, TK)],
                                   ones_rows], axis=0)
            acc_scr[h] = alpha * acc_scr[h] + jnp.dot(vtj, pt, preferred_element_type=F32)

        n = len(units)
        st = {0: scores(units[0]), 1: scores(units[1])}
        p = {}
        for i in range(n):
            p[i] = softmax(units[i], st.pop(i))
            if i + 2 < n:
                st[i + 2] = scores(units[i + 2])
            if i >= 1:
                accumulate(units[i - 1], *p.pop(i - 1))
        accumulate(units[n - 1], *p.pop(n - 1))

    n_far = jnp.maximum(qi - 1, 0)

    def far_pair(jj, carry):
        key_tiles([(2 * jj, None), (2 * jj + 1, None)])
        return carry

    lax.fori_loop(0, n_far // 2, far_pair, 0)

    @pl.when(n_far % 2 == 1)
    def _():
        key_tiles([(n_far - 1, None)])

    @pl.when(qi >= 1)
    def _():
        key_tiles([(qi - 1, 1), (qi, 0)])

    @pl.when(qi == 0)
    def _():
        key_tiles([(qi, 0)])

    lam = _diff_lambda(lamv_ref, lam_init)
    for h in range(H_DIFF):
        acc = acc_scr[h, 0:V_HEAD_DIM, :]
        rl = 1.0 / acc_scr[h, V_HEAD_DIM:V_HEAD_DIM + 1, :]
        ot = acc[:, :TQ] * rl[:, :TQ] - lam * (acc[:, TQ:] * rl[:, TQ:])
        ot = ot * lax.rsqrt(jnp.mean(ot * ot, axis=0, keepdims=True) + SUBLN_EPS)
        ot = ot * sg_ref[...] * (1.0 - lam_init)
        o_ref[:, h * V_HEAD_DIM:(h + 1) * V_HEAD_DIM] = ot.T.astype(o_ref.dtype)


def _sample_page_copies(step, sl, pt_ref, ck_hbm, cv_hbm, kbuf, vbuf, sem, *, layer, n_pages):
    copies = []
    for i in range(SPS):
        for p in range(n_pages):
            page = pt_ref[step * SPS + i, p]
            copies.append(pltpu.make_async_copy(ck_hbm.at[layer, page], kbuf.at[sl, i, p],
                                                sem.at[sl, 0]))
            copies.append(pltpu.make_async_copy(cv_hbm.at[layer, page], vbuf.at[sl, i, p],
                                                sem.at[sl, 1]))
    return copies


def _sample_attn_step(step, first, slot, rb_ref, lamv_ref, idx_ref, sg_ref, q_ref, kn_ref,
                      vn_ref, o_ref, kbuf, vbuf, s_scr, kpad, vpad, bias_scr,
                      *, n_pages, dec_seq, lam_init):
    del step
    rows = 2 * dec_seq
    past = n_pages * PAGE_SIZE

    @pl.when(first)
    def _():
        for h in range(H_DIFF):
            for t in range(2):
                bias_scr[t, h * rows:(h + 1) * rows, :] = _bias_from_buckets(
                    idx_ref[t], rb_ref, h, 1.0)

    lane = lax.broadcasted_iota(jnp.int32, (dec_seq, HEAD_W), 1)
    first_map = lane < QK_HEAD_DIM
    lam = _diff_lambda(lamv_ref, lam_init)

    def one_sequence(i, carry):
        _sample_attn_sequence(i)
        return carry

    def _sample_attn_sequence(i):
        q0 = pl.multiple_of(i * dec_seq, dec_seq)

        def new_rows(ref, h):
            return ref[pl.ds(i * (dec_seq * H_DIFF) + h, dec_seq, stride=H_DIFF), :]

        def page_head(buf, p, h):
            return buf[slot, i, p, pl.ds(h, PAGE_SIZE, stride=H_DIFF), :]

        kpad[...] = jnp.zeros_like(kpad)
        vpad[...] = jnp.zeros_like(vpad)
        for h in range(H_DIFF):
            kpad[h, 0:dec_seq, :] = new_rows(kn_ref, h)
            vpad[h, 0:dec_seq, :] = new_rows(vn_ref, h)

        for h in range(H_DIFF):
            qh = q_ref[pl.ds(q0, dec_seq), h * HEAD_W:(h + 1) * HEAD_W]
            qm = jnp.concatenate([jnp.where(first_map, qh, 0.0), jnp.where(first_map, 0.0, qh)],
                                 axis=0).astype(BF16)
            r0, r1 = h * rows, (h + 1) * rows
            for p in range(0, n_pages, 2):
                k2 = jnp.concatenate([page_head(kbuf, p, h), page_head(kbuf, p + 1, h)],
                                     axis=0).astype(BF16)
                s_scr[r0:r1, p * PAGE_SIZE:(p + 2) * PAGE_SIZE] = _nt_dot(qm, k2)
            s_scr[r0:r1, past - PAGE_SIZE:past] += bias_scr[0, r0:r1, :]
            s_scr[r0:r1, past:] = _nt_dot(qm, kpad[h].astype(BF16)) + bias_scr[1, r0:r1, :]

        s = s_scr[...]
        e = jnp.exp(s - jnp.max(s, axis=-1, keepdims=True))
        pn = e / jnp.sum(e, axis=-1, keepdims=True)
        for h in range(H_DIFF):
            r0 = h * rows
            a = (pn[r0:r0 + dec_seq] - lam * pn[r0 + dec_seq:r0 + rows]).astype(BF16)
            acc = jnp.dot(a[:, past:], vpad[h].astype(BF16), preferred_element_type=F32)
            for p in range(n_pages):
                acc = acc + jnp.dot(a[:, p * PAGE_SIZE:(p + 1) * PAGE_SIZE],
                                    page_head(vbuf, p, h).astype(BF16),
                                    preferred_element_type=F32)
            o_ref[pl.ds(q0, dec_seq), h * V_HEAD_DIM:(h + 1) * V_HEAD_DIM] = (
                _rms(acc, sg_ref[...], SUBLN_EPS) * (1.0 - lam_init))

    lax.fori_loop(0, SPS, one_sequence, 0)


def _attention_kernel(pt_ref, rb_ref, lamv_ref,
                      idx_p_ref, sg_p_ref, qt_ref, k_ref, vt_ref,
                      idx_s_ref, sg_s_ref, q_ref, kn_ref, vn_ref, ck_hbm, cv_hbm,
                      op_ref, os_ref,
                      bias_p, qw_scr, m_scr, acc_scr,
                      kbuf, vbuf, sem, s_scr, kpad, vpad, bias_s,
                      *, layer, n_pages, dec_seq, lam_init):
    qi = pl.program_id(1)
    step = pl.program_id(0) * pl.num_programs(1) + qi
    nsteps = pl.num_programs(0) * pl.num_programs(1)
    first = step == 0
    slot = lax.rem(step, 2)
    copies = functools.partial(_sample_page_copies, pt_ref=pt_ref, ck_hbm=ck_hbm, cv_hbm=cv_hbm,
                               kbuf=kbuf, vbuf=vbuf, sem=sem, layer=layer, n_pages=n_pages)

    @pl.when(first)
    def _():
        for cp in copies(0, 0):
            cp.start()

    @pl.when(step + 1 < nsteps)
    def _():
        for cp in copies(step + 1, 1 - slot):
            cp.start()

    _prompt_attn_step(first, qi, rb_ref, lamv_ref, idx_p_ref, sg_p_ref, qt_ref, k_ref, vt_ref,
                      op_ref, bias_p, qw_scr, m_scr, acc_scr, lam_init=lam_init)

    for cp in copies(step, slot):
        cp.wait()
    _sample_attn_step(step, first, slot, rb_ref, lamv_ref, idx_s_ref, sg_s_ref, q_ref, kn_ref,
                      vn_ref, os_ref, kbuf, vbuf, s_scr, kpad, vpad, bias_s,
                      n_pages=n_pages, dec_seq=dec_seq, lam_init=lam_init)


def _attention(qt, k, vt, sg_p, q_s, kv_s, cache_k, cache_v, page_table, rel_bias, lamv, sg_s,
               *, layer, batch, seq, dec_seq, lam_init):
    nq = seq // TQ
    kk = np.arange(TK)[:, None]
    qq = np.arange(TQ)[None, :]
    idx_p = jnp.asarray(np.stack([_bucket_table(qq - kk), _bucket_table(TQ + qq - kk)]))

    dec_batch, n_pages = page_table.shape
    assert n_pages % 2 == 0 and dec_batch == SPS * batch * nq
    n_pool = cache_k.shape[1]
    page_rows = PAGE_SIZE * H_DIFF
    ck = cache_k.reshape(DEPTH, n_pool, page_rows, HEAD_W)
    cv = cache_v.reshape(DEPTH, n_pool, page_rows, V_HEAD_DIM)
    past_len = n_pages * PAGE_SIZE
    rows = 2 * dec_seq
    i = (np.arange(rows) % dec_seq)[:, None]
    c = np.arange(PAGE_SIZE)[None, :]
    idx_last = _bucket_table(past_len + i - (past_len - PAGE_SIZE + c))
    idx_new = np.where(c < dec_seq, _bucket_table(i - c), -1).astype(np.int32)
    idx_s = jnp.asarray(np.stack([idx_last, idx_new]))
    nkeys = past_len + PAGE_SIZE

    step = lambda b, i: b * nq + i
    const = lambda shape: pl.BlockSpec(shape, lambda b, i, pt: (0,) * len(shape),
                                       pipeline_mode=pl.Buffered(1))
    per_layer = lambda shape: pl.BlockSpec((None,) + shape,
                                           lambda b, i, pt: (layer,) + (0,) * len(shape),
                                           pipeline_mode=pl.Buffered(1))
    srow_spec = pl.BlockSpec((SPS * dec_seq, Q_COLS), lambda b, i, pt: (step(b, i), 0))
    snew_spec = pl.BlockSpec((None, SPS * dec_seq * H_DIFF, HEAD_W),
                             lambda b, i, pt: (layer, step(b, i), 0))
    grid_spec = pltpu.PrefetchScalarGridSpec(
        num_scalar_prefetch=1,
        grid=(batch, nq),
        in_specs=[
            pl.BlockSpec(memory_space=pltpu.SMEM),
            per_layer((4, QK_HEAD_DIM)),
            const((2, TK, TQ)),
            per_layer((V_HEAD_DIM, TQ)),
            pl.BlockSpec((Q_COLS, TQ), lambda b, i, pt: (0, step(b, i))),
            pl.BlockSpec((seq, Q_COLS), lambda b, i, pt: (b, 0)),
            pl.BlockSpec((Q_COLS, seq), lambda b, i, pt: (0, b)),
            const((2, rows, PAGE_SIZE)),
            per_layer((1, V_HEAD_DIM)),
            srow_spec, snew_spec, snew_spec,
            pl.BlockSpec(memory_space=pl.ANY), pl.BlockSpec(memory_space=pl.ANY),
        ],
        out_specs=[pl.BlockSpec((TQ, DIFF_WIDTH), lambda b, i, pt: (step(b, i), 0)), srow_spec],
        scratch_shapes=[
            pltpu.VMEM((H_DIFF, 2, TK, TQ), F32),
            pltpu.VMEM((H_DIFF, HEAD_W, 2 * TQ), BF16),
            pltpu.VMEM((H_DIFF, 1, 2 * TQ), F32),
            pltpu.VMEM((H_DIFF, V_HEAD_DIM + DENOM_ROWS, 2 * TQ), F32),
            pltpu.VMEM((2, SPS, n_pages, page_rows, HEAD_W), F32),
            pltpu.VMEM((2, SPS, n_pages, page_rows, V_HEAD_DIM), F32),
            pltpu.SemaphoreType.DMA((2, 2)),
            pltpu.VMEM((H_DIFF * rows, nkeys), F32),
            pltpu.VMEM((H_DIFF, PAGE_SIZE, HEAD_W), F32),
            pltpu.VMEM((H_DIFF, PAGE_SIZE, V_HEAD_DIM), F32),
            pltpu.VMEM((2, H_DIFF * rows, PAGE_SIZE), F32),
        ],
    )
    return pl.pallas_call(
        functools.partial(_attention_kernel, layer=layer, n_pages=n_pages, dec_seq=dec_seq,
                          lam_init=lam_init),
        grid_spec=grid_spec,
        out_shape=[jax.ShapeDtypeStruct((batch * seq, DIFF_WIDTH), BF16),
                   jax.ShapeDtypeStruct((dec_batch * dec_seq, DIFF_WIDTH), F32)],
        compiler_params=pltpu.CompilerParams(
            dimension_semantics=("arbitrary", "arbitrary"),
            vmem_limit_bytes=V7X_VMEM_LIMIT_BYTES),
        name="attention",
    )(page_table, rel_bias, lamv, idx_p, sg_p, qt, k, vt, idx_s, sg_s, q_s, kv_s[0], kv_s[1],
      ck, cv)


def _mix_and_prenorm(x_ref, ob_ref, oa_ref, wo_ref, g_post_ref, g_pre_ref):
    mix = jnp.dot(ob_ref[...].astype(BF16), wo_ref[0:DIFF_WIDTH, :], preferred_element_type=F32)
    mix = mix + jnp.dot(oa_ref[...].astype(BF16), wo_ref[DIFF_WIDTH:, :],
                        preferred_element_type=F32)
    x1 = x_ref[...] + _rms(mix, g_post_ref[...], NORM_EPS)
    return x1, _rms(x1, g_pre_ref[...], NORM_EPS)


def _conv_gate(cw, cb, h, h1, h2):
    return cb + cw[0:1] * h2 + cw[1:2] * h1 + cw[2:3] * h


def _ffn_prompt_kernel(x_ref, ob_ref, oa_ref, wo_ref, g_post_ref, g_pre_ref, g_out_ref,
                       wu_ref, cw_ref, cb_ref, wd_ref, y_ref, st_ref, perm, carry, act,
                       *, tm):
    nsub = tm // TS
    nv = TS // 8
    nj = D_FF // TF
    ncb = D_MODEL // LANE

    @pl.when(pl.program_id(1) == 0)
    def _():
        carry[...] = jnp.zeros_like(carry)

    def pre(t):
        rows = slice(t * TS, (t + 1) * TS)
        x1, xn2 = _mix_and_prenorm(x_ref.at[rows], ob_ref.at[rows], oa_ref.at[rows], wo_ref,
                                   g_post_ref, g_pre_ref)
        y_ref[rows, :] = x1
        for c in range(ncb):
            for s in range(8):
                perm[t, c, s * PITCH:s * PITCH + nv, :] = xn2[s * nv:(s + 1) * nv,
                                                              c * LANE:(c + 1) * LANE]
        return jnp.concatenate(
            [jnp.concatenate([perm[t, c, pl.ds(v, 8, stride=PITCH), :] for c in range(ncb)],
                             axis=1)
             for v in range(nv)], axis=0).astype(BF16)

    first_row = lax.broadcasted_iota(jnp.int32, (8, TF), 0) == 0

    def conv_ffn(t, xp):
        def up(j):
            return [jnp.dot(xp, wu_ref[:, half * D_FF + j * TF:half * D_FF + (j + 1) * TF],
                            preferred_element_type=F32) for half in range(2)]

        h_next = up(0)
        for j in range(nj):
            h_cur = h_next
            if j + 1 < nj:
                h_next = up(j + 1)
            conv = []
            for half in range(2):
                c0 = half * D_FF + j * TF
                c1 = c0 + TF
                h = h_cur[half]
                prev = carry[:, c0:c1]
                head = [jnp.where(first_row, pltpu.roll(prev[8 * i:8 * i + 8], 1, 0),
                                  pltpu.roll(h[TS - 16 + 8 * i:TS - 8 + 8 * i], 1, 0))
                        for i in range(2)]
                h1 = jnp.concatenate([head[1], h[:TS - 8]], axis=0)
                h2 = jnp.concatenate([head[0], head[1], h[:TS - 16]], axis=0)
                carry[:, c0:c1] = h[TS - 16:]
                conv.append(_conv_gate(cw_ref[:, c0:c1], cb_ref[:, c0:c1], h, h1, h2))
            act[t, :, j * TF:(j + 1) * TF] = (jax.nn.silu(conv[0]) * conv[1]).astype(BF16)

    def post(t):
        rows = slice(t * TS, (t + 1) * TS)
        fp = jnp.dot(act[t], wd_ref[...], preferred_element_type=F32)
        for v in range(nv):
            for c in range(ncb):
                perm[t, c, pl.ds(v, 8, stride=PITCH), :] = fp[8 * v:8 * v + 8,
                                                              c * LANE:(c + 1) * LANE]
        f = jnp.concatenate(
            [jnp.concatenate([perm[t, c, s * PITCH:s * PITCH + nv, :] for s in range(8)], axis=0)
             for c in range(ncb)], axis=1)
        y_ref[rows, :] = y_ref[rows, :] + _rms(f, g_out_ref[...], NORM_EPS)

    xps = [pre(t) for t in range(nsub)]
    for t in range(nsub):
        conv_ffn(t, xps[t])
        post(t)
    st_ref[0, 0:1, :] = carry[7:8, :]
    st_ref[0, 1:2, :] = carry[15:16, :]


def _ffn_prompt(x, o_b, o_a, w_out, g_post, g_pre, g_out, w_up, conv_w, conv_b, w_down,
                *, layer, batch, seq):
    tm = TM_FFN
    nt = seq // tm
    row_spec = lambda w: pl.BlockSpec((tm, w), lambda b, i: (b * nt + i, 0))
    per_layer = functools.partial(_layer_spec, layer=layer)
    return pl.pallas_call(
        functools.partial(_ffn_prompt_kernel, tm=tm),
        grid=(batch, nt),
        in_specs=[
            row_spec(D_MODEL), row_spec(DIFF_WIDTH), row_spec(A_WIDTH),
            per_layer((D_MODEL, D_MODEL)),
            per_layer((1, D_MODEL)), per_layer((1, D_MODEL)), per_layer((1, D_MODEL)),
            per_layer((D_MODEL, 2 * D_FF)),
            per_layer((CONV_W, 2 * D_FF)),
            per_layer((1, 2 * D_FF)),
            per_layer((D_FF, D_MODEL)),
        ],
        out_specs=[row_spec(D_MODEL),
                   pl.BlockSpec((1, CONV_W - 1, 2 * D_FF), lambda b, i: (b, 0, 0))],
        out_shape=[jax.ShapeDtypeStruct((batch * seq, D_MODEL), F32),
                   jax.ShapeDtypeStruct((batch, CONV_W - 1, 2 * D_FF), F32)],
        scratch_shapes=[pltpu.VMEM((tm // TS, D_MODEL // LANE, 8 * PITCH, LANE), F32),
                        pltpu.VMEM((16, 2 * D_FF), F32),
                        pltpu.VMEM((tm // TS, TS, D_FF), BF16)],
        compiler_params=pltpu.CompilerParams(
            dimension_semantics=("arbitrary", "arbitrary"),
            vmem_limit_bytes=V7X_VMEM_LIMIT_BYTES),
        name="ffn_prompt",
    )(x, o_b, o_a, w_out, g_post, g_pre, g_out, w_up, conv_w, conv_b, w_down)


def _ffn_sample_kernel(x_ref, ob_ref, oa_ref, wo_ref, g_post_ref, g_pre_ref, g_out_ref,
                       wug_ref, wuu_ref, cwg_ref, cwu_ref, cbg_ref, cbu_ref, wd_ref,
                       stg_ref, stu_ref, y_ref, sog_ref, sou_ref, x1_scr, xn_scr, acc_scr,
                       *, dec_batch, dec_seq):
    j = pl.program_id(0)
    rows = dec_batch * dec_seq

    @pl.when(j == 0)
    def _():
        x1, xn2 = _mix_and_prenorm(x_ref, ob_ref, oa_ref, wo_ref, g_post_ref, g_pre_ref)
        x1_scr[...] = x1
        xn_scr[...] = xn2.astype(BF16)
        acc_scr[...] = jnp.zeros_like(acc_scr)

    xn2 = xn_scr[...]
    conv = []
    for wu_ref, cw_ref, cb_ref, st_ref, so_ref in ((wug_ref, cwg_ref, cbg_ref, stg_ref, sog_ref),
                                                   (wuu_ref, cwu_ref, cbu_ref, stu_ref, sou_ref)):
        h = jnp.dot(xn2, wu_ref[...], preferred_element_type=F32)
        s0 = st_ref[0]
        s1 = st_ref[1]
        h1 = jnp.concatenate([s1, h[:rows - dec_batch]], axis=0)
        h2 = jnp.concatenate([s0, s1, h[:rows - 2 * dec_batch]], axis=0)
        conv.append(_conv_gate(cw_ref[...], cb_ref[...], h, h1, h2))
        so_ref[0] = h[rows - 2 * dec_batch:rows - dec_batch]
        so_ref[1] = h[rows - dec_batch:]
    act = (jax.nn.silu(conv[0]) * conv[1]).astype(BF16)
    acc_scr[...] += jnp.dot(act, wd_ref[...], preferred_element_type=F32)

    @pl.when(j == pl.num_programs(0) - 1)
    def _():
        y_ref[...] = x1_scr[...] + _rms(acc_scr[...], g_out_ref[...], NORM_EPS)


def _ffn_sample(x, o_b, o_a, w_out, g_post, g_pre, g_out, w_up, conv_w, conv_b, w_down, state,
                *, layer, dec_batch, dec_seq):
    rows = dec_batch * dec_seq
    nj = D_FF // TF
    ns = CONV_W - 1

    def col_tile(shape, ax, first):
        return pl.BlockSpec((None,) + shape,
                            lambda j: (layer,) + tuple(first + j if a == ax else 0
                                                       for a in range(len(shape))))

    gate = functools.partial(col_tile, first=0)
    up = functools.partial(col_tile, first=nj)
    per_layer = functools.partial(_layer_spec, layer=layer)
    state_out = pl.BlockSpec((ns, dec_batch, TF), lambda j: (0, 0, j))
    sds = jax.ShapeDtypeStruct
    y, st_gate, st_up = pl.pallas_call(
        functools.partial(_ffn_sample_kernel, dec_batch=dec_batch, dec_seq=dec_seq),
        grid=(nj,),
        in_specs=[
            _const_spec((rows, D_MODEL)), _const_spec((rows, DIFF_WIDTH)),
            _const_spec((rows, A_WIDTH)),
            per_layer((D_MODEL, D_MODEL)),
            per_layer((1, D_MODEL)), per_layer((1, D_MODEL)), per_layer((1, D_MODEL)),
            gate((D_MODEL, TF), 1), up((D_MODEL, TF), 1),
            gate((CONV_W, TF), 1), up((CONV_W, TF), 1),
            gate((1, TF), 1), up((1, TF), 1),
            gate((TF, D_MODEL), 0),
            gate((ns, dec_batch, TF), 2), up((ns, dec_batch, TF), 2),
        ],
        out_specs=[pl.BlockSpec((rows, D_MODEL), lambda j: (0, 0)), state_out, state_out],
        out_shape=[sds((rows, D_MODEL), F32), sds((ns, dec_batch, D_FF), F32),
                   sds((ns, dec_batch, D_FF), F32)],
        scratch_shapes=[pltpu.VMEM((rows, D_MODEL), F32), pltpu.VMEM((rows, D_MODEL), BF16),
                        pltpu.VMEM((rows, D_MODEL), F32)],
        compiler_params=pltpu.CompilerParams(
            dimension_semantics=("arbitrary",), vmem_limit_bytes=V7X_VMEM_LIMIT_BYTES),
        name="ffn_sample",
    )(x, o_b, o_a, w_out, g_post, g_pre, g_out, w_up, w_up, conv_w, conv_w, conv_b, conv_b,
      w_down, state, state)
    return y, jnp.concatenate([st_gate, st_up], axis=-1)


def kernel(x_prompt, x_sample, cache_k, cache_v, state_conv, page_table, rel_bias, norm_mix_pre, norm_mix_post, norm_ffn_pre, norm_ffn_post, w_in, lambda_q1, lambda_k1, lambda_q2, lambda_k2, subln_gain, sgu_ln_gain, sgu_ln_bias, w_spatial, b_spatial, w_out, w_up, conv_w, conv_b, w_down):
    batch, seq, _ = x_prompt.shape
    dec_batch, dec_seq, _ = x_sample.shape
    assert seq % TQ == 0 and seq % TM_FFN == 0 and (batch * seq) % TM_IN == 0
    assert (dec_batch * dec_seq) % TM_IN == 0 and CHUNK % dec_seq == 0 and TQ == TK

    def to_pm(a):
        return a.reshape(dec_batch, dec_seq, -1).transpose(1, 0, 2).reshape(dec_batch * dec_seq, -1)

    def to_bm(a):
        return a.reshape(dec_seq, dec_batch, -1).transpose(1, 0, 2).reshape(dec_batch * dec_seq, -1)

    y_p = x_prompt.reshape(batch * seq, D_MODEL)
    y_s = x_sample.reshape(dec_batch * dec_seq, D_MODEL)
    reps = CHUNK // dec_seq

    rows3 = lambda a: a.reshape(DEPTH, 1, -1)
    w_in_b, w_out_b = w_in.astype(BF16), w_out.astype(BF16)
    w_up_b, w_down_b = w_up.astype(BF16), w_down.astype(BF16)
    lamv = jnp.stack([lambda_q1, lambda_k1, lambda_q2, lambda_k2], axis=1)
    sg_rows = jnp.broadcast_to(subln_gain[:, :, None], (DEPTH, V_HEAD_DIM, TQ))
    ffn_args = (w_out_b, rows3(norm_mix_post), rows3(norm_ffn_pre), rows3(norm_ffn_post),
                w_up_b, conv_w, rows3(conv_b), w_down_b)
    in_args = (rows3(norm_mix_pre), w_in_b, rows3(sgu_ln_gain), rows3(sgu_ln_bias))
    gate_shape = (DEPTH, A_GROUPS, CHUNK, A_GROUP_DIM)
    bsp_p = jnp.broadcast_to(b_spatial[:, :, :, None], gate_shape)
    wsp_s = jnp.tile(w_spatial[:, :, :dec_seq, :dec_seq], (1, 1, reps, reps))
    bsp_s = jnp.broadcast_to(jnp.tile(b_spatial[:, :, :dec_seq], (1, 1, reps))[:, :, :, None],
                             gate_shape)
    state = state_conv.transpose(0, 2, 1, 3)

    cp, csm, svs = [], [], []
    kv_p = kv_s = None
    for l in range(DEPTH):
        lam_init = _lambda_init(l)

        qt, kf, vf, kb, vt, o_a = _inproj(y_p, *in_args, w_spatial, bsp_p, kv_p,
                                          layer=l, sample=False)
        kv_p = (kf, vf)
        q_s, kf, vf, o_a_s, vn = _inproj(y_s, *in_args, wsp_s, bsp_s, kv_s, layer=l, sample=True)
        kv_s = (kf, vf)
        o_b, o_b_s = _attention(qt, kb, vt, sg_rows, q_s, kv_s, cache_k, cache_v, page_table,
                                rel_bias, lamv, rows3(subln_gain), layer=l, batch=batch, seq=seq,
                                dec_seq=dec_seq, lam_init=lam_init)
        y_p, conv_p = _ffn_prompt(y_p, o_b, o_a, *ffn_args, layer=l, batch=batch, seq=seq)
        cp.append(conv_p)
        y_pm, conv_s = _ffn_sample(to_pm(y_s), to_pm(o_b_s), to_pm(o_a_s), *ffn_args, state,
                                   layer=l, dec_batch=dec_batch, dec_seq=dec_seq)
        y_s = to_bm(y_pm)
        csm.append(conv_s.transpose(1, 0, 2))
        svs.append(vn.reshape(dec_batch, dec_seq, A_WIDTH))

    kv_shape_p = (DEPTH, batch, seq, H_DIFF, HEAD_W)
    kv_shape_s = (DEPTH, dec_batch, dec_seq, H_DIFF, HEAD_W)
    return (y_p.reshape(batch, seq, D_MODEL), y_s.reshape(dec_batch, dec_seq, D_MODEL),
            kv_p[0].reshape(kv_shape_p), kv_p[1].reshape(kv_shape_p), jnp.stack(cp),
            kv_s[0].reshape(kv_shape_s), kv_s[1].reshape(kv_shape_s),
            jnp.stack(csm), jnp.stack(svs))
```

```python
import functools
import math

import jax
import jax.numpy as jnp
import numpy as np
from jax import lax
from jax.experimental import pallas as pl
from jax.experimental.pallas import tpu as pltpu

F32 = jnp.float32
BF16 = jnp.bfloat16

D_MODEL = 1024
DEPTH = 2
PAGE_SIZE = 128
H_DIFF = 4
QK_HEAD_DIM = 64
V_HEAD_DIM = 128
HEAD_W = 2 * QK_HEAD_DIM
DIFF_WIDTH = H_DIFF * V_HEAD_DIM
A_GROUPS = 4
A_WIDTH = 512
A_GROUP_DIM = 128
CHUNK = 128
LANE = 128
Q_COLS = H_DIFF * HEAD_W
IN_COLS = 3 * Q_COLS + 2 * A_WIDTH
D_FF = 2816
CONV_W = 3
REL_BUCKETS = 32
REL_MAX_EXACT = 16
REL_MAX_DIST = 128
NORM_EPS = 1e-6
SUBLN_EPS = 1e-5
NEG_INF = -1e30
LOG2_E = math.log2(math.e)

V7X_VMEM_LIMIT_BYTES = 56 * 1024 * 1024

TM_IN = 1024
TQ = 256
TK = 256
DENOM_ROWS = 16
SPS = 2
TM_FFN = 512
TS = 256
PITCH = TS // 8 + 8
TF = 256


def _lambda_init(layer):
    return 0.8 - 0.6 * math.exp(-0.3 * layer)


def _bucket_table(dist):
    n = np.maximum(dist, 0)
    nf = np.maximum(n, 1).astype(np.float32)
    large = REL_MAX_EXACT + (np.log(nf / REL_MAX_EXACT) / math.log(REL_MAX_DIST / REL_MAX_EXACT)
                             * (REL_BUCKETS - REL_MAX_EXACT)).astype(np.int32)
    large = np.minimum(large, REL_BUCKETS - 1)
    return np.where(dist >= 0, np.where(n < REL_MAX_EXACT, n, large), -1).astype(np.int32)


def _const_spec(shape):
    nd = len(shape)
    return pl.BlockSpec(shape, lambda *_: (0,) * nd, pipeline_mode=pl.Buffered(1))


def _layer_spec(shape, layer):
    nd = len(shape)
    return pl.BlockSpec((None,) + tuple(shape), lambda *_: (layer,) + (0,) * nd,
                        pipeline_mode=pl.Buffered(1))


def _rms(x, gain, eps):
    y = x * lax.rsqrt(jnp.mean(x * x, axis=-1, keepdims=True) + eps)
    return y * gain


def _nt_dot(a, b):
    return lax.dot_general(a, b, (((1,), (1,)), ((), ())), preferred_element_type=F32)


def _diff_lambda(lamv_ref, lam_init):
    p = lamv_ref[...]
    a = jnp.sum(p[0:1] * p[1:2], axis=-1, keepdims=True)
    b = jnp.sum(p[2:3] * p[3:4], axis=-1, keepdims=True)
    return jnp.exp(a) - jnp.exp(b) + lam_init


def _bias_from_buckets(idx, rb_ref, head, scale):
    far = rb_ref[REL_BUCKETS - 1, head]
    tbl = jnp.full(idx.shape, NEG_INF, F32)
    for b in range(REL_BUCKETS):
        tbl = jnp.where(idx == b, (rb_ref[b, head] - far) * scale, tbl)
    return tbl


def _inproj_kernel(x_ref, g_ref, w_ref, lng_ref, lnb_ref, wsp_ref, bsp_ref, *out_refs,
                   tm, sample, n_aliased):
    out_refs = out_refs[n_aliased:]
    if sample:
        q_ref, kf_ref, vf_ref, oa_ref, vn_ref = out_refs
    else:
        qt_ref, kf_ref, vf_ref, kb_ref, vt_ref, oa_ref = out_refs

    nsub = tm // TS
    a0 = 3 * Q_COLS

    def norm(t):
        return _rms(x_ref[t * TS:(t + 1) * TS, :], g_ref[...], NORM_EPS).astype(BF16)

    def proj(xn, c0, c1):
        return jnp.dot(xn, w_ref[:, c0:c1], preferred_element_type=F32)

    def qkv(t, xn):
        rows = slice(t * TS, (t + 1) * TS)
        q = proj(xn, 0, Q_COLS)
        k = proj(xn, Q_COLS, 2 * Q_COLS)
        v = proj(xn, 2 * Q_COLS, 3 * Q_COLS)
        for h in range(H_DIFF):
            head_rows = pl.ds(t * TS * H_DIFF + h, TS, stride=H_DIFF)
            for d in range(kf_ref.shape[0]):
                kf_ref[d, head_rows, :] = k[:, h * HEAD_W:(h + 1) * HEAD_W]
                vf_ref[d, head_rows, :] = v[:, h * V_HEAD_DIM:(h + 1) * V_HEAD_DIM]
        if sample:
            q_ref[rows, :] = q * (QK_HEAD_DIM ** -0.5)
        else:
            qt_ref[:, rows] = (q * (QK_HEAD_DIM ** -0.5 * LOG2_E)).T.astype(BF16)
            vt_ref[:, rows] = v.T.astype(BF16)
            kb_ref[rows, :] = k.astype(BF16)

    def a_proj(xn):
        return proj(xn, a0, a0 + A_WIDTH), proj(xn, a0 + A_WIDTH, a0 + 2 * A_WIDTH)

    row = lax.broadcasted_iota(jnp.int32, (CHUNK, CHUNK), 0)
    col = lax.broadcasted_iota(jnp.int32, (CHUNK, CHUNK), 1)
    if sample:
        keep = jnp.logical_and((row >> 3) == (col >> 3), (col & 7) <= (row & 7))
    else:
        keep = col <= row
    w_mix = [jnp.where(keep, wsp_ref[g], 0.0).astype(BF16) for g in range(A_GROUPS)]

    def gate(t, u_raw, v_raw):
        u_a = jax.nn.gelu(u_raw)
        v_a = jax.nn.gelu(v_raw)
        xc = v_a - jnp.mean(v_a, axis=-1, keepdims=True)
        vn = xc * lax.rsqrt(jnp.mean(xc * xc, axis=-1, keepdims=True) + NORM_EPS)
        vn = vn * lng_ref[...] + lnb_ref[...]
        if sample:
            vn_ref[t * TS:(t + 1) * TS, :] = vn
        vnb = vn.astype(BF16)
        for g in range(A_GROUPS):
            c0, c1 = g * A_GROUP_DIM, (g + 1) * A_GROUP_DIM
            for c in range(TS // CHUNK):
                r0, r1 = c * CHUNK, (c + 1) * CHUNK
                s = jnp.dot(w_mix[g], vnb[r0:r1, c0:c1], preferred_element_type=F32) + bsp_ref[g]
                oa_ref[t * TS + r0:t * TS + r1, c0:c1] = (u_a[r0:r1, c0:c1] * s).astype(
                    oa_ref.dtype)

    xns = [norm(t) for t in range(nsub)]
    for t in range(nsub):
        a_cur = a_proj(xns[t])
        qkv(t, xns[t])
        gate(t, *a_cur)


def _inproj(x, gain, w_in, ln_g, ln_b, wsp, bsp, kv_all, *, layer, sample):
    rows = x.shape[0]
    tm = TM_IN
    row_spec = lambda w: pl.BlockSpec((tm, w), lambda i: (i, 0))
    assert kv_all is not None or layer == 0
    n_slabs = DEPTH if kv_all is None else 1
    head_row_spec = pl.BlockSpec((n_slabs, tm * H_DIFF, HEAD_W),
                                 lambda i: (layer // n_slabs, i, 0))
    col_spec = pl.BlockSpec((Q_COLS, tm), lambda i: (0, i))
    in_specs = [
        row_spec(D_MODEL),
        _layer_spec((1, D_MODEL), layer),
        _layer_spec((D_MODEL, IN_COLS), layer),
        _layer_spec((1, A_WIDTH), layer),
        _layer_spec((1, A_WIDTH), layer),
        _layer_spec((A_GROUPS, CHUNK, CHUNK), layer),
        _layer_spec((A_GROUPS, CHUNK, A_GROUP_DIM), layer),
    ]
    sds = jax.ShapeDtypeStruct
    kv_shape = sds((DEPTH, rows * H_DIFF, HEAD_W), F32)
    aliased = () if kv_all is None else tuple(kv_all)
    n_in = len(in_specs)
    in_specs += [pl.BlockSpec(memory_space=pl.ANY)] * len(aliased)
    aliases = {n_in + i: 1 + i for i in range(len(aliased))}
    if sample:
        out_shape = [sds((rows, Q_COLS), F32), kv_shape, kv_shape,
                     sds((rows, A_WIDTH), F32), sds((rows, A_WIDTH), F32)]
        out_specs = [row_spec(Q_COLS), head_row_spec, head_row_spec,
                     row_spec(A_WIDTH), row_spec(A_WIDTH)]
    else:
        out_shape = [sds((Q_COLS, rows), BF16), kv_shape, kv_shape,
                     sds((rows, Q_COLS), BF16), sds((Q_COLS, rows), BF16),
                     sds((rows, A_WIDTH), BF16)]
        out_specs = [col_spec, head_row_spec, head_row_spec,
                     row_spec(Q_COLS), col_spec, row_spec(A_WIDTH)]
    return pl.pallas_call(
        functools.partial(_inproj_kernel, tm=tm, sample=sample, n_aliased=len(aliased)),
        grid=(rows // tm,),
        in_specs=in_specs,
        out_specs=out_specs,
        out_shape=out_shape,
        input_output_aliases=aliases,
        compiler_params=pltpu.CompilerParams(
            dimension_semantics=("arbitrary",), vmem_limit_bytes=V7X_VMEM_LIMIT_BYTES),
        name="inproj_sample" if sample else "inproj_prompt",
    )(x, gain, w_in, ln_g, ln_b, wsp, bsp, *aliased)


def _prompt_attn_step(first, qi, rb_ref, lamv_ref, idx_ref, sg_ref, qt_ref, k_ref, vt_ref, o_ref,
                      bias_scr, qw_scr, m_scr, acc_scr, *, lam_init):
    @pl.when(first)
    def _():
        for h in range(H_DIFF):
            for t in range(2):
                bias_scr[h, t] = _bias_from_buckets(idx_ref[t], rb_ref, h, LOG2_E)

    drow = lax.broadcasted_iota(jnp.int32, (HEAD_W, TQ), 0)
    first_map = drow < QK_HEAD_DIM
    for h in range(H_DIFF):
        qh = qt_ref[h * HEAD_W:(h + 1) * HEAD_W, :]
        zero = jnp.zeros_like(qh)
        qw_scr[h] = jnp.concatenate(
            [jnp.where(first_map, qh, zero), jnp.where(first_map, zero, qh)], axis=1)
    m_scr[...] = jnp.full_like(m_scr, NEG_INF)
    acc_scr[...] = jnp.zeros_like(acc_scr)
    ones_rows = jnp.ones((DENOM_ROWS, TK), BF16)

    def key_tiles(tiles):
        units = [(pl.multiple_of(j * TK, TK), kind, h) for j, kind in tiles for h in range(H_DIFF)]

        def scores(u):
            r0, _, h = u
            kj = k_ref[pl.ds(r0, TK), h * HEAD_W:(h + 1) * HEAD_W]
            return jnp.dot(kj, qw_scr[h], preferred_element_type=F32)

        def softmax(u, st):
            _, kind, h = u
            if kind is not None:
                bias = bias_scr[h, kind]
                st = jnp.concatenate([bias, bias], axis=1) + st
            m_old = m_scr[h]
            m_new = jnp.maximum(m_old, jnp.max(st, axis=0, keepdims=True))
            m_scr[h] = m_new
            return jnp.exp2(m_old - m_new), jnp.exp2(st - m_new).astype(BF16)

        def accumulate(u, alpha, pt):
            r0, _, h = u
            vtj = jnp.concatenate([vt_ref[h * HEAD_W:(h + 1) * HEAD_W, pl.ds(r0, TK)],
                                   ones_rows], axis=0)
            acc_scr[h] = alpha * acc_scr[h] + jnp.dot(vtj, pt, preferred_element_type=F32)

        n = len(units)
        st = {0: scores(units[0]), 1: scores(units[1])}
        p = {}
        for i in range(n):
            p[i] = softmax(units[i], st.pop(i))
            if i + 2 < n:
                st[i + 2] = scores(units[i + 2])
            if i >= 1:
                accumulate(units[i - 1], *p.pop(i - 1))
        accumulate(units[n - 1], *p.pop(n - 1))

    n_far = jnp.maximum(qi - 1, 0)

    def far_pair(jj, carry):
        key_tiles([(2 * jj, None), (2 * jj + 1, None)])
        return carry

    lax.fori_loop(0, n_far // 2, far_pair, 0)

    @pl.when(n_far % 2 == 1)
    def _():
        key_tiles([(n_far - 1, None)])

    @pl.when(qi >= 1)
    def _():
        key_tiles([(qi - 1, 1), (qi, 0)])

    @pl.when(qi == 0)
    def _():
        key_tiles([(qi, 0)])

    lam = _diff_lambda(lamv_ref, lam_init)
    for h in range(H_DIFF):
        acc = acc_scr[h, 0:V_HEAD_DIM, :]
        rl = 1.0 / acc_scr[h, V_HEAD_DIM:V_HEAD_DIM + 1, :]
        ot = acc[:, :TQ] * rl[:, :TQ] - lam * (acc[:, TQ:] * rl[:, TQ:])
        ot = ot * lax.rsqrt(jnp.mean(ot * ot, axis=0, keepdims=True) + SUBLN_EPS)
        ot = ot * sg_ref[...] * (1.0 - lam_init)
        o_ref[:, h * V_HEAD_DIM:(h + 1) * V_HEAD_DIM] = ot.T.astype(o_ref.dtype)


def _sample_page_copies(step, sl, pt_ref, ck_hbm, cv_hbm, kbuf, vbuf, sem, *, layer, n_pages):
    copies = []
    for i in range(SPS):
        for p in range(n_pages):
            page = pt_ref[step * SPS + i, p]
            copies.append(pltpu.make_async_copy(ck_hbm.at[layer, page], kbuf.at[sl, i, p],
                                                sem.at[sl, 0]))
            copies.append(pltpu.make_async_copy(cv_hbm.at[layer, page], vbuf.at[sl, i, p],
                                                sem.at[sl, 1]))
    return copies


def _sample_attn_step(step, first, slot, rb_ref, lamv_ref, idx_ref, sg_ref, q_ref, kn_ref,
                      vn_ref, o_ref, kbuf, vbuf, s_scr, kpad, vpad, bias_scr,
                      *, n_pages, dec_seq, lam_init):
    del step
    rows = 2 * dec_seq
    past = n_pages * PAGE_SIZE

    @pl.when(first)
    def _():
        for h in range(H_DIFF):
            for t in range(2):
                bias_scr[t, h * rows:(h + 1) * rows, :] = _bias_from_buckets(
                    idx_ref[t], rb_ref, h, 1.0)

    lane = lax.broadcasted_iota(jnp.int32, (dec_seq, HEAD_W), 1)
    first_map = lane < QK_HEAD_DIM
    lam = _diff_lambda(lamv_ref, lam_init)

    def one_sequence(i, carry):
        _sample_attn_sequence(i)
        return carry

    def _sample_attn_sequence(i):
        q0 = pl.multiple_of(i * dec_seq, dec_seq)

        def new_rows(ref, h):
            return ref[pl.ds(i * (dec_seq * H_DIFF) + h, dec_seq, stride=H_DIFF), :]

        def page_head(buf, p, h):
            return buf[slot, i, p, pl.ds(h, PAGE_SIZE, stride=H_DIFF), :]

        kpad[...] = jnp.zeros_like(kpad)
        vpad[...] = jnp.zeros_like(vpad)
        for h in range(H_DIFF):
            kpad[h, 0:dec_seq, :] = new_rows(kn_ref, h)
            vpad[h, 0:dec_seq, :] = new_rows(vn_ref, h)

        for h in range(H_DIFF):
            qh = q_ref[pl.ds(q0, dec_seq), h * HEAD_W:(h + 1) * HEAD_W]
            qm = jnp.concatenate([jnp.where(first_map, qh, 0.0), jnp.where(first_map, 0.0, qh)],
                                 axis=0).astype(BF16)
            r0, r1 = h * rows, (h + 1) * rows
            for p in range(0, n_pages, 2):
                k2 = jnp.concatenate([page_head(kbuf, p, h), page_head(kbuf, p + 1, h)],
                                     axis=0).astype(BF16)
                s_scr[r0:r1, p * PAGE_SIZE:(p + 2) * PAGE_SIZE] = _nt_dot(qm, k2)
            s_scr[r0:r1, past - PAGE_SIZE:past] += bias_scr[0, r0:r1, :]
            s_scr[r0:r1, past:] = _nt_dot(qm, kpad[h].astype(BF16)) + bias_scr[1, r0:r1, :]

        s = s_scr[...]
        e = jnp.exp(s - jnp.max(s, axis=-1, keepdims=True))
        pn = e / jnp.sum(e, axis=-1, keepdims=True)
        for h in range(H_DIFF):
            r0 = h * rows
            a = (pn[r0:r0 + dec_seq] - lam * pn[r0 + dec_seq:r0 + rows]).astype(BF16)
            acc = jnp.dot(a[:, past:], vpad[h].astype(BF16), preferred_element_type=F32)
            for p in range(n_pages):
                acc = acc + jnp.dot(a[:, p * PAGE_SIZE:(p + 1) * PAGE_SIZE],
                                    page_head(vbuf, p, h).astype(BF16),
                                    preferred_element_type=F32)
            o_ref[pl.ds(q0, dec_seq), h * V_HEAD_DIM:(h + 1) * V_HEAD_DIM] = (
                _rms(acc, sg_ref[...], SUBLN_EPS) * (1.0 - lam_init))

    lax.fori_loop(0, SPS, one_sequence, 0)


def _attention_kernel(pt_ref, rb_ref, lamv_ref,
                      idx_p_ref, sg_p_ref, qt_ref, k_ref, vt_ref,
                      idx_s_ref, sg_s_ref, q_ref, kn_ref, vn_ref, ck_hbm, cv_hbm,
                      op_ref, os_ref,
                      bias_p, qw_scr, m_scr, acc_scr,
                      kbuf, vbuf, sem, s_scr, kpad, vpad, bias_s,
                      *, layer, n_pages, dec_seq, lam_init):
    qi = pl.program_id(1)
    step = pl.program_id(0) * pl.num_programs(1) + qi
    nsteps = pl.num_programs(0) * pl.num_programs(1)
    first = step == 0
    slot = lax.rem(step, 2)
    copies = functools.partial(_sample_page_copies, pt_ref=pt_ref, ck_hbm=ck_hbm, cv_hbm=cv_hbm,
                               kbuf=kbuf, vbuf=vbuf, sem=sem, layer=layer, n_pages=n_pages)

    @pl.when(first)
    def _():
        for cp in copies(0, 0):
            cp.start()

    @pl.when(step + 1 < nsteps)
    def _():
        for cp in copies(step + 1, 1 - slot):
            cp.start()

    _prompt_attn_step(first, qi, rb_ref, lamv_ref, idx_p_ref, sg_p_ref, qt_ref, k_ref, vt_ref,
                      op_ref, bias_p, qw_scr, m_scr, acc_scr, lam_init=lam_init)

    for cp in copies(step, slot):
        cp.wait()
    _sample_attn_step(step, first, slot, rb_ref, lamv_ref, idx_s_ref, sg_s_ref, q_ref, kn_ref,
                      vn_ref, os_ref, kbuf, vbuf, s_scr, kpad, vpad, bias_s,
                      n_pages=n_pages, dec_seq=dec_seq, lam_init=lam_init)


def _attention(qt, k, vt, sg_p, q_s, kv_s, cache_k, cache_v, page_table, rel_bias, lamv, sg_s,
               *, layer, batch, seq, dec_seq, lam_init):
    nq = seq // TQ
    kk = np.arange(TK)[:, None]
    qq = np.arange(TQ)[None, :]
    idx_p = jnp.asarray(np.stack([_bucket_table(qq - kk), _bucket_table(TQ + qq - kk)]))

    dec_batch, n_pages = page_table.shape
    assert n_pages % 2 == 0 and dec_batch == SPS * batch * nq
    n_pool = cache_k.shape[1]
    page_rows = PAGE_SIZE * H_DIFF
    ck = cache_k.reshape(DEPTH, n_pool, page_rows, HEAD_W)
    cv = cache_v.reshape(DEPTH, n_pool, page_rows, V_HEAD_DIM)
    past_len = n_pages * PAGE_SIZE
    rows = 2 * dec_seq
    i = (np.arange(rows) % dec_seq)[:, None]
    c = np.arange(PAGE_SIZE)[None, :]
    idx_last = _bucket_table(past_len + i - (past_len - PAGE_SIZE + c))
    idx_new = np.where(c < dec_seq, _bucket_table(i - c), -1).astype(np.int32)
    idx_s = jnp.asarray(np.stack([idx_last, idx_new]))
    nkeys = past_len + PAGE_SIZE

    step = lambda b, i: b * nq + i
    const = lambda shape: pl.BlockSpec(shape, lambda b, i, pt: (0,) * len(shape),
                                       pipeline_mode=pl.Buffered(1))
    per_layer = lambda shape: pl.BlockSpec((None,) + shape,
                                           lambda b, i, pt: (layer,) + (0,) * len(shape),
                                           pipeline_mode=pl.Buffered(1))
    srow_spec = pl.BlockSpec((SPS * dec_seq, Q_COLS), lambda b, i, pt: (step(b, i), 0))
    snew_spec = pl.BlockSpec((None, SPS * dec_seq * H_DIFF, HEAD_W),
                             lambda b, i, pt: (layer, step(b, i), 0))
    grid_spec = pltpu.PrefetchScalarGridSpec(
        num_scalar_prefetch=1,
        grid=(batch, nq),
        in_specs=[
            pl.BlockSpec(memory_space=pltpu.SMEM),
            per_layer((4, QK_HEAD_DIM)),
            const((2, TK, TQ)),
            per_layer((V_HEAD_DIM, TQ)),
            pl.BlockSpec((Q_COLS, TQ), lambda b, i, pt: (0, step(b, i))),
            pl.BlockSpec((seq, Q_COLS), lambda b, i, pt: (b, 0)),
            pl.BlockSpec((Q_COLS, seq), lambda b, i, pt: (0, b)),
            const((2, rows, PAGE_SIZE)),
            per_layer((1, V_HEAD_DIM)),
            srow_spec, snew_spec, snew_spec,
            pl.BlockSpec(memory_space=pl.ANY), pl.BlockSpec(memory_space=pl.ANY),
        ],
        out_specs=[pl.BlockSpec((TQ, DIFF_WIDTH), lambda b, i, pt: (step(b, i), 0)), srow_spec],
        scratch_shapes=[
            pltpu.VMEM((H_DIFF, 2, TK, TQ), F32),
            pltpu.VMEM((H_DIFF, HEAD_W, 2 * TQ), BF16),
            pltpu.VMEM((H_DIFF, 1, 2 * TQ), F32),
            pltpu.VMEM((H_DIFF, V_HEAD_DIM + DENOM_ROWS, 2 * TQ), F32),
            pltpu.VMEM((2, SPS, n_pages, page_rows, HEAD_W), F32),
            pltpu.VMEM((2, SPS, n_pages, page_rows, V_HEAD_DIM), F32),
            pltpu.SemaphoreType.DMA((2, 2)),
            pltpu.VMEM((H_DIFF * rows, nkeys), F32),
            pltpu.VMEM((H_DIFF, PAGE_SIZE, HEAD_W), F32),
            pltpu.VMEM((H_DIFF, PAGE_SIZE, V_HEAD_DIM), F32),
            pltpu.VMEM((2, H_DIFF * rows, PAGE_SIZE), F32),
        ],
    )
    return pl.pallas_call(
        functools.partial(_attention_kernel, layer=layer, n_pages=n_pages, dec_seq=dec_seq,
                          lam_init=lam_init),
        grid_spec=grid_spec,
        out_shape=[jax.ShapeDtypeStruct((batch * seq, DIFF_WIDTH), BF16),
                   jax.ShapeDtypeStruct((dec_batch * dec_seq, DIFF_WIDTH), F32)],
        compiler_params=pltpu.CompilerParams(
            dimension_semantics=("arbitrary", "arbitrary"),
            vmem_limit_bytes=V7X_VMEM_LIMIT_BYTES),
        name="attention",
    )(page_table, rel_bias, lamv, idx_p, sg_p, qt, k, vt, idx_s, sg_s, q_s, kv_s[0], kv_s[1],
      ck, cv)


def _mix_and_prenorm(x_ref, ob_ref, oa_ref, wo_ref, g_post_ref, g_pre_ref):
    mix = jnp.dot(ob_ref[...].astype(BF16), wo_ref[0:DIFF_WIDTH, :], preferred_element_type=F32)
    mix = mix + jnp.dot(oa_ref[...].astype(BF16), wo_ref[DIFF_WIDTH:, :],
                        preferred_element_type=F32)
    x1 = x_ref[...] + _rms(mix, g_post_ref[...], NORM_EPS)
    return x1, _rms(x1, g_pre_ref[...], NORM_EPS)


def _conv_gate(cw, cb, h, h1, h2):
    return cb + cw[0:1] * h2 + cw[1:2] * h1 + cw[2:3] * h


def _ffn_prompt_kernel(x_ref, ob_ref, oa_ref, wo_ref, g_post_ref, g_pre_ref, g_out_ref,
                       wu_ref, cw_ref, cb_ref, wd_ref, y_ref, st_ref, perm, carry, act,
                       *, tm):
    nsub = tm // TS
    nv = TS // 8
    nj = D_FF // TF
    ncb = D_MODEL // LANE

    @pl.when(pl.program_id(1) == 0)
    def _():
        carry[...] = jnp.zeros_like(carry)

    def pre(t):
        rows = slice(t * TS, (t + 1) * TS)
        x1, xn2 = _mix_and_prenorm(x_ref.at[rows], ob_ref.at[rows], oa_ref.at[rows], wo_ref,
                                   g_post_ref, g_pre_ref)
        y_ref[rows, :] = x1
        for c in range(ncb):
            for s in range(8):
                perm[t, c, s * PITCH:s * PITCH + nv, :] = xn2[s * nv:(s + 1) * nv,
                                                              c * LANE:(c + 1) * LANE]
        return jnp.concatenate(
            [jnp.concatenate([perm[t, c, pl.ds(v, 8, stride=PITCH), :] for c in range(ncb)],
                             axis=1)
             for v in range(nv)], axis=0).astype(BF16)

    first_row = lax.broadcasted_iota(jnp.int32, (8, TF), 0) == 0

    def conv_ffn(t, xp):
        def up(j):
            return [jnp.dot(xp, wu_ref[:, half * D_FF + j * TF:half * D_FF + (j + 1) * TF],
                            preferred_element_type=F32) for half in range(2)]

        h_next = up(0)
        for j in range(nj):
            h_cur = h_next
            if j + 1 < nj:
                h_next = up(j + 1)
            conv = []
            for half in range(2):
                c0 = half * D_FF + j * TF
                c1 = c0 + TF
                h = h_cur[half]
                prev = carry[:, c0:c1]
                head = [jnp.where(first_row, pltpu.roll(prev[8 * i:8 * i + 8], 1, 0),
                                  pltpu.roll(h[TS - 16 + 8 * i:TS - 8 + 8 * i], 1, 0))
                        for i in range(2)]
                h1 = jnp.concatenate([head[1], h[:TS - 8]], axis=0)
                h2 = jnp.concatenate([head[0], head[1], h[:TS - 16]], axis=0)
                carry[:, c0:c1] = h[TS - 16:]
                conv.append(_conv_gate(cw_ref[:, c0:c1], cb_ref[:, c0:c1], h, h1, h2))
            act[t, :, j * TF:(j + 1) * TF] = (jax.nn.silu(conv[0]) * conv[1]).astype(BF16)

    def post(t):
        rows = slice(t * TS, (t + 1) * TS)
        fp = jnp.dot(act[t], wd_ref[...], preferred_element_type=F32)
        for v in range(nv):
            for c in range(ncb):
                perm[t, c, pl.ds(v, 8, stride=PITCH), :] = fp[8 * v:8 * v + 8,
                                                              c * LANE:(c + 1) * LANE]
        f = jnp.concatenate(
            [jnp.concatenate([perm[t, c, s * PITCH:s * PITCH + nv, :] for s in range(8)], axis=0)
             for c in range(ncb)], axis=1)
        y_ref[rows, :] = y_ref[rows, :] + _rms(f, g_out_ref[...], NORM_EPS)

    xps = [pre(t) for t in range(nsub)]
    for t in range(nsub):
        conv_ffn(t, xps[t])
        post(t)
    st_ref[0, 0:1, :] = carry[7:8, :]
    st_ref[0, 1:2, :] = carry[15:16, :]


def _ffn_prompt(x, o_b, o_a, w_out, g_post, g_pre, g_out, w_up, conv_w, conv_b, w_down,
                *, layer, batch, seq):
    tm = TM_FFN
    nt = seq // tm
    row_spec = lambda w: pl.BlockSpec((tm, w), lambda b, i: (b * nt + i, 0))
    per_layer = functools.partial(_layer_spec, layer=layer)
    return pl.pallas_call(
        functools.partial(_ffn_prompt_kernel, tm=tm),
        grid=(batch, nt),
        in_specs=[
            row_spec(D_MODEL), row_spec(DIFF_WIDTH), row_spec(A_WIDTH),
            per_layer((D_MODEL, D_MODEL)),
            per_layer((1, D_MODEL)), per_layer((1, D_MODEL)), per_layer((1, D_MODEL)),
            per_layer((D_MODEL, 2 * D_FF)),
            per_layer((CONV_W, 2 * D_FF)),
            per_layer((1, 2 * D_FF)),
            per_layer((D_FF, D_MODEL)),
        ],
        out_specs=[row_spec(D_MODEL),
                   pl.BlockSpec((1, CONV_W - 1, 2 * D_FF), lambda b, i: (b, 0, 0))],
        out_shape=[jax.ShapeDtypeStruct((batch * seq, D_MODEL), F32),
                   jax.ShapeDtypeStruct((batch, CONV_W - 1, 2 * D_FF), F32)],
        scratch_shapes=[pltpu.VMEM((tm // TS, D_MODEL // LANE, 8 * PITCH, LANE), F32),
                        pltpu.VMEM((16, 2 * D_FF), F32),
                        pltpu.VMEM((tm // TS, TS, D_FF), BF16)],
        compiler_params=pltpu.CompilerParams(
            dimension_semantics=("arbitrary", "arbitrary"),
            vmem_limit_bytes=V7X_VMEM_LIMIT_BYTES),
        name="ffn_prompt",
    )(x, o_b, o_a, w_out, g_post, g_pre, g_out, w_up, conv_w, conv_b, w_down)


def _ffn_sample_kernel(x_ref, ob_ref, oa_ref, wo_ref, g_post_ref, g_pre_ref, g_out_ref,
                       wug_ref, wuu_ref, cwg_ref, cwu_ref, cbg_ref, cbu_ref, wd_ref,
                       stg_ref, stu_ref, y_ref, sog_ref, sou_ref, x1_scr, xn_scr, acc_scr,
                       *, dec_batch, dec_seq):
    j = pl.program_id(0)
    rows = dec_batch * dec_seq

    @pl.when(j == 0)
    def _():
        x1, xn2 = _mix_and_prenorm(x_ref, ob_ref, oa_ref, wo_ref, g_post_ref, g_pre_ref)
        x1_scr[...] = x1
        xn_scr[...] = xn2.astype(BF16)
        acc_scr[...] = jnp.zeros_like(acc_scr)

    xn2 = xn_scr[...]
    conv = []
    for wu_ref, cw_ref, cb_ref, st_ref, so_ref in ((wug_ref, cwg_ref, cbg_ref, stg_ref, sog_ref),
                                                   (wuu_ref, cwu_ref, cbu_ref, stu_ref, sou_ref)):
        h = jnp.dot(xn2, wu_ref[...], preferred_element_type=F32)
        s0 = st_ref[0]
        s1 = st_ref[1]
        h1 = jnp.concatenate([s1, h[:rows - dec_batch]], axis=0)
        h2 = jnp.concatenate([s0, s1, h[:rows - 2 * dec_batch]], axis=0)
        conv.append(_conv_gate(cw_ref[...], cb_ref[...], h, h1, h2))
        so_ref[0] = h[rows - 2 * dec_batch:rows - dec_batch]
        so_ref[1] = h[rows - dec_batch:]
    act = (jax.nn.silu(conv[0]) * conv[1]).astype(BF16)
    acc_scr[...] += jnp.dot(act, wd_ref[...], preferred_element_type=F32)

    @pl.when(j == pl.num_programs(0) - 1)
    def _():
        y_ref[...] = x1_scr[...] + _rms(acc_scr[...], g_out_ref[...], NORM_EPS)


def _ffn_sample(x, o_b, o_a, w_out, g_post, g_pre, g_out, w_up, conv_w, conv_b, w_down, state,
                *, layer, dec_batch, dec_seq):
    rows = dec_batch * dec_seq
    nj = D_FF // TF
    ns = CONV_W - 1

    def col_tile(shape, ax, first):
        return pl.BlockSpec((None,) + shape,
                            lambda j: (layer,) + tuple(first + j if a == ax else 0
                                                       for a in range(len(shape))))

    gate = functools.partial(col_tile, first=0)
    up = functools.partial(col_tile, first=nj)
    per_layer = functools.partial(_layer_spec, layer=layer)
    state_out = pl.BlockSpec((ns, dec_batch, TF), lambda j: (0, 0, j))
    sds = jax.ShapeDtypeStruct
    y, st_gate, st_up = pl.pallas_call(
        functools.partial(_ffn_sample_kernel, dec_batch=dec_batch, dec_seq=dec_seq),
        grid=(nj,),
        in_specs=[
            _const_spec((rows, D_MODEL)), _const_spec((rows, DIFF_WIDTH)),
            _const_spec((rows, A_WIDTH)),
            per_layer((D_MODEL, D_MODEL)),
            per_layer((1, D_MODEL)), per_layer((1, D_MODEL)), per_layer((1, D_MODEL)),
            gate((D_MODEL, TF), 1), up((D_MODEL, TF), 1),
            gate((CONV_W, TF), 1), up((CONV_W, TF), 1),
            gate((1, TF), 1), up((1, TF), 1),
            gate((TF, D_MODEL), 0),
            gate((ns, dec_batch, TF), 2), up((ns, dec_batch, TF), 2),
        ],
        out_specs=[pl.BlockSpec((rows, D_MODEL), lambda j: (0, 0)), state_out, state_out],
        out_shape=[sds((rows, D_MODEL), F32), sds((ns, dec_batch, D_FF), F32),
                   sds((ns, dec_batch, D_FF), F32)],
        scratch_shapes=[pltpu.VMEM((rows, D_MODEL), F32), pltpu.VMEM((rows, D_MODEL), BF16),
                        pltpu.VMEM((rows, D_MODEL), F32)],
        compiler_params=pltpu.CompilerParams(
            dimension_semantics=("arbitrary",), vmem_limit_bytes=V7X_VMEM_LIMIT_BYTES),
        name="ffn_sample",
    )(x, o_b, o_a, w_out, g_post, g_pre, g_out, w_up, w_up, conv_w, conv_w, conv_b, conv_b,
      w_down, state, state)
    return y, jnp.concatenate([st_gate, st_up], axis=-1)


def kernel(x_prompt, x_sample, cache_k, cache_v, state_conv, page_table, rel_bias, norm_mix_pre, norm_mix_post, norm_ffn_pre, norm_ffn_post, w_in, lambda_q1, lambda_k1, lambda_q2, lambda_k2, subln_gain, sgu_ln_gain, sgu_ln_bias, w_spatial, b_spatial, w_out, w_up, conv_w, conv_b, w_down):
    batch, seq, _ = x_prompt.shape
    dec_batch, dec_seq, _ = x_sample.shape
    assert seq % TQ == 0 and seq % TM_FFN == 0 and (batch * seq) % TM_IN == 0
    assert (dec_batch * dec_seq) % TM_IN == 0 and CHUNK % dec_seq == 0 and TQ == TK

    def to_pm(a):
        return a.reshape(dec_batch, dec_seq, -1).transpose(1, 0, 2).reshape(dec_batch * dec_seq, -1)

    def to_bm(a):
        return a.reshape(dec_seq, dec_batch, -1).transpose(1, 0, 2).reshape(dec_batch * dec_seq, -1)

    y_p = x_prompt.reshape(batch * seq, D_MODEL)
    y_s = x_sample.reshape(dec_batch * dec_seq, D_MODEL)
    reps = CHUNK // dec_seq

    rows3 = lambda a: a.reshape(DEPTH, 1, -1)
    w_in_b, w_out_b = w_in.astype(BF16), w_out.astype(BF16)
    w_up_b, w_down_b = w_up.astype(BF16), w_down.astype(BF16)
    lamv = jnp.stack([lambda_q1, lambda_k1, lambda_q2, lambda_k2], axis=1)
    sg_rows = jnp.broadcast_to(subln_gain[:, :, None], (DEPTH, V_HEAD_DIM, TQ))
    ffn_args = (w_out_b, rows3(norm_mix_post), rows3(norm_ffn_pre), rows3(norm_ffn_post),
                w_up_b, conv_w, rows3(conv_b), w_down_b)
    in_args = (rows3(norm_mix_pre), w_in_b, rows3(sgu_ln_gain), rows3(sgu_ln_bias))
    gate_shape = (DEPTH, A_GROUPS, CHUNK, A_GROUP_DIM)
    bsp_p = jnp.broadcast_to(b_spatial[:, :, :, None], gate_shape)
    wsp_s = jnp.tile(w_spatial[:, :, :dec_seq, :dec_seq], (1, 1, reps, reps))
    bsp_s = jnp.broadcast_to(jnp.tile(b_spatial[:, :, :dec_seq], (1, 1, reps))[:, :, :, None],
                             gate_shape)
    state = state_conv.transpose(0, 2, 1, 3)

    cp, csm, svs = [], [], []
    kv_p = kv_s = None
    for l in range(DEPTH):
        lam_init = _lambda_init(l)

        qt, kf, vf, kb, vt, o_a = _inproj(y_p, *in_args, w_spatial, bsp_p, kv_p,
                                          layer=l, sample=False)
        kv_p = (kf, vf)
        q_s, kf, vf, o_a_s, vn = _inproj(y_s, *in_args, wsp_s, bsp_s, kv_s, layer=l, sample=True)
        kv_s = (kf, vf)
        o_b, o_b_s = _attention(qt, kb, vt, sg_rows, q_s, kv_s, cache_k, cache_v, page_table,
                                rel_bias, lamv, rows3(subln_gain), layer=l, batch=batch, seq=seq,
                                dec_seq=dec_seq, lam_init=lam_init)
        y_p, conv_p = _ffn_prompt(y_p, o_b, o_a, *ffn_args, layer=l, batch=batch, seq=seq)
        cp.append(conv_p)
        y_pm, conv_s = _ffn_sample(to_pm(y_s), to_pm(o_b_s), to_pm(o_a_s), *ffn_args, state,
                                   layer=l, dec_batch=dec_batch, dec_seq=dec_seq)
        y_s = to_bm(y_pm)
        csm.append(conv_s.transpose(1, 0, 2))
        svs.append(vn.reshape(dec_batch, dec_seq, A_WIDTH))

    kv_shape_p = (DEPTH, batch, seq, H_DIFF, HEAD_W)
    kv_shape_s = (DEPTH, dec_batch, dec_seq, H_DIFF, HEAD_W)
    return (y_p.reshape(batch, seq, D_MODEL), y_s.reshape(dec_batch, dec_seq, D_MODEL),
            kv_p[0].reshape(kv_shape_p), kv_p[1].reshape(kv_shape_p), jnp.stack(cp),
            kv_s[0].reshape(kv_shape_s), kv_s[1].reshape(kv_shape_s),
            jnp.stack(csm), jnp.stack(svs))
```

```python
import functools
import math

import jax
import jax.numpy as jnp
import numpy as np
from jax import lax
from jax.experimental import pallas as pl
from jax.experimental.pallas import tpu as pltpu

F32 = jnp.float32
BF16 = jnp.bfloat16

D_MODEL = 1024
DEPTH = 2
PAGE_SIZE = 128
H_DIFF = 4
QK_HEAD_DIM = 64
V_HEAD_DIM = 128
HEAD_W = 2 * QK_HEAD_DIM
DIFF_WIDTH = H_DIFF * V_HEAD_DIM
A_GROUPS = 4
A_WIDTH = 512
A_GROUP_DIM = 128
CHUNK = 128
LANE = 128
Q_COLS = H_DIFF * HEAD_W
IN_COLS = 3 * Q_COLS + 2 * A_WIDTH
D_FF = 2816
CONV_W = 3
REL_BUCKETS = 32
REL_MAX_EXACT = 16
REL_MAX_DIST = 128
NORM_EPS = 1e-6
SUBLN_EPS = 1e-5
NEG_INF = -1e30
LOG2_E = math.log2(math.e)

V7X_VMEM_LIMIT_BYTES = 56 * 1024 * 1024

TM_IN = 1024
TQ = 256
TK = 256
DENOM_ROWS = 16
SPS = 2
TM_FFN = 512
TS = 256
PITCH = TS // 8 + 8
TF = 256


def _lambda_init(layer):
    return 0.8 - 0.6 * math.exp(-0.3 * layer)


def _bucket_table(dist):
    n = np.maximum(dist, 0)
    nf = np.maximum(n, 1).astype(np.float32)
    large = REL_MAX_EXACT + (np.log(nf / REL_MAX_EXACT) / math.log(REL_MAX_DIST / REL_MAX_EXACT)
                             * (REL_BUCKETS - REL_MAX_EXACT)).astype(np.int32)
    large = np.minimum(large, REL_BUCKETS - 1)
    return np.where(dist >= 0, np.where(n < REL_MAX_EXACT, n, large), -1).astype(np.int32)


def _const_spec(shape):
    nd = len(shape)
    return pl.BlockSpec(shape, lambda *_: (0,) * nd, pipeline_mode=pl.Buffered(1))


def _layer_spec(shape, layer):
    nd = len(shape)
    return pl.BlockSpec((None,) + tuple(shape), lambda *_: (layer,) + (0,) * nd,
                        pipeline_mode=pl.Buffered(1))


def _rms(x, gain, eps):
    y = x * lax.rsqrt(jnp.mean(x * x, axis=-1, keepdims=True) + eps)
    return y * gain


def _nt_dot(a, b):
    return lax.dot_general(a, b, (((1,), (1,)), ((), ())), preferred_element_type=F32)


def _diff_lambda(lamv_ref, lam_init):
    p = lamv_ref[...]
    a = jnp.sum(p[0:1] * p[1:2], axis=-1, keepdims=True)
    b = jnp.sum(p[2:3] * p[3:4], axis=-1, keepdims=True)
    return jnp.exp(a) - jnp.exp(b) + lam_init


def _bias_from_buckets(idx, rb_ref, head, scale):
    far = rb_ref[REL_BUCKETS - 1, head]
    tbl = jnp.full(idx.shape, NEG_INF, F32)
    for b in range(REL_BUCKETS):
        tbl = jnp.where(idx == b, (rb_ref[b, head] - far) * scale, tbl)
    return tbl


def _inproj_kernel(x_ref, g_ref, w_ref, lng_ref, lnb_ref, wsp_ref, bsp_ref, *out_refs,
                   tm, sample, n_aliased):
    out_refs = out_refs[n_aliased:]
    if sample:
        q_ref, kf_ref, vf_ref, oa_ref, vn_ref = out_refs
    else:
        qt_ref, kf_ref, vf_ref, kb_ref, vt_ref, oa_ref = out_refs

    nsub = tm // TS
    a0 = 3 * Q_COLS

    def norm(t):
        return _rms(x_ref[t * TS:(t + 1) * TS, :], g_ref[...], NORM_EPS).astype(BF16)

    def proj(xn, c0, c1):
        return jnp.dot(xn, w_ref[:, c0:c1], preferred_element_type=F32)

    def qkv(t, xn):
        rows = slice(t * TS, (t + 1) * TS)
        q = proj(xn, 0, Q_COLS)
        k = proj(xn, Q_COLS, 2 * Q_COLS)
        v = proj(xn, 2 * Q_COLS, 3 * Q_COLS)
        for h in range(H_DIFF):
            head_rows = pl.ds(t * TS * H_DIFF + h, TS, stride=H_DIFF)
            for d in range(kf_ref.shape[0]):
                kf_ref[d, head_rows, :] = k[:, h * HEAD_W:(h + 1) * HEAD_W]
                vf_ref[d, head_rows, :] = v[:, h * V_HEAD_DIM:(h + 1) * V_HEAD_DIM]
        if sample:
            q_ref[rows, :] = q * (QK_HEAD_DIM ** -0.5)
        else:
            qt_ref[:, rows] = (q * (QK_HEAD_DIM ** -0.5 * LOG2_E)).T.astype(BF16)
            vt_ref[:, rows] = v.T.astype(BF16)
            kb_ref[rows, :] = k.astype(BF16)

    def a_proj(xn):
        return proj(xn, a0, a0 + A_WIDTH), proj(xn, a0 + A_WIDTH, a0 + 2 * A_WIDTH)

    row = lax.broadcasted_iota(jnp.int32, (CHUNK, CHUNK), 0)
    col = lax.broadcasted_iota(jnp.int32, (CHUNK, CHUNK), 1)
    if sample:
        keep = jnp.logical_and((row >> 3) == (col >> 3), (col & 7) <= (row & 7))
    else:
        keep = col <= row
    w_mix = [jnp.where(keep, wsp_ref[g], 0.0).astype(BF16) for g in range(A_GROUPS)]

    def gate(t, u_raw, v_raw):
        u_a = jax.nn.gelu(u_raw)
        v_a = jax.nn.gelu(v_raw)
        xc = v_a - jnp.mean(v_a, axis=-1, keepdims=True)
        vn = xc * lax.rsqrt(jnp.mean(xc * xc, axis=-1, keepdims=True) + NORM_EPS)
        vn = vn * lng_ref[...] + lnb_ref[...]
        if sample:
            vn_ref[t * TS:(t + 1) * TS, :] = vn
        vnb = vn.astype(BF16)
        for g in range(A_GROUPS):
            c0, c1 = g * A_GROUP_DIM, (g + 1) * A_GROUP_DIM
            for c in range(TS // CHUNK):
                r0, r1 = c * CHUNK, (c + 1) * CHUNK
                s = jnp.dot(w_mix[g], vnb[r0:r1, c0:c1], preferred_element_type=F32) + bsp_ref[g]
                oa_ref[t * TS + r0:t * TS + r1, c0:c1] = (u_a[r0:r1, c0:c1] * s).astype(
                    oa_ref.dtype)

    xns = [norm(t) for t in range(nsub)]
    for t in range(nsub):
        a_cur = a_proj(xns[t])
        qkv(t, xns[t])
        gate(t, *a_cur)


def _inproj(x, gain, w_in, ln_g, ln_b, wsp, bsp, kv_all, *, layer, sample):
    rows = x.shape[0]
    tm = TM_IN
    row_spec = lambda w: pl.BlockSpec((tm, w), lambda i: (i, 0))
    assert kv_all is not None or layer == 0
    n_slabs = DEPTH if kv_all is None else 1
    head_row_spec = pl.BlockSpec((n_slabs, tm * H_DIFF, HEAD_W),
                                 lambda i: (layer // n_slabs, i, 0))
    col_spec = pl.BlockSpec((Q_COLS, tm), lambda i: (0, i))
    in_specs = [
        row_spec(D_MODEL),
        _layer_spec((1, D_MODEL), layer),
        _layer_spec((D_MODEL, IN_COLS), layer),
        _layer_spec((1, A_WIDTH), layer),
        _layer_spec((1, A_WIDTH), layer),
        _layer_spec((A_GROUPS, CHUNK, CHUNK), layer),
        _layer_spec((A_GROUPS, CHUNK, A_GROUP_DIM), layer),
    ]
    sds = jax.ShapeDtypeStruct
    kv_shape = sds((DEPTH, rows * H_DIFF, HEAD_W), F32)
    aliased = () if kv_all is None else tuple(kv_all)
    n_in = len(in_specs)
    in_specs += [pl.BlockSpec(memory_space=pl.ANY)] * len(aliased)
    aliases = {n_in + i: 1 + i for i in range(len(aliased))}
    if sample:
        out_shape = [sds((rows, Q_COLS), F32), kv_shape, kv_shape,
                     sds((rows, A_WIDTH), F32), sds((rows, A_WIDTH), F32)]
        out_specs = [row_spec(Q_COLS), head_row_spec, head_row_spec,
                     row_spec(A_WIDTH), row_spec(A_WIDTH)]
    else:
        out_shape = [sds((Q_COLS, rows), BF16), kv_shape, kv_shape,
                     sds((rows, Q_COLS), BF16), sds((Q_COLS, rows), BF16),
                     sds((rows, A_WIDTH), BF16)]
        out_specs = [col_spec, head_row_spec, head_row_spec,
                     row_spec(Q_COLS), col_spec, row_spec(A_WIDTH)]
    return pl.pallas_call(
        functools.partial(_inproj_kernel, tm=tm, sample=sample, n_aliased=len(aliased)),
        grid=(rows // tm,),
        in_specs=in_specs,
        out_specs=out_specs,
        out_shape=out_shape,
        input_output_aliases=aliases,
        compiler_params=pltpu.CompilerParams(
            dimension_semantics=("arbitrary",), vmem_limit_bytes=V7X_VMEM_LIMIT_BYTES),
        name="inproj_sample" if sample else "inproj_prompt",
    )(x, gain, w_in, ln_g, ln_b, wsp, bsp, *aliased)


def _prompt_attn_step(first, qi, rb_ref, lamv_ref, idx_ref, sg_ref, qt_ref, k_ref, vt_ref, o_ref,
                      bias_scr, qw_scr, m_scr, acc_scr, *, lam_init):
    @pl.when(first)
    def _():
        for h in range(H_DIFF):
            for t in range(2):
                bias_scr[h, t] = _bias_from_buckets(idx_ref[t], rb_ref, h, LOG2_E)

    drow = lax.broadcasted_iota(jnp.int32, (HEAD_W, TQ), 0)
    first_map = drow < QK_HEAD_DIM
    for h in range(H_DIFF):
        qh = qt_ref[h * HEAD_W:(h + 1) * HEAD_W, :]
        zero = jnp.zeros_like(qh)
        qw_scr[h] = jnp.concatenate(
            [jnp.where(first_map, qh, zero), jnp.where(first_map, zero, qh)], axis=1)
    m_scr[...] = jnp.full_like(m_scr, NEG_INF)
    acc_scr[...] = jnp.zeros_like(acc_scr)
    ones_rows = jnp.ones((DENOM_ROWS, TK), BF16)

    def key_tiles(tiles):
        units = [(pl.multiple_of(j * TK, TK), kind, h) for j, kind in tiles for h in range(H_DIFF)]

        def scores(u):
            r0, _, h = u
            kj = k_ref[pl.ds(r0, TK), h * HEAD_W:(h + 1) * HEAD_W]
            return jnp.dot(kj, qw_scr[h], preferred_element_type=F32)

        def softmax(u, st):
            _, kind, h = u
            if kind is not None:
                bias = bias_scr[h, kind]
                st = jnp.concatenate([bias, bias], axis=1) + st
            m_old = m_scr[h]
            m_new = jnp.maximum(m_old, jnp.max(st, axis=0, keepdims=True))
            m_scr[h] = m_new
            return jnp.exp2(m_old - m_new), jnp.exp2(st - m_new).astype(BF16)

        def accumulate(u, alpha, pt):
            r0, _, h = u
            vtj = jnp.concatenate([vt_ref[h * HEAD_W:(h + 1) * HEAD_W, pl.ds(r0, TK)],
                                   ones_rows], axis=0)
            acc_scr[h] = alpha * acc_scr[h] + jnp.dot(vtj, pt, preferred_element_type=F32)

        n = len(units)
        st = {0: scores(units[0]), 1: scores(units[1])}
        p = {}
        for i in range(n):
            p[i] = softmax(units[i], st.pop(i))
            if i + 2 < n:
                st[i + 2] = scores(units[i + 2])
            if i >= 1:
                accumulate(units[i - 1], *p.pop(i - 1))
        accumulate(units[n - 1], *p.pop(n - 1))

    n_far = jnp.maximum(qi - 1, 0)

    def far_pair(jj, carry):
        key_tiles([(2 * jj, None), (2 * jj + 1, None)])
        return carry

    lax.fori_loop(0, n_far // 2, far_pair, 0)

    @pl.when(n_far % 2 == 1)
    def _():
        key_tiles([(n_far - 1, None)])

    @pl.when(qi >= 1)
    def _():
        key_tiles([(qi - 1, 1), (qi, 0)])

    @pl.when(qi == 0)
    def _():
        key_tiles([(qi, 0)])

    lam = _diff_lambda(lamv_ref, lam_init)
    for h in range(H_DIFF):
        acc = acc_scr[h, 0:V_HEAD_DIM, :]
        rl = 1.0 / acc_scr[h, V_HEAD_DIM:V_HEAD_DIM + 1, :]
        ot = acc[:, :TQ] * rl[:, :TQ] - lam * (acc[:, TQ:] * rl[:, TQ:])
        ot = ot * lax.rsqrt(jnp.mean(ot * ot, axis=0, keepdims=True) + SUBLN_EPS)
        ot = ot * sg_ref[...] * (1.0 - lam_init)
        o_ref[:, h * V_HEAD_DIM:(h + 1) * V_HEAD_DIM] = ot.T.astype(o_ref.dtype)


def _sample_page_copies(step, sl, pt_ref, ck_hbm, cv_hbm, kbuf, vbuf, sem, *, layer, n_pages):
    copies = []
    for i in range(SPS):
        for p in range(n_pages):
            page = pt_ref[step * SPS + i, p]
            copies.append(pltpu.make_async_copy(ck_hbm.at[layer, page], kbuf.at[sl, i, p],
                                                sem.at[sl, 0]))
            copies.append(pltpu.make_async_copy(cv_hbm.at[layer, page], vbuf.at[sl, i, p],
                                                sem.at[sl, 1]))
    return copies


def _sample_attn_step(step, first, slot, rb_ref, lamv_ref, idx_ref, sg_ref, q_ref, kn_ref,
                      vn_ref, o_ref, kbuf, vbuf, s_scr, kpad, vpad, bias_scr,
                      *, n_pages, dec_seq, lam_init):
    del step
    rows = 2 * dec_seq
    past = n_pages * PAGE_SIZE

    @pl.when(first)
    def _():
        for h in range(H_DIFF):
            for t in range(2):
                bias_scr[t, h * rows:(h + 1) * rows, :] = _bias_from_buckets(
                    idx_ref[t], rb_ref, h, 1.0)

    lane = lax.broadcasted_iota(jnp.int32, (dec_seq, HEAD_W), 1)
    first_map = lane < QK_HEAD_DIM
    lam = _diff_lambda(lamv_ref, lam_init)

    def one_sequence(i, carry):
        _sample_attn_sequence(i)
        return carry

    def _sample_attn_sequence(i):
        q0 = pl.multiple_of(i * dec_seq, dec_seq)

        def new_rows(ref, h):
            return ref[pl.ds(i * (dec_seq * H_DIFF) + h, dec_seq, stride=H_DIFF), :]

        def page_head(buf, p, h):
            return buf[slot, i, p, pl.ds(h, PAGE_SIZE, stride=H_DIFF), :]

        kpad[...] = jnp.zeros_like(kpad)
        vpad[...] = jnp.zeros_like(vpad)
        for h in range(H_DIFF):
            kpad[h, 0:dec_seq, :] = new_rows(kn_ref, h)
            vpad[h, 0:dec_seq, :] = new_rows(vn_ref, h)

        for h in range(H_DIFF):
            qh = q_ref[pl.ds(q0, dec_seq), h * HEAD_W:(h + 1) * HEAD_W]
            qm = jnp.concatenate([jnp.where(first_map, qh, 0.0), jnp.where(first_map, 0.0, qh)],
                                 axis=0).astype(BF16)
            r0, r1 = h * rows, (h + 1) * rows
            for p in range(0, n_pages, 2):
                k2 = jnp.concatenate([page_head(kbuf, p, h), page_head(kbuf, p + 1, h)],
                                     axis=0).astype(BF16)
                s_scr[r0:r1, p * PAGE_SIZE:(p + 2) * PAGE_SIZE] = _nt_dot(qm, k2)
            s_scr[r0:r1, past - PAGE_SIZE:past] += bias_scr[0, r0:r1, :]
            s_scr[r0:r1, past:] = _nt_dot(qm, kpad[h].astype(BF16)) + bias_scr[1, r0:r1, :]

        s = s_scr[...]
        e = jnp.exp(s - jnp.max(s, axis=-1, keepdims=True))
        pn = e / jnp.sum(e, axis=-1, keepdims=True)
        for h in range(H_DIFF):
            r0 = h * rows
            a = (pn[r0:r0 + dec_seq] - lam * pn[r0 + dec_seq:r0 + rows]).astype(BF16)
            acc = jnp.dot(a[:, past:], vpad[h].astype(BF16), preferred_element_type=F32)
            for p in range(n_pages):
                acc = acc + jnp.dot(a[:, p * PAGE_SIZE:(p + 1) * PAGE_SIZE],
                                    page_head(vbuf, p, h).astype(BF16),
                                    preferred_element_type=F32)
            o_ref[pl.ds(q0, dec_seq), h * V_HEAD_DIM:(h + 1) * V_HEAD_DIM] = (
                _rms(acc, sg_ref[...], SUBLN_EPS) * (1.0 - lam_init))

    lax.fori_loop(0, SPS, one_sequence, 0)


def _attention_kernel(pt_ref, rb_ref, lamv_ref,
                      idx_p_ref, sg_p_ref, qt_ref, k_ref, vt_ref,
                      idx_s_ref, sg_s_ref, q_ref, kn_ref, vn_ref, ck_hbm, cv_hbm,
                      op_ref, os_ref,
                      bias_p, qw_scr, m_scr, acc_scr,
                      kbuf, vbuf, sem, s_scr, kpad, vpad, bias_s,
                      *, layer, n_pages, dec_seq, lam_init):
    qi = pl.program_id(1)
    step = pl.program_id(0) * pl.num_programs(1) + qi
    nsteps = pl.num_programs(0) * pl.num_programs(1)
    first = step == 0
    slot = lax.rem(step, 2)
    copies = functools.partial(_sample_page_copies, pt_ref=pt_ref, ck_hbm=ck_hbm, cv_hbm=cv_hbm,
                               kbuf=kbuf, vbuf=vbuf, sem=sem, layer=layer, n_pages=n_pages)

    @pl.when(first)
    def _():
        for cp in copies(0, 0):
            cp.start()

    @pl.when(step + 1 < nsteps)
    def _():
        for cp in copies(step + 1, 1 - slot):
            cp.start()

    _prompt_attn_step(first, qi, rb_ref, lamv_ref, idx_p_ref, sg_p_ref, qt_ref, k_ref, vt_ref,
                      op_ref, bias_p, qw_scr, m_scr, acc_scr, lam_init=lam_init)

    for cp in copies(step, slot):
        cp.wait()
    _sample_attn_step(step, first, slot, rb_ref, lamv_ref, idx_s_ref, sg_s_ref, q_ref, kn_ref,
                      vn_ref, os_ref, kbuf, vbuf, s_scr, kpad, vpad, bias_s,
                      n_pages=n_pages, dec_seq=dec_seq, lam_init=lam_init)


def _attention(qt, k, vt, sg_p, q_s, kv_s, cache_k, cache_v, page_table, rel_bias, lamv, sg_s,
               *, layer, batch, seq, dec_seq, lam_init):
    nq = seq // TQ
    kk = np.arange(TK)[:, None]
    qq = np.arange(TQ)[None, :]
    idx_p = jnp.asarray(np.stack([_bucket_table(qq - kk), _bucket_table(TQ + qq - kk)]))

    dec_batch, n_pages = page_table.shape
    assert n_pages % 2 == 0 and dec_batch == SPS * batch * nq
    n_pool = cache_k.shape[1]
    page_rows = PAGE_SIZE * H_DIFF
    ck = cache_k.reshape(DEPTH, n_pool, page_rows, HEAD_W)
    cv = cache_v.reshape(DEPTH, n_pool, page_rows, V_HEAD_DIM)
    past_len = n_pages * PAGE_SIZE
    rows = 2 * dec_seq
    i = (np.arange(rows) % dec_seq)[:, None]
    c = np.arange(PAGE_SIZE)[None, :]
    idx_last = _bucket_table(past_len + i - (past_len - PAGE_SIZE + c))
    idx_new = np.where(c < dec_seq, _bucket_table(i - c), -1).astype(np.int32)
    idx_s = jnp.asarray(np.stack([idx_last, idx_new]))
    nkeys = past_len + PAGE_SIZE

    step = lambda b, i: b * nq + i
    const = lambda shape: pl.BlockSpec(shape, lambda b, i, pt: (0,) * len(shape),
                                       pipeline_mode=pl.Buffered(1))
    per_layer = lambda shape: pl.BlockSpec((None,) + shape,
                                           lambda b, i, pt: (layer,) + (0,) * len(shape),
                                           pipeline_mode=pl.Buffered(1))
    srow_spec = pl.BlockSpec((SPS * dec_seq, Q_COLS), lambda b, i, pt: (step(b, i), 0))
    snew_spec = pl.BlockSpec((None, SPS * dec_seq * H_DIFF, HEAD_W),
                             lambda b, i, pt: (layer, step(b, i), 0))
    grid_spec = pltpu.PrefetchScalarGridSpec(
        num_scalar_prefetch=1,
        grid=(batch, nq),
        in_specs=[
            pl.BlockSpec(memory_space=pltpu.SMEM),
            per_layer((4, QK_HEAD_DIM)),
            const((2, TK, TQ)),
            per_layer((V_HEAD_DIM, TQ)),
            pl.BlockSpec((Q_COLS, TQ), lambda b, i, pt: (0, step(b, i))),
            pl.BlockSpec((seq, Q_COLS), lambda b, i, pt: (b, 0)),
            pl.BlockSpec((Q_COLS, seq), lambda b, i, pt: (0, b)),
            const((2, rows, PAGE_SIZE)),
            per_layer((1, V_HEAD_DIM)),
            srow_spec, snew_spec, snew_spec,
            pl.BlockSpec(memory_space=pl.ANY), pl.BlockSpec(memory_space=pl.ANY),
        ],
        out_specs=[pl.BlockSpec((TQ, DIFF_WIDTH), lambda b, i, pt: (step(b, i), 0)), srow_spec],
        scratch_shapes=[
            pltpu.VMEM((H_DIFF, 2, TK, TQ), F32),
            pltpu.VMEM((H_DIFF, HEAD_W, 2 * TQ), BF16),
            pltpu.VMEM((H_DIFF, 1, 2 * TQ), F32),
            pltpu.VMEM((H_DIFF, V_HEAD_DIM + DENOM_ROWS, 2 * TQ), F32),
            pltpu.VMEM((2, SPS, n_pages, page_rows, HEAD_W), F32),
            pltpu.VMEM((2, SPS, n_pages, page_rows, V_HEAD_DIM), F32),
            pltpu.SemaphoreType.DMA((2, 2)),
            pltpu.VMEM((H_DIFF * rows, nkeys), F32),
            pltpu.VMEM((H_DIFF, PAGE_SIZE, HEAD_W), F32),
            pltpu.VMEM((H_DIFF, PAGE_SIZE, V_HEAD_DIM), F32),
            pltpu.VMEM((2, H_DIFF * rows, PAGE_SIZE), F32),
        ],
    )
    return pl.pallas_call(
        functools.partial(_attention_kernel, layer=layer, n_pages=n_pages, dec_seq=dec_seq,
                          lam_init=lam_init),
        grid_spec=grid_spec,
        out_shape=[jax.ShapeDtypeStruct((batch * seq, DIFF_WIDTH), BF16),
                   jax.ShapeDtypeStruct((dec_batch * dec_seq, DIFF_WIDTH), F32)],
        compiler_params=pltpu.CompilerParams(
            dimension_semantics=("arbitrary", "arbitrary"),
            vmem_limit_bytes=V7X_VMEM_LIMIT_BYTES),
        name="attention",
    )(page_table, rel_bias, lamv, idx_p, sg_p, qt, k, vt, idx_s, sg_s, q_s, kv_s[0], kv_s[1],
      ck, cv)


def _mix_and_prenorm(x_ref, ob_ref, oa_ref, wo_ref, g_post_ref, g_pre_ref):
    mix = jnp.dot(ob_ref[...].astype(BF16), wo_ref[0:DIFF_WIDTH, :], preferred_element_type=F32)
    mix = mix + jnp.dot(oa_ref[...].astype(BF16), wo_ref[DIFF_WIDTH:, :],
                        preferred_element_type=F32)
    x1 = x_ref[...] + _rms(mix, g_post_ref[...], NORM_EPS)
    return x1, _rms(x1, g_pre_ref[...], NORM_EPS)


def _conv_gate(cw, cb, h, h1, h2):
    return cb + cw[0:1] * h2 + cw[1:2] * h1 + cw[2:3] * h


def _ffn_prompt_kernel(x_ref, ob_ref, oa_ref, wo_ref, g_post_ref, g_pre_ref, g_out_ref,
                       wu_ref, cw_ref, cb_ref, wd_ref, y_ref, st_ref, perm, carry, act,
                       *, tm):
    nsub = tm // TS
    nv = TS // 8
    nj = D_FF // TF
    ncb = D_MODEL // LANE

    @pl.when(pl.program_id(1) == 0)
    def _():
        carry[...] = jnp.zeros_like(carry)

    def pre(t):
        rows = slice(t * TS, (t + 1) * TS)
        x1, xn2 = _mix_and_prenorm(x_ref.at[rows], ob_ref.at[rows], oa_ref.at[rows], wo_ref,
                                   g_post_ref, g_pre_ref)
        y_ref[rows, :] = x1
        for c in range(ncb):
            for s in range(8):
                perm[t, c, s * PITCH:s * PITCH + nv, :] = xn2[s * nv:(s + 1) * nv,
                                                              c * LANE:(c + 1) * LANE]
        return jnp.concatenate(
            [jnp.concatenate([perm[t, c, pl.ds(v, 8, stride=PITCH), :] for c in range(ncb)],
                             axis=1)
             for v in range(nv)], axis=0).astype(BF16)

    first_row = lax.broadcasted_iota(jnp.int32, (8, TF), 0) == 0

    def conv_ffn(t, xp):
        def up(j):
            return [jnp.dot(xp, wu_ref[:, half * D_FF + j * TF:half * D_FF + (j + 1) * TF],
                            preferred_element_type=F32) for half in range(2)]

        h_next = up(0)
        for j in range(nj):
            h_cur = h_next
            if j + 1 < nj:
                h_next = up(j + 1)
            conv = []
            for half in range(2):
                c0 = half * D_FF + j * TF
                c1 = c0 + TF
                h = h_cur[half]
                prev = carry[:, c0:c1]
                head = [jnp.where(first_row, pltpu.roll(prev[8 * i:8 * i + 8], 1, 0),
                                  pltpu.roll(h[TS - 16 + 8 * i:TS - 8 + 8 * i], 1, 0))
                        for i in range(2)]
                h1 = jnp.concatenate([head[1], h[:TS - 8]], axis=0)
                h2 = jnp.concatenate([head[0], head[1], h[:TS - 16]], axis=0)
                carry[:, c0:c1] = h[TS - 16:]
                conv.append(_conv_gate(cw_ref[:, c0:c1], cb_ref[:, c0:c1], h, h1, h2))
            act[t, :, j * TF:(j + 1) * TF] = (jax.nn.silu(conv[0]) * conv[1]).astype(BF16)

    def post(t):
        rows = slice(t * TS, (t + 1) * TS)
        fp = jnp.dot(act[t], wd_ref[...], preferred_element_type=F32)
        for v in range(nv):
            for c in range(ncb):
                perm[t, c, pl.ds(v, 8, stride=PITCH), :] = fp[8 * v:8 * v + 8,
                                                              c * LANE:(c + 1) * LANE]
        f = jnp.concatenate(
            [jnp.concatenate([perm[t, c, s * PITCH:s * PITCH + nv, :] for s in range(8)], axis=0)
             for c in range(ncb)], axis=1)
        y_ref[rows, :] = y_ref[rows, :] + _rms(f, g_out_ref[...], NORM_EPS)

    xps = [pre(t) for t in range(nsub)]
    for t in range(nsub):
        conv_ffn(t, xps[t])
        post(t)
    st_ref[0, 0:1, :] = carry[7:8, :]
    st_ref[0, 1:2, :] = carry[15:16, :]


def _ffn_prompt(x, o_b, o_a, w_out, g_post, g_pre, g_out, w_up, conv_w, conv_b, w_down,
                *, layer, batch, seq):
    tm = TM_FFN
    nt = seq // tm
    row_spec = lambda w: pl.BlockSpec((tm, w), lambda b, i: (b * nt + i, 0))
    per_layer = functools.partial(_layer_spec, layer=layer)
    return pl.pallas_call(
        functools.partial(_ffn_prompt_kernel, tm=tm),
        grid=(batch, nt),
        in_specs=[
            row_spec(D_MODEL), row_spec(DIFF_WIDTH), row_spec(A_WIDTH),
            per_layer((D_MODEL, D_MODEL)),
            per_layer((1, D_MODEL)), per_layer((1, D_MODEL)), per_layer((1, D_MODEL)),
            per_layer((D_MODEL, 2 * D_FF)),
            per_layer((CONV_W, 2 * D_FF)),
            per_layer((1, 2 * D_FF)),
            per_layer((D_FF, D_MODEL)),
        ],
        out_specs=[row_spec(D_MODEL),
                   pl.BlockSpec((1, CONV_W - 1, 2 * D_FF), lambda b, i: (b, 0, 0))],
        out_shape=[jax.ShapeDtypeStruct((batch * seq, D_MODEL), F32),
                   jax.ShapeDtypeStruct((batch, CONV_W - 1, 2 * D_FF), F32)],
        scratch_shapes=[pltpu.VMEM((tm // TS, D_MODEL // LANE, 8 * PITCH, LANE), F32),
                        pltpu.VMEM((16, 2 * D_FF), F32),
                        pltpu.VMEM((tm // TS, TS, D_FF), BF16)],
        compiler_params=pltpu.CompilerParams(
            dimension_semantics=("arbitrary", "arbitrary"),
            vmem_limit_bytes=V7X_VMEM_LIMIT_BYTES),
        name="ffn_prompt",
    )(x, o_b, o_a, w_out, g_post, g_pre, g_out, w_up, conv_w, conv_b, w_down)


def _ffn_sample_kernel(x_hbm, ob_hbm, oa_hbm, wo_ref, g_post_ref, g_pre_ref, g_out_ref,
                       wug_ref, wuu_ref, cwg_ref, cwu_ref, cbg_ref, cbu_ref, wd_ref,
                       stg_ref, stu_ref, y_hbm, sog_ref, sou_ref,
                       x_buf, ob_buf, oa_buf, sem, x1_scr, xn_scr, acc_scr,
                       *, dec_batch, dec_seq):
    j = pl.program_id(0)
    rows = dec_batch * dec_seq

    def position_copies(pairs, sem_ref, to_hbm=False):
        copies = []
        for hbm, buf in pairs:
            for t in range(dec_seq):
                ends = (hbm.at[:, t, :], buf.at[t * dec_batch:(t + 1) * dec_batch, :])
                copies.append(pltpu.make_async_copy(*(ends[::-1] if to_hbm else ends), sem_ref))
        return copies

    @pl.when(j == 0)
    def _():
        loads = position_copies([(x_hbm, x_buf), (ob_hbm, ob_buf), (oa_hbm, oa_buf)], sem.at[0])
        for cp in loads:
            cp.start()
        for cp in loads:
            cp.wait()
        x1, xn2 = _mix_and_prenorm(x_buf, ob_buf, oa_buf, wo_ref, g_post_ref, g_pre_ref)
        x1_scr[...] = x1
        xn_scr[...] = xn2.astype(BF16)
        acc_scr[...] = jnp.zeros_like(acc_scr)

    xn2 = xn_scr[...]
    conv = []
    for wu_ref, cw_ref, cb_ref, st_ref, so_ref in ((wug_ref, cwg_ref, cbg_ref, stg_ref, sog_ref),
                                                   (wuu_ref, cwu_ref, cbu_ref, stu_ref, sou_ref)):
        h = jnp.dot(xn2, wu_ref[...], preferred_element_type=F32)
        s0 = st_ref[0]
        s1 = st_ref[1]
        h1 = jnp.concatenate([s1, h[:rows - dec_batch]], axis=0)
        h2 = jnp.concatenate([s0, s1, h[:rows - 2 * dec_batch]], axis=0)
        conv.append(_conv_gate(cw_ref[...], cb_ref[...], h, h1, h2))
        so_ref[0] = h[rows - 2 * dec_batch:rows - dec_batch]
        so_ref[1] = h[rows - dec_batch:]
    act = (jax.nn.silu(conv[0]) * conv[1]).astype(BF16)
    acc_scr[...] += jnp.dot(act, wd_ref[...], preferred_element_type=F32)

    @pl.when(j == pl.num_programs(0) - 1)
    def _():
        x_buf[...] = x1_scr[...] + _rms(acc_scr[...], g_out_ref[...], NORM_EPS)
        stores = position_copies([(y_hbm, x_buf)], sem.at[1], to_hbm=True)
        for cp in stores:
            cp.start()
        for cp in stores:
            cp.wait()


def _ffn_sample(x, o_b, o_a, w_out, g_post, g_pre, g_out, w_up, conv_w, conv_b, w_down, state,
                *, layer, dec_batch, dec_seq):
    rows = dec_batch * dec_seq
    nj = D_FF // TF
    ns = CONV_W - 1

    def col_tile(shape, ax, first):
        return pl.BlockSpec((None,) + shape,
                            lambda j: (layer,) + tuple(first + j if a == ax else 0
                                                       for a in range(len(shape))))

    gate = functools.partial(col_tile, first=0)
    up = functools.partial(col_tile, first=nj)
    per_layer = functools.partial(_layer_spec, layer=layer)
    state_out = pl.BlockSpec((ns, dec_batch, TF), lambda j: (0, 0, j))
    any_spec = pl.BlockSpec(memory_space=pl.ANY)
    by_position = lambda a: a.reshape(dec_batch, dec_seq, a.shape[-1])
    sds = jax.ShapeDtypeStruct
    y, st_gate, st_up = pl.pallas_call(
        functools.partial(_ffn_sample_kernel, dec_batch=dec_batch, dec_seq=dec_seq),
        grid=(nj,),
        in_specs=[
            any_spec, any_spec, any_spec,
            per_layer((D_MODEL, D_MODEL)),
            per_layer((1, D_MODEL)), per_layer((1, D_MODEL)), per_layer((1, D_MODEL)),
            gate((D_MODEL, TF), 1), up((D_MODEL, TF), 1),
            gate((CONV_W, TF), 1), up((CONV_W, TF), 1),
            gate((1, TF), 1), up((1, TF), 1),
            gate((TF, D_MODEL), 0),
            gate((ns, dec_batch, TF), 2), up((ns, dec_batch, TF), 2),
        ],
        out_specs=[any_spec, state_out, state_out],
        out_shape=[sds((dec_batch, dec_seq, D_MODEL), F32), sds((ns, dec_batch, D_FF), F32),
                   sds((ns, dec_batch, D_FF), F32)],
        scratch_shapes=[pltpu.VMEM((rows, D_MODEL), F32), pltpu.VMEM((rows, DIFF_WIDTH), F32),
                        pltpu.VMEM((rows, A_WIDTH), F32),
                        pltpu.SemaphoreType.DMA((2,)),
                        pltpu.VMEM((rows, D_MODEL), F32), pltpu.VMEM((rows, D_MODEL), BF16),
                        pltpu.VMEM((rows, D_MODEL), F32)],
        compiler_params=pltpu.CompilerParams(
            dimension_semantics=("arbitrary",), vmem_limit_bytes=V7X_VMEM_LIMIT_BYTES),
        name="ffn_sample",
    )(by_position(x), by_position(o_b), by_position(o_a), w_out, g_post, g_pre, g_out,
      w_up, w_up, conv_w, conv_w, conv_b, conv_b, w_down, state, state)
    return y.reshape(rows, D_MODEL), jnp.concatenate([st_gate, st_up], axis=-1)


def kernel(x_prompt, x_sample, cache_k, cache_v, state_conv, page_table, rel_bias, norm_mix_pre, norm_mix_post, norm_ffn_pre, norm_ffn_post, w_in, lambda_q1, lambda_k1, lambda_q2, lambda_k2, subln_gain, sgu_ln_gain, sgu_ln_bias, w_spatial, b_spatial, w_out, w_up, conv_w, conv_b, w_down):
    batch, seq, _ = x_prompt.shape
    dec_batch, dec_seq, _ = x_sample.shape
    assert seq % TQ == 0 and seq % TM_FFN == 0 and (batch * seq) % TM_IN == 0
    assert (dec_batch * dec_seq) % TM_IN == 0 and CHUNK % dec_seq == 0 and TQ == TK

    y_p = x_prompt.reshape(batch * seq, D_MODEL)
    y_s = x_sample.reshape(dec_batch * dec_seq, D_MODEL)
    reps = CHUNK // dec_seq

    rows3 = lambda a: a.reshape(DEPTH, 1, -1)
    w_in_b, w_out_b = w_in.astype(BF16), w_out.astype(BF16)
    w_up_b, w_down_b = w_up.astype(BF16), w_down.astype(BF16)
    lamv = jnp.stack([lambda_q1, lambda_k1, lambda_q2, lambda_k2], axis=1)
    sg_rows = jnp.broadcast_to(subln_gain[:, :, None], (DEPTH, V_HEAD_DIM, TQ))
    ffn_args = (w_out_b, rows3(norm_mix_post), rows3(norm_ffn_pre), rows3(norm_ffn_post),
                w_up_b, conv_w, rows3(conv_b), w_down_b)
    in_args = (rows3(norm_mix_pre), w_in_b, rows3(sgu_ln_gain), rows3(sgu_ln_bias))
    gate_shape = (DEPTH, A_GROUPS, CHUNK, A_GROUP_DIM)
    bsp_p = jnp.broadcast_to(b_spatial[:, :, :, None], gate_shape)
    wsp_s = jnp.tile(w_spatial[:, :, :dec_seq, :dec_seq], (1, 1, reps, reps))
    bsp_s = jnp.broadcast_to(jnp.tile(b_spatial[:, :, :dec_seq], (1, 1, reps))[:, :, :, None],
                             gate_shape)
    state = state_conv.transpose(0, 2, 1, 3)

    cp, csm, svs = [], [], []
    kv_p = kv_s = None
    for l in range(DEPTH):
        lam_init = _lambda_init(l)

        qt, kf, vf, kb, vt, o_a = _inproj(y_p, *in_args, w_spatial, bsp_p, kv_p,
                                          layer=l, sample=False)
        kv_p = (kf, vf)
        q_s, kf, vf, o_a_s, vn = _inproj(y_s, *in_args, wsp_s, bsp_s, kv_s, layer=l, sample=True)
        kv_s = (kf, vf)
        o_b, o_b_s = _attention(qt, kb, vt, sg_rows, q_s, kv_s, cache_k, cache_v, page_table,
                                rel_bias, lamv, rows3(subln_gain), layer=l, batch=batch, seq=seq,
                                dec_seq=dec_seq, lam_init=lam_init)
        y_p, conv_p = _ffn_prompt(y_p, o_b, o_a, *ffn_args, layer=l, batch=batch, seq=seq)
        cp.append(conv_p)
        y_s, conv_s = _ffn_sample(y_s, o_b_s, o_a_s, *ffn_args, state,
                                  layer=l, dec_batch=dec_batch, dec_seq=dec_seq)
        csm.append(conv_s.transpose(1, 0, 2))
        svs.append(vn.reshape(dec_batch, dec_seq, A_WIDTH))

    kv_shape_p = (DEPTH, batch, seq, H_DIFF, HEAD_W)
    kv_shape_s = (DEPTH, dec_batch, dec_seq, H_DIFF, HEAD_W)
    return (y_p.reshape(batch, seq, D_MODEL), y_s.reshape(dec_batch, dec_seq, D_MODEL),
            kv_p[0].reshape(kv_shape_p), kv_p[1].reshape(kv_shape_p), jnp.stack(cp),
            kv_s[0].reshape(kv_shape_s), kv_s[1].reshape(kv_shape_s),
            jnp.stack(csm), jnp.stack(svs))
```

```python
import functools
import math

import jax
import jax.numpy as jnp
import numpy as np
from jax import lax
from jax.experimental import pallas as pl
from jax.experimental.pallas import tpu as pltpu

F32 = jnp.float32
BF16 = jnp.bfloat16

D_MODEL = 1024
DEPTH = 2
PAGE_SIZE = 128
H_DIFF = 4
QK_HEAD_DIM = 64
V_HEAD_DIM = 128
HEAD_W = 2 * QK_HEAD_DIM
DIFF_WIDTH = H_DIFF * V_HEAD_DIM
A_GROUPS = 4
A_WIDTH = 512
A_GROUP_DIM = 128
CHUNK = 128
LANE = 128
Q_COLS = H_DIFF * HEAD_W
IN_COLS = 3 * Q_COLS + 2 * A_WIDTH
D_FF = 2816
CONV_W = 3
REL_BUCKETS = 32
REL_MAX_EXACT = 16
REL_MAX_DIST = 128
NORM_EPS = 1e-6
SUBLN_EPS = 1e-5
NEG_INF = -1e30
LOG2_E = math.log2(math.e)

V7X_VMEM_LIMIT_BYTES = 56 * 1024 * 1024

TM_IN = 1024
TQ = 256
TK = 256
DENOM_ROWS = 16
SPS = 2
TM_FFN = 512
TS = 256
PITCH = TS // 8 + 8
TF = 256


def _lambda_init(layer):
    return 0.8 - 0.6 * math.exp(-0.3 * layer)


def _bucket_table(dist):
    n = np.maximum(dist, 0)
    nf = np.maximum(n, 1).astype(np.float32)
    large = REL_MAX_EXACT + (np.log(nf / REL_MAX_EXACT) / math.log(REL_MAX_DIST / REL_MAX_EXACT)
                             * (REL_BUCKETS - REL_MAX_EXACT)).astype(np.int32)
    large = np.minimum(large, REL_BUCKETS - 1)
    return np.where(dist >= 0, np.where(n < REL_MAX_EXACT, n, large), -1).astype(np.int32)


def _const_spec(shape):
    nd = len(shape)
    return pl.BlockSpec(shape, lambda *_: (0,) * nd, pipeline_mode=pl.Buffered(1))


def _layer_spec(shape, layer):
    nd = len(shape)
    return pl.BlockSpec((None,) + tuple(shape), lambda *_: (layer,) + (0,) * nd,
                        pipeline_mode=pl.Buffered(1))


def _rms(x, gain, eps):
    y = x * lax.rsqrt(jnp.mean(x * x, axis=-1, keepdims=True) + eps)
    return y * gain


def _nt_dot(a, b):
    return lax.dot_general(a, b, (((1,), (1,)), ((), ())), preferred_element_type=F32)


def _diff_lambda(lamv_ref, lam_init):
    p = lamv_ref[...]
    a = jnp.sum(p[0:1] * p[1:2], axis=-1, keepdims=True)
    b = jnp.sum(p[2:3] * p[3:4], axis=-1, keepdims=True)
    return jnp.exp(a) - jnp.exp(b) + lam_init


def _bias_from_buckets(idx, rb_ref, head, scale):
    far = rb_ref[REL_BUCKETS - 1, head]
    tbl = jnp.full(idx.shape, NEG_INF, F32)
    for b in range(REL_BUCKETS):
        tbl = jnp.where(idx == b, (rb_ref[b, head] - far) * scale, tbl)
    return tbl


def _inproj_kernel(x_ref, g_ref, w_ref, lng_ref, lnb_ref, wsp_ref, bsp_ref, *out_refs,
                   tm, sample, n_aliased):
    out_refs = out_refs[n_aliased:]
    if sample:
        q_ref, kf_ref, vf_ref, oa_ref, vn_ref = out_refs
    else:
        qt_ref, kf_ref, vf_ref, kb_ref, vt_ref, oa_ref = out_refs

    nsub = tm // TS
    a0 = 3 * Q_COLS

    def norm(t):
        return _rms(x_ref[t * TS:(t + 1) * TS, :], g_ref[...], NORM_EPS).astype(BF16)

    def proj(xn, c0, c1):
        return jnp.dot(xn, w_ref[:, c0:c1], preferred_element_type=F32)

    def qkv(t, xn):
        rows = slice(t * TS, (t + 1) * TS)
        q = proj(xn, 0, Q_COLS)
        k = proj(xn, Q_COLS, 2 * Q_COLS)
        v = proj(xn, 2 * Q_COLS, 3 * Q_COLS)
        for h in range(H_DIFF):
            head_rows = pl.ds(t * TS * H_DIFF + h, TS, stride=H_DIFF)
            for d in range(kf_ref.shape[0]):
                kf_ref[d, head_rows, :] = k[:, h * HEAD_W:(h + 1) * HEAD_W]
                vf_ref[d, head_rows, :] = v[:, h * V_HEAD_DIM:(h + 1) * V_HEAD_DIM]
        if sample:
            q_ref[rows, :] = q * (QK_HEAD_DIM ** -0.5)
        else:
            qt_ref[:, rows] = (q * (QK_HEAD_DIM ** -0.5 * LOG2_E)).T.astype(BF16)
            vt_ref[:, rows] = v.T.astype(BF16)
            kb_ref[rows, :] = k.astype(BF16)

    def a_proj(xn):
        return proj(xn, a0, a0 + A_WIDTH), proj(xn, a0 + A_WIDTH, a0 + 2 * A_WIDTH)

    row = lax.broadcasted_iota(jnp.int32, (CHUNK, CHUNK), 0)
    col = lax.broadcasted_iota(jnp.int32, (CHUNK, CHUNK), 1)
    if sample:
        keep = jnp.logical_and((row >> 3) == (col >> 3), (col & 7) <= (row & 7))
    else:
        keep = col <= row
    w_mix = [jnp.where(keep, wsp_ref[g], 0.0).astype(BF16) for g in range(A_GROUPS)]

    def gate(t, u_raw, v_raw):
        u_a = jax.nn.gelu(u_raw)
        v_a = jax.nn.gelu(v_raw)
        xc = v_a - jnp.mean(v_a, axis=-1, keepdims=True)
        vn = xc * lax.rsqrt(jnp.mean(xc * xc, axis=-1, keepdims=True) + NORM_EPS)
        vn = vn * lng_ref[...] + lnb_ref[...]
        if sample:
            vn_ref[t * TS:(t + 1) * TS, :] = vn
        vnb = vn.astype(BF16)
        for g in range(A_GROUPS):
            c0, c1 = g * A_GROUP_DIM, (g + 1) * A_GROUP_DIM
            for c in range(TS // CHUNK):
                r0, r1 = c * CHUNK, (c + 1) * CHUNK
                s = jnp.dot(w_mix[g], vnb[r0:r1, c0:c1], preferred_element_type=F32) + bsp_ref[g]
                oa_ref[t * TS + r0:t * TS + r1, c0:c1] = (u_a[r0:r1, c0:c1] * s).astype(
                    oa_ref.dtype)

    xns = [norm(t) for t in range(nsub)]
    for t in range(nsub):
        a_cur = a_proj(xns[t])
        qkv(t, xns[t])
        gate(t, *a_cur)


def _inproj(x, gain, w_in, ln_g, ln_b, wsp, bsp, kv_all, *, layer, sample):
    rows = x.shape[0]
    tm = TM_IN
    row_spec = lambda w: pl.BlockSpec((tm, w), lambda i: (i, 0))
    assert kv_all is not None or layer == 0
    n_slabs = DEPTH if kv_all is None else 1
    head_row_spec = pl.BlockSpec((n_slabs, tm * H_DIFF, HEAD_W),
                                 lambda i: (layer // n_slabs, i, 0))
    col_spec = pl.BlockSpec((Q_COLS, tm), lambda i: (0, i))
    in_specs = [
        row_spec(D_MODEL),
        _layer_spec((1, D_MODEL), layer),
        _layer_spec((D_MODEL, IN_COLS), layer),
        _layer_spec((1, A_WIDTH), layer),
        _layer_spec((1, A_WIDTH), layer),
        _layer_spec((A_GROUPS, CHUNK, CHUNK), layer),
        _layer_spec((A_GROUPS, CHUNK, A_GROUP_DIM), layer),
    ]
    sds = jax.ShapeDtypeStruct
    kv_shape = sds((DEPTH, rows * H_DIFF, HEAD_W), F32)
    aliased = () if kv_all is None else tuple(kv_all)
    n_in = len(in_specs)
    in_specs += [pl.BlockSpec(memory_space=pl.ANY)] * len(aliased)
    aliases = {n_in + i: 1 + i for i in range(len(aliased))}
    if sample:
        out_shape = [sds((rows, Q_COLS), F32), kv_shape, kv_shape,
                     sds((rows, A_WIDTH), F32), sds((rows, A_WIDTH), F32)]
        out_specs = [row_spec(Q_COLS), head_row_spec, head_row_spec,
                     row_spec(A_WIDTH), row_spec(A_WIDTH)]
    else:
        out_shape = [sds((Q_COLS, rows), BF16), kv_shape, kv_shape,
                     sds((rows, Q_COLS), BF16), sds((Q_COLS, rows), BF16),
                     sds((rows, A_WIDTH), BF16)]
        out_specs = [col_spec, head_row_spec, head_row_spec,
                     row_spec(Q_COLS), col_spec, row_spec(A_WIDTH)]
    return pl.pallas_call(
        functools.partial(_inproj_kernel, tm=tm, sample=sample, n_aliased=len(aliased)),
        grid=(rows // tm,),
        in_specs=in_specs,
        out_specs=out_specs,
        out_shape=out_shape,
        input_output_aliases=aliases,
        compiler_params=pltpu.CompilerParams(
            dimension_semantics=("arbitrary",), vmem_limit_bytes=V7X_VMEM_LIMIT_BYTES),
        name="inproj_sample" if sample else "inproj_prompt",
    )(x, gain, w_in, ln_g, ln_b, wsp, bsp, *aliased)


def _prompt_attn_step(first, qi, rb_ref, lamv_ref, idx_ref, sg_ref, qt_ref, k_ref, vt_ref, o_ref,
                      bias_scr, qw_scr, m_scr, acc_scr, *, lam_init):
    @pl.when(first)
    def _():
        for h in range(H_DIFF):
            for t in range(2):
                bias_scr[h, t] = _bias_from_buckets(idx_ref[t], rb_ref, h, LOG2_E)

    drow = lax.broadcasted_iota(jnp.int32, (HEAD_W, TQ), 0)
    first_map = drow < QK_HEAD_DIM
    for h in range(H_DIFF):
        qh = qt_ref[h * HEAD_W:(h + 1) * HEAD_W, :]
        zero = jnp.zeros_like(qh)
        qw_scr[h] = jnp.concatenate(
            [jnp.where(first_map, qh, zero), jnp.where(first_map, zero, qh)], axis=1)
    m_scr[...] = jnp.full_like(m_scr, NEG_INF)
    acc_scr[...] = jnp.zeros_like(acc_scr)
    ones_rows = jnp.ones((DENOM_ROWS, TK), BF16)

    def key_tiles(tiles):
        units = [(pl.multiple_of(j * TK, TK), kind, h) for j, kind in tiles for h in range(H_DIFF)]

        def scores(u):
            r0, _, h = u
            kj = k_ref[pl.ds(r0, TK), h * HEAD_W:(h + 1) * HEAD_W]
            return jnp.dot(kj, qw_scr[h], preferred_element_type=F32)

        def softmax(u, st):
            _, kind, h = u
            if kind is not None:
                bias = bias_scr[h, kind]
                st = jnp.concatenate([bias, bias], axis=1) + st
            m_old = m_scr[h]
            m_new = jnp.maximum(m_old, jnp.max(st, axis=0, keepdims=True))
            m_scr[h] = m_new
            return jnp.exp2(m_old - m_new), jnp.exp2(st - m_new).astype(BF16)

        def accumulate(u, alpha, pt):
            r0, _, h = u
            vtj = jnp.concatenate([vt_ref[h * HEAD_W:(h + 1) * HEAD_W, pl.ds(r0, TK)],
                                   ones_rows], axis=0)
            acc_scr[h] = alpha * acc_scr[h] + jnp.dot(vtj, pt, preferred_element_type=F32)

        n = len(units)
        st = {0: scores(units[0]), 1: scores(units[1])}
        p = {}
        for i in range(n):
            p[i] = softmax(units[i], st.pop(i))
            if i + 2 < n:
                st[i + 2] = scores(units[i + 2])
            if i >= 1:
                accumulate(units[i - 1], *p.pop(i - 1))
        accumulate(units[n - 1], *p.pop(n - 1))

    n_far = jnp.maximum(qi - 1, 0)

    def far_pair(jj, carry):
        key_tiles([(2 * jj, None), (2 * jj + 1, None)])
        return carry

    lax.fori_loop(0, n_far // 2, far_pair, 0)

    @pl.when(n_far % 2 == 1)
    def _():
        key_tiles([(n_far - 1, None)])

    def finalize():
        lam = _diff_lambda(lamv_ref, lam_init)
        for h in range(H_DIFF):
            acc = acc_scr[h, 0:V_HEAD_DIM, :]
            rl = 1.0 / acc_scr[h, V_HEAD_DIM:V_HEAD_DIM + 1, :]
            ot = acc[:, :TQ] * rl[:, :TQ] - lam * (acc[:, TQ:] * rl[:, TQ:])
            ot = ot * lax.rsqrt(jnp.mean(ot * ot, axis=0, keepdims=True) + SUBLN_EPS)
            ot = ot * sg_ref[...] * (1.0 - lam_init)
            o_ref[:, h * V_HEAD_DIM:(h + 1) * V_HEAD_DIM] = ot.T.astype(o_ref.dtype)

    @pl.when(qi >= 1)
    def _():
        key_tiles([(qi - 1, 1), (qi, 0)])
        finalize()

    @pl.when(qi == 0)
    def _():
        key_tiles([(qi, 0)])
        finalize()


def _sample_page_copies(step, sl, pt_ref, ck_hbm, cv_hbm, kbuf, vbuf, sem, *, layer, n_pages):
    copies = []
    for i in range(SPS):
        for p in range(n_pages):
            page = pt_ref[step * SPS + i, p]
            copies.append(pltpu.make_async_copy(ck_hbm.at[layer, page], kbuf.at[sl, i, p],
                                                sem.at[sl, 0]))
            copies.append(pltpu.make_async_copy(cv_hbm.at[layer, page], vbuf.at[sl, i, p],
                                                sem.at[sl, 1]))
    return copies


def _sample_attn_step(step, first, slot, rb_ref, lamv_ref, idx_ref, sg_ref, q_ref, kn_ref,
                      vn_ref, o_ref, kbuf, vbuf, s_scr, kpad, vpad, bias_scr,
                      *, n_pages, dec_seq, lam_init):
    del step
    rows = 2 * dec_seq
    past = n_pages * PAGE_SIZE

    @pl.when(first)
    def _():
        for h in range(H_DIFF):
            for t in range(2):
                bias_scr[t, h * rows:(h + 1) * rows, :] = _bias_from_buckets(
                    idx_ref[t], rb_ref, h, 1.0)

    lane = lax.broadcasted_iota(jnp.int32, (dec_seq, HEAD_W), 1)
    first_map = lane < QK_HEAD_DIM
    lam = _diff_lambda(lamv_ref, lam_init)

    def one_sequence(i, carry):
        _sample_attn_sequence(i)
        return carry

    def _sample_attn_sequence(i):
        q0 = pl.multiple_of(i * dec_seq, dec_seq)

        def new_rows(ref, h):
            return ref[pl.ds(i * (dec_seq * H_DIFF) + h, dec_seq, stride=H_DIFF), :]

        def page_head(buf, p, h):
            return buf[slot, i, p, pl.ds(h, PAGE_SIZE, stride=H_DIFF), :]

        kpad[...] = jnp.zeros_like(kpad)
        vpad[...] = jnp.zeros_like(vpad)
        for h in range(H_DIFF):
            kpad[h, 0:dec_seq, :] = new_rows(kn_ref, h)
            vpad[h, 0:dec_seq, :] = new_rows(vn_ref, h)

        for h in range(H_DIFF):
            qh = q_ref[pl.ds(q0, dec_seq), h * HEAD_W:(h + 1) * HEAD_W]
            qm = jnp.concatenate([jnp.where(first_map, qh, 0.0), jnp.where(first_map, 0.0, qh)],
                                 axis=0).astype(BF16)
            r0, r1 = h * rows, (h + 1) * rows
            for p in range(0, n_pages, 2):
                k2 = jnp.concatenate([page_head(kbuf, p, h), page_head(kbuf, p + 1, h)],
                                     axis=0).astype(BF16)
                s_scr[r0:r1, p * PAGE_SIZE:(p + 2) * PAGE_SIZE] = _nt_dot(qm, k2)
            s_scr[r0:r1, past - PAGE_SIZE:past] += bias_scr[0, r0:r1, :]
            s_scr[r0:r1, past:] = _nt_dot(qm, kpad[h].astype(BF16)) + bias_scr[1, r0:r1, :]

        s = s_scr[...]
        e = jnp.exp(s - jnp.max(s, axis=-1, keepdims=True))
        pn = e / jnp.sum(e, axis=-1, keepdims=True)
        for h in range(H_DIFF):
            r0 = h * rows
            a = (pn[r0:r0 + dec_seq] - lam * pn[r0 + dec_seq:r0 + rows]).astype(BF16)
            acc = jnp.dot(a[:, past:], vpad[h].astype(BF16), preferred_element_type=F32)
            for p in range(n_pages):
                acc = acc + jnp.dot(a[:, p * PAGE_SIZE:(p + 1) * PAGE_SIZE],
                                    page_head(vbuf, p, h).astype(BF16),
                                    preferred_element_type=F32)
            o_ref[pl.ds(q0, dec_seq), h * V_HEAD_DIM:(h + 1) * V_HEAD_DIM] = (
                _rms(acc, sg_ref[...], SUBLN_EPS) * (1.0 - lam_init))

    lax.fori_loop(0, SPS, one_sequence, 0)


def _attention_kernel(pt_ref, rb_ref, lamv_ref,
                      idx_p_ref, sg_p_ref, qt_ref, k_ref, vt_ref,
                      idx_s_ref, sg_s_ref, q_ref, kn_ref, vn_ref, ck_hbm, cv_hbm,
                      op_ref, os_ref,
                      bias_p, qw_scr, m_scr, acc_scr,
                      kbuf, vbuf, sem, s_scr, kpad, vpad, bias_s,
                      *, layer, n_pages, dec_seq, lam_init):
    qi = pl.program_id(1)
    step = pl.program_id(0) * pl.num_programs(1) + qi
    nsteps = pl.num_programs(0) * pl.num_programs(1)
    first = step == 0
    slot = lax.rem(step, 2)
    copies = functools.partial(_sample_page_copies, pt_ref=pt_ref, ck_hbm=ck_hbm, cv_hbm=cv_hbm,
                               kbuf=kbuf, vbuf=vbuf, sem=sem, layer=layer, n_pages=n_pages)

    @pl.when(first)
    def _():
        for cp in copies(0, 0):
            cp.start()

    @pl.when(step + 1 < nsteps)
    def _():
        for cp in copies(step + 1, 1 - slot):
            cp.start()

    _prompt_attn_step(first, qi, rb_ref, lamv_ref, idx_p_ref, sg_p_ref, qt_ref, k_ref, vt_ref,
                      op_ref, bias_p, qw_scr, m_scr, acc_scr, lam_init=lam_init)

    for cp in copies(step, slot):
        cp.wait()
    _sample_attn_step(step, first, slot, rb_ref, lamv_ref, idx_s_ref, sg_s_ref, q_ref, kn_ref,
                      vn_ref, os_ref, kbuf, vbuf, s_scr, kpad, vpad, bias_s,
                      n_pages=n_pages, dec_seq=dec_seq, lam_init=lam_init)


def _attention(qt, k, vt, sg_p, q_s, kv_s, cache_k, cache_v, page_table, rel_bias, lamv, sg_s,
               *, layer, batch, seq, dec_seq, lam_init):
    nq = seq // TQ
    kk = np.arange(TK)[:, None]
    qq = np.arange(TQ)[None, :]
    idx_p = jnp.asarray(np.stack([_bucket_table(qq - kk), _bucket_table(TQ + qq - kk)]))

    dec_batch, n_pages = page_table.shape
    assert n_pages % 2 == 0 and dec_batch == SPS * batch * nq
    n_pool = cache_k.shape[1]
    page_rows = PAGE_SIZE * H_DIFF
    ck = cache_k.reshape(DEPTH, n_pool, page_rows, HEAD_W)
    cv = cache_v.reshape(DEPTH, n_pool, page_rows, V_HEAD_DIM)
    past_len = n_pages * PAGE_SIZE
    rows = 2 * dec_seq
    i = (np.arange(rows) % dec_seq)[:, None]
    c = np.arange(PAGE_SIZE)[None, :]
    idx_last = _bucket_table(past_len + i - (past_len - PAGE_SIZE + c))
    idx_new = np.where(c < dec_seq, _bucket_table(i - c), -1).astype(np.int32)
    idx_s = jnp.asarray(np.stack([idx_last, idx_new]))
    nkeys = past_len + PAGE_SIZE

    step = lambda b, i: b * nq + i
    const = lambda shape: pl.BlockSpec(shape, lambda b, i, pt: (0,) * len(shape),
                                       pipeline_mode=pl.Buffered(1))
    per_layer = lambda shape: pl.BlockSpec((None,) + shape,
                                           lambda b, i, pt: (layer,) + (0,) * len(shape),
                                           pipeline_mode=pl.Buffered(1))
    srow_spec = pl.BlockSpec((SPS * dec_seq, Q_COLS), lambda b, i, pt: (step(b, i), 0))
    snew_spec = pl.BlockSpec((None, SPS * dec_seq * H_DIFF, HEAD_W),
                             lambda b, i, pt: (layer, step(b, i), 0))
    grid_spec = pltpu.PrefetchScalarGridSpec(
        num_scalar_prefetch=1,
        grid=(batch, nq),
        in_specs=[
            pl.BlockSpec(memory_space=pltpu.SMEM),
            per_layer((4, QK_HEAD_DIM)),
            const((2, TK, TQ)),
            per_layer((V_HEAD_DIM, TQ)),
            pl.BlockSpec((Q_COLS, TQ), lambda b, i, pt: (0, step(b, i))),
            pl.BlockSpec((seq, Q_COLS), lambda b, i, pt: (b, 0)),
            pl.BlockSpec((Q_COLS, seq), lambda b, i, pt: (0, b)),
            const((2, rows, PAGE_SIZE)),
            per_layer((1, V_HEAD_DIM)),
            srow_spec, snew_spec, snew_spec,
            pl.BlockSpec(memory_space=pl.ANY), pl.BlockSpec(memory_space=pl.ANY),
        ],
        out_specs=[pl.BlockSpec((TQ, DIFF_WIDTH), lambda b, i, pt: (step(b, i), 0)), srow_spec],
        scratch_shapes=[
            pltpu.VMEM((H_DIFF, 2, TK, TQ), F32),
            pltpu.VMEM((H_DIFF, HEAD_W, 2 * TQ), BF16),
            pltpu.VMEM((H_DIFF, 1, 2 * TQ), F32),
            pltpu.VMEM((H_DIFF, V_HEAD_DIM + DENOM_ROWS, 2 * TQ), F32),
            pltpu.VMEM((2, SPS, n_pages, page_rows, HEAD_W), F32),
            pltpu.VMEM((2, SPS, n_pages, page_rows, V_HEAD_DIM), F32),
            pltpu.SemaphoreType.DMA((2, 2)),
            pltpu.VMEM((H_DIFF * rows, nkeys), F32),
            pltpu.VMEM((H_DIFF, PAGE_SIZE, HEAD_W), F32),
            pltpu.VMEM((H_DIFF, PAGE_SIZE, V_HEAD_DIM), F32),
            pltpu.VMEM((2, H_DIFF * rows, PAGE_SIZE), F32),
        ],
    )
    return pl.pallas_call(
        functools.partial(_attention_kernel, layer=layer, n_pages=n_pages, dec_seq=dec_seq,
                          lam_init=lam_init),
        grid_spec=grid_spec,
        out_shape=[jax.ShapeDtypeStruct((batch * seq, DIFF_WIDTH), BF16),
                   jax.ShapeDtypeStruct((dec_batch * dec_seq, DIFF_WIDTH), F32)],
        compiler_params=pltpu.CompilerParams(
            dimension_semantics=("arbitrary", "arbitrary"),
            vmem_limit_bytes=V7X_VMEM_LIMIT_BYTES),
        name="attention",
    )(page_table, rel_bias, lamv, idx_p, sg_p, qt, k, vt, idx_s, sg_s, q_s, kv_s[0], kv_s[1],
      ck, cv)


def _mix_and_prenorm(x_ref, ob_ref, oa_ref, wo_ref, g_post_ref, g_pre_ref):
    mix = jnp.dot(ob_ref[...].astype(BF16), wo_ref[0:DIFF_WIDTH, :], preferred_element_type=F32)
    mix = mix + jnp.dot(oa_ref[...].astype(BF16), wo_ref[DIFF_WIDTH:, :],
                        preferred_element_type=F32)
    x1 = x_ref[...] + _rms(mix, g_post_ref[...], NORM_EPS)
    return x1, _rms(x1, g_pre_ref[...], NORM_EPS)


def _conv_gate(cw, cb, h, h1, h2):
    return cb + cw[0:1] * h2 + cw[1:2] * h1 + cw[2:3] * h


def _ffn_prompt_kernel(x_ref, ob_ref, oa_ref, wo_ref, g_post_ref, g_pre_ref, g_out_ref,
                       wu_ref, cw_ref, cb_ref, wd_ref, y_ref, st_ref, perm, carry, act,
                       *, tm):
    nsub = tm // TS
    nv = TS // 8
    nj = D_FF // TF
    ncb = D_MODEL // LANE

    @pl.when(pl.program_id(1) == 0)
    def _():
        carry[...] = jnp.zeros_like(carry)

    def pre(t):
        rows = slice(t * TS, (t + 1) * TS)
        x1, xn2 = _mix_and_prenorm(x_ref.at[rows], ob_ref.at[rows], oa_ref.at[rows], wo_ref,
                                   g_post_ref, g_pre_ref)
        y_ref[rows, :] = x1
        for c in range(ncb):
            for s in range(8):
                perm[t, c, s * PITCH:s * PITCH + nv, :] = xn2[s * nv:(s + 1) * nv,
                                                              c * LANE:(c + 1) * LANE]
        return jnp.concatenate(
            [jnp.concatenate([perm[t, c, pl.ds(v, 8, stride=PITCH), :] for c in range(ncb)],
                             axis=1)
             for v in range(nv)], axis=0).astype(BF16)

    first_row = lax.broadcasted_iota(jnp.int32, (8, TF), 0) == 0

    def conv_ffn(t, xp):
        def up(j):
            return [jnp.dot(xp, wu_ref[:, half * D_FF + j * TF:half * D_FF + (j + 1) * TF],
                            preferred_element_type=F32) for half in range(2)]

        h_next = up(0)
        for j in range(nj):
            h_cur = h_next
            if j + 1 < nj:
                h_next = up(j + 1)
            conv = []
            for half in range(2):
                c0 = half * D_FF + j * TF
                c1 = c0 + TF
                h = h_cur[half]
                prev = carry[:, c0:c1]
                head = [jnp.where(first_row, pltpu.roll(prev[8 * i:8 * i + 8], 1, 0),
                                  pltpu.roll(h[TS - 16 + 8 * i:TS - 8 + 8 * i], 1, 0))
                        for i in range(2)]
                h1 = jnp.concatenate([head[1], h[:TS - 8]], axis=0)
                h2 = jnp.concatenate([head[0], head[1], h[:TS - 16]], axis=0)
                carry[:, c0:c1] = h[TS - 16:]
                conv.append(_conv_gate(cw_ref[:, c0:c1], cb_ref[:, c0:c1], h, h1, h2))
            act[t, :, j * TF:(j + 1) * TF] = (jax.nn.silu(conv[0]) * conv[1]).astype(BF16)

    def post(t):
        rows = slice(t * TS, (t + 1) * TS)
        fp = jnp.dot(act[t], wd_ref[...], preferred_element_type=F32)
        for v in range(nv):
            for c in range(ncb):
                perm[t, c, pl.ds(v, 8, stride=PITCH), :] = fp[8 * v:8 * v + 8,
                                                              c * LANE:(c + 1) * LANE]
        f = jnp.concatenate(
            [jnp.concatenate([perm[t, c, s * PITCH:s * PITCH + nv, :] for s in range(8)], axis=0)
             for c in range(ncb)], axis=1)
        y_ref[rows, :] = y_ref[rows, :] + _rms(f, g_out_ref[...], NORM_EPS)

    xps = [pre(t) for t in range(nsub)]
    for t in range(nsub):
        conv_ffn(t, xps[t])
        post(t)
    st_ref[0, 0:1, :] = carry[7:8, :]
    st_ref[0, 1:2, :] = carry[15:16, :]


def _ffn_prompt(x, o_b, o_a, w_out, g_post, g_pre, g_out, w_up, conv_w, conv_b, w_down,
                *, layer, batch, seq):
    tm = TM_FFN
    nt = seq // tm
    row_spec = lambda w: pl.BlockSpec((tm, w), lambda b, i: (b * nt + i, 0))
    per_layer = functools.partial(_layer_spec, layer=layer)
    return pl.pallas_call(
        functools.partial(_ffn_prompt_kernel, tm=tm),
        grid=(batch, nt),
        in_specs=[
            row_spec(D_MODEL), row_spec(DIFF_WIDTH), row_spec(A_WIDTH),
            per_layer((D_MODEL, D_MODEL)),
            per_layer((1, D_MODEL)), per_layer((1, D_MODEL)), per_layer((1, D_MODEL)),
            per_layer((D_MODEL, 2 * D_FF)),
            per_layer((CONV_W, 2 * D_FF)),
            per_layer((1, 2 * D_FF)),
            per_layer((D_FF, D_MODEL)),
        ],
        out_specs=[row_spec(D_MODEL),
                   pl.BlockSpec((1, CONV_W - 1, 2 * D_FF), lambda b, i: (b, 0, 0))],
        out_shape=[jax.ShapeDtypeStruct((batch * seq, D_MODEL), F32),
                   jax.ShapeDtypeStruct((batch, CONV_W - 1, 2 * D_FF), F32)],
        scratch_shapes=[pltpu.VMEM((tm // TS, D_MODEL // LANE, 8 * PITCH, LANE), F32),
                        pltpu.VMEM((16, 2 * D_FF), F32),
                        pltpu.VMEM((tm // TS, TS, D_FF), BF16)],
        compiler_params=pltpu.CompilerParams(
            dimension_semantics=("arbitrary", "arbitrary"),
            vmem_limit_bytes=V7X_VMEM_LIMIT_BYTES),
        name="ffn_prompt",
    )(x, o_b, o_a, w_out, g_post, g_pre, g_out, w_up, conv_w, conv_b, w_down)


def _ffn_sample_kernel(x_hbm, ob_hbm, oa_hbm, wo_ref, g_post_ref, g_pre_ref, g_out_ref,
                       wug_ref, wuu_ref, cwg_ref, cwu_ref, cbg_ref, cbu_ref, wd_ref,
                       stg_ref, stu_ref, y_hbm, sog_ref, sou_ref,
                       x_buf, ob_buf, oa_buf, sem, x1_scr, xn_scr, acc_scr,
                       *, dec_batch, dec_seq):
    j = pl.program_id(0)
    rows = dec_batch * dec_seq

    def position_copies(pairs, sem_ref, to_hbm=False):
        copies = []
        for hbm, buf in pairs:
            for t in range(dec_seq):
                ends = (hbm.at[:, t, :], buf.at[t * dec_batch:(t + 1) * dec_batch, :])
                copies.append(pltpu.make_async_copy(*(ends[::-1] if to_hbm else ends), sem_ref))
        return copies

    @pl.when(j == 0)
    def _():
        loads = position_copies([(x_hbm, x_buf), (ob_hbm, ob_buf), (oa_hbm, oa_buf)], sem.at[0])
        for cp in loads:
            cp.start()
        for cp in loads:
            cp.wait()
        x1, xn2 = _mix_and_prenorm(x_buf, ob_buf, oa_buf, wo_ref, g_post_ref, g_pre_ref)
        x1_scr[...] = x1
        xn_scr[...] = xn2.astype(BF16)
        acc_scr[...] = jnp.zeros_like(acc_scr)

    xn2 = xn_scr[...]
    conv = []
    for wu_ref, cw_ref, cb_ref, st_ref, so_ref in ((wug_ref, cwg_ref, cbg_ref, stg_ref, sog_ref),
                                                   (wuu_ref, cwu_ref, cbu_ref, stu_ref, sou_ref)):
        h = jnp.dot(xn2, wu_ref[...], preferred_element_type=F32)
        s0 = st_ref[0]
        s1 = st_ref[1]
        h1 = jnp.concatenate([s1, h[:rows - dec_batch]], axis=0)
        h2 = jnp.concatenate([s0, s1, h[:rows - 2 * dec_batch]], axis=0)
        conv.append(_conv_gate(cw_ref[...], cb_ref[...], h, h1, h2))
        so_ref[0] = h[rows - 2 * dec_batch:rows - dec_batch]
        so_ref[1] = h[rows - dec_batch:]
    act = (jax.nn.silu(conv[0]) * conv[1]).astype(BF16)
    acc_scr[...] += jnp.dot(act, wd_ref[...], preferred_element_type=F32)

    @pl.when(j == pl.num_programs(0) - 1)
    def _():
        x_buf[...] = x1_scr[...] + _rms(acc_scr[...], g_out_ref[...], NORM_EPS)
        stores = position_copies([(y_hbm, x_buf)], sem.at[1], to_hbm=True)
        for cp in stores:
            cp.start()
        for cp in stores:
            cp.wait()


def _ffn_sample(x, o_b, o_a, w_out, g_post, g_pre, g_out, w_up, conv_w, conv_b, w_down, state,
                *, layer, dec_batch, dec_seq):
    rows = dec_batch * dec_seq
    nj = D_FF // TF
    ns = CONV_W - 1

    def col_tile(shape, ax, first):
        return pl.BlockSpec((None,) + shape,
                            lambda j: (layer,) + tuple(first + j if a == ax else 0
                                                       for a in range(len(shape))))

    gate = functools.partial(col_tile, first=0)
    up = functools.partial(col_tile, first=nj)
    per_layer = functools.partial(_layer_spec, layer=layer)
    state_out = pl.BlockSpec((ns, dec_batch, TF), lambda j: (0, 0, j))
    any_spec = pl.BlockSpec(memory_space=pl.ANY)
    by_position = lambda a: a.reshape(dec_batch, dec_seq, a.shape[-1])
    sds = jax.ShapeDtypeStruct
    y, st_gate, st_up = pl.pallas_call(
        functools.partial(_ffn_sample_kernel, dec_batch=dec_batch, dec_seq=dec_seq),
        grid=(nj,),
        in_specs=[
            any_spec, any_spec, any_spec,
            per_layer((D_MODEL, D_MODEL)),
            per_layer((1, D_MODEL)), per_layer((1, D_MODEL)), per_layer((1, D_MODEL)),
            gate((D_MODEL, TF), 1), up((D_MODEL, TF), 1),
            gate((CONV_W, TF), 1), up((CONV_W, TF), 1),
            gate((1, TF), 1), up((1, TF), 1),
            gate((TF, D_MODEL), 0),
            gate((ns, dec_batch, TF), 2), up((ns, dec_batch, TF), 2),
        ],
        out_specs=[any_spec, state_out, state_out],
        out_shape=[sds((dec_batch, dec_seq, D_MODEL), F32), sds((ns, dec_batch, D_FF), F32),
                   sds((ns, dec_batch, D_FF), F32)],
        scratch_shapes=[pltpu.VMEM((rows, D_MODEL), F32), pltpu.VMEM((rows, DIFF_WIDTH), F32),
                        pltpu.VMEM((rows, A_WIDTH), F32),
                        pltpu.SemaphoreType.DMA((2,)),
                        pltpu.VMEM((rows, D_MODEL), F32), pltpu.VMEM((rows, D_MODEL), BF16),
                        pltpu.VMEM((rows, D_MODEL), F32)],
        compiler_params=pltpu.CompilerParams(
            dimension_semantics=("arbitrary",), vmem_limit_bytes=V7X_VMEM_LIMIT_BYTES),
        name="ffn_sample",
    )(by_position(x), by_position(o_b), by_position(o_a), w_out, g_post, g_pre, g_out,
      w_up, w_up, conv_w, conv_w, conv_b, conv_b, w_down, state, state)
    return y.reshape(rows, D_MODEL), jnp.concatenate([st_gate, st_up], axis=-1)


def kernel(x_prompt, x_sample, cache_k, cache_v, state_conv, page_table, rel_bias, norm_mix_pre, norm_mix_post, norm_ffn_pre, norm_ffn_post, w_in, lambda_q1, lambda_k1, lambda_q2, lambda_k2, subln_gain, sgu_ln_gain, sgu_ln_bias, w_spatial, b_spatial, w_out, w_up, conv_w, conv_b, w_down):
    batch, seq, _ = x_prompt.shape
    dec_batch, dec_seq, _ = x_sample.shape
    assert seq % TQ == 0 and seq % TM_FFN == 0 and (batch * seq) % TM_IN == 0
    assert (dec_batch * dec_seq) % TM_IN == 0 and CHUNK % dec_seq == 0 and TQ == TK

    y_p = x_prompt.reshape(batch * seq, D_MODEL)
    y_s = x_sample.reshape(dec_batch * dec_seq, D_MODEL)
    reps = CHUNK // dec_seq

    rows3 = lambda a: a.reshape(DEPTH, 1, -1)
    w_in_b, w_out_b = w_in.astype(BF16), w_out.astype(BF16)
    w_up_b, w_down_b = w_up.astype(BF16), w_down.astype(BF16)
    lamv = jnp.stack([lambda_q1, lambda_k1, lambda_q2, lambda_k2], axis=1)
    sg_rows = jnp.broadcast_to(subln_gain[:, :, None], (DEPTH, V_HEAD_DIM, TQ))
    ffn_args = (w_out_b, rows3(norm_mix_post), rows3(norm_ffn_pre), rows3(norm_ffn_post),
                w_up_b, conv_w, rows3(conv_b), w_down_b)
    in_args = (rows3(norm_mix_pre), w_in_b, rows3(sgu_ln_gain), rows3(sgu_ln_bias))
    gate_shape = (DEPTH, A_GROUPS, CHUNK, A_GROUP_DIM)
    bsp_p = jnp.broadcast_to(b_spatial[:, :, :, None], gate_shape)
    wsp_s = jnp.tile(w_spatial[:, :, :dec_seq, :dec_seq], (1, 1, reps, reps))
    bsp_s = jnp.broadcast_to(jnp.tile(b_spatial[:, :, :dec_seq], (1, 1, reps))[:, :, :, None],
                             gate_shape)
    state = state_conv.transpose(0, 2, 1, 3)

    cp, csm, svs = [], [], []
    kv_p = kv_s = None
    for l in range(DEPTH):
        lam_init = _lambda_init(l)

        qt, kf, vf, kb, vt, o_a = _inproj(y_p, *in_args, w_spatial, bsp_p, kv_p,
                                          layer=l, sample=False)
        kv_p = (kf, vf)
        q_s, kf, vf, o_a_s, vn = _inproj(y_s, *in_args, wsp_s, bsp_s, kv_s, layer=l, sample=True)
        kv_s = (kf, vf)
        o_b, o_b_s = _attention(qt, kb, vt, sg_rows, q_s, kv_s, cache_k, cache_v, page_table,
                                rel_bias, lamv, rows3(subln_gain), layer=l, batch=batch, seq=seq,
                                dec_seq=dec_seq, lam_init=lam_init)
        y_p, conv_p = _ffn_prompt(y_p, o_b, o_a, *ffn_args, layer=l, batch=batch, seq=seq)
        cp.append(conv_p)
        y_s, conv_s = _ffn_sample(y_s, o_b_s, o_a_s, *ffn_args, state,
                                  layer=l, dec_batch=dec_batch, dec_seq=dec_seq)
        csm.append(conv_s.transpose(1, 0, 2))
        svs.append(vn.reshape(dec_batch, dec_seq, A_WIDTH))

    kv_shape_p = (DEPTH, batch, seq, H_DIFF, HEAD_W)
    kv_shape_s = (DEPTH, dec_batch, dec_seq, H_DIFF, HEAD_W)
    return (y_p.reshape(batch, seq, D_MODEL), y_s.reshape(dec_batch, dec_seq, D_MODEL),
            kv_p[0].reshape(kv_shape_p), kv_p[1].reshape(kv_shape_p), jnp.stack(cp),
            kv_s[0].reshape(kv_shape_s), kv_s[1].reshape(kv_shape_s),
            jnp.stack(csm), jnp.stack(svs))
```

```python
import functools
import math

import jax
import jax.numpy as jnp
import numpy as np
from jax import lax
from jax.experimental import pallas as pl
from jax.experimental.pallas import tpu as pltpu

F32 = jnp.float32
BF16 = jnp.bfloat16

D_MODEL = 1024
DEPTH = 2
PAGE_SIZE = 128
H_DIFF = 4
QK_HEAD_DIM = 64
V_HEAD_DIM = 128
HEAD_W = 2 * QK_HEAD_DIM
DIFF_WIDTH = H_DIFF * V_HEAD_DIM
A_GROUPS = 4
A_WIDTH = 512
A_GROUP_DIM = 128
CHUNK = 128
LANE = 128
Q_COLS = H_DIFF * HEAD_W
IN_COLS = 3 * Q_COLS + 2 * A_WIDTH
D_FF = 2816
CONV_W = 3
REL_BUCKETS = 32
REL_MAX_EXACT = 16
REL_MAX_DIST = 128
NORM_EPS = 1e-6
SUBLN_EPS = 1e-5
NEG_INF = -1e30
LOG2_E = math.log2(math.e)

V7X_VMEM_LIMIT_BYTES = 56 * 1024 * 1024

TM_IN = 1024
TQ = 256
TK = 256
DENOM_ROWS = 16
SPS = 2
TM_FFN = 512
TS = 256
PITCH = TS // 8 + 8
TF = 256


def _lambda_init(layer):
    return 0.8 - 0.6 * math.exp(-0.3 * layer)


def _bucket_table(dist):
    n = np.maximum(dist, 0)
    nf = np.maximum(n, 1).astype(np.float32)
    large = REL_MAX_EXACT + (np.log(nf / REL_MAX_EXACT) / math.log(REL_MAX_DIST / REL_MAX_EXACT)
                             * (REL_BUCKETS - REL_MAX_EXACT)).astype(np.int32)
    large = np.minimum(large, REL_BUCKETS - 1)
    return np.where(dist >= 0, np.where(n < REL_MAX_EXACT, n, large), -1).astype(np.int32)


def _const_spec(shape):
    nd = len(shape)
    return pl.BlockSpec(shape, lambda *_: (0,) * nd, pipeline_mode=pl.Buffered(1))


def _layer_spec(shape, layer):
    nd = len(shape)
    return pl.BlockSpec((None,) + tuple(shape), lambda *_: (layer,) + (0,) * nd,
                        pipeline_mode=pl.Buffered(1))


def _rms(x, gain, eps):
    y = x * lax.rsqrt(jnp.mean(x * x, axis=-1, keepdims=True) + eps)
    return y * gain


def _nt_dot(a, b):
    return lax.dot_general(a, b, (((1,), (1,)), ((), ())), preferred_element_type=F32)


def _diff_lambda(lamv_ref, lam_init):
    p = lamv_ref[...]
    a = jnp.sum(p[0:1] * p[1:2], axis=-1, keepdims=True)
    b = jnp.sum(p[2:3] * p[3:4], axis=-1, keepdims=True)
    return jnp.exp(a) - jnp.exp(b) + lam_init


def _bias_from_buckets(idx, rb_ref, head, scale):
    far = rb_ref[REL_BUCKETS - 1, head]
    tbl = jnp.full(idx.shape, NEG_INF, F32)
    for b in range(REL_BUCKETS):
        tbl = jnp.where(idx == b, (rb_ref[b, head] - far) * scale, tbl)
    return tbl


def _inproj_kernel(x_ref, g_ref, w_ref, lng_ref, lnb_ref, wsp_ref, bsp_ref, *out_refs,
                   tm, sample, n_aliased):
    out_refs = out_refs[n_aliased:]
    if sample:
        q_ref, kf_ref, vf_ref, oa_ref, vn_ref = out_refs
    else:
        qt_ref, kf_ref, vf_ref, kb_ref, vt_ref, oa_ref = out_refs

    nsub = tm // TS
    a0 = 3 * Q_COLS

    def norm(t):
        return _rms(x_ref[t * TS:(t + 1) * TS, :], g_ref[...], NORM_EPS).astype(BF16)

    def proj(xn, c0, c1):
        return jnp.dot(xn, w_ref[:, c0:c1], preferred_element_type=F32)

    def qkv(t, xn):
        rows = slice(t * TS, (t + 1) * TS)
        q = proj(xn, 0, Q_COLS)
        k = proj(xn, Q_COLS, 2 * Q_COLS)
        v = proj(xn, 2 * Q_COLS, 3 * Q_COLS)
        for h in range(H_DIFF):
            head_rows = pl.ds(t * TS * H_DIFF + h, TS, stride=H_DIFF)
            for d in range(kf_ref.shape[0]):
                kf_ref[d, head_rows, :] = k[:, h * HEAD_W:(h + 1) * HEAD_W]
                vf_ref[d, head_rows, :] = v[:, h * V_HEAD_DIM:(h + 1) * V_HEAD_DIM]
        if sample:
            q_ref[rows, :] = q * (QK_HEAD_DIM ** -0.5)
        else:
            qt_ref[:, rows] = (q * (QK_HEAD_DIM ** -0.5 * LOG2_E)).T.astype(BF16)
            vt_ref[:, rows] = v.T.astype(BF16)
            kb_ref[rows, :] = k.astype(BF16)

    def a_proj(xn):
        return proj(xn, a0, a0 + A_WIDTH), proj(xn, a0 + A_WIDTH, a0 + 2 * A_WIDTH)

    row = lax.broadcasted_iota(jnp.int32, (CHUNK, CHUNK), 0)
    col = lax.broadcasted_iota(jnp.int32, (CHUNK, CHUNK), 1)
    if sample:
        keep = jnp.logical_and((row >> 3) == (col >> 3), (col & 7) <= (row & 7))
    else:
        keep = col <= row
    w_mix = [jnp.where(keep, wsp_ref[g], 0.0).astype(BF16) for g in range(A_GROUPS)]

    def gate(t, u_raw, v_raw):
        u_a = jax.nn.gelu(u_raw)
        v_a = jax.nn.gelu(v_raw)
        xc = v_a - jnp.mean(v_a, axis=-1, keepdims=True)
        vn = xc * lax.rsqrt(jnp.mean(xc * xc, axis=-1, keepdims=True) + NORM_EPS)
        vn = vn * lng_ref[...] + lnb_ref[...]
        if sample:
            vn_ref[t * TS:(t + 1) * TS, :] = vn
        vnb = vn.astype(BF16)
        for g in range(A_GROUPS):
            c0, c1 = g * A_GROUP_DIM, (g + 1) * A_GROUP_DIM
            for c in range(TS // CHUNK):
                r0, r1 = c * CHUNK, (c + 1) * CHUNK
                s = jnp.dot(w_mix[g], vnb[r0:r1, c0:c1], preferred_element_type=F32) + bsp_ref[g]
                oa_ref[t * TS + r0:t * TS + r1, c0:c1] = (u_a[r0:r1, c0:c1] * s).astype(
                    oa_ref.dtype)

    xns = [norm(t) for t in range(nsub)]
    for t in range(nsub):
        a_cur = a_proj(xns[t])
        qkv(t, xns[t])
        gate(t, *a_cur)


def _inproj(x, gain, w_in, ln_g, ln_b, wsp, bsp, kv_all, *, layer, sample):
    rows = x.shape[0]
    tm = TM_IN
    row_spec = lambda w: pl.BlockSpec((tm, w), lambda i: (i, 0))
    assert kv_all is not None or layer == 0
    n_slabs = DEPTH if kv_all is None else 1
    head_row_spec = pl.BlockSpec((n_slabs, tm * H_DIFF, HEAD_W),
                                 lambda i: (layer // n_slabs, i, 0))
    col_spec = pl.BlockSpec((Q_COLS, tm), lambda i: (0, i))
    in_specs = [
        row_spec(D_MODEL),
        _layer_spec((1, D_MODEL), layer),
        _layer_spec((D_MODEL, IN_COLS), layer),
        _layer_spec((1, A_WIDTH), layer),
        _layer_spec((1, A_WIDTH), layer),
        _layer_spec((A_GROUPS, CHUNK, CHUNK), layer),
        _layer_spec((A_GROUPS, CHUNK, A_GROUP_DIM), layer),
    ]
    sds = jax.ShapeDtypeStruct
    kv_shape = sds((DEPTH, rows * H_DIFF, HEAD_W), F32)
    aliased = () if kv_all is None else tuple(kv_all)
    n_in = len(in_specs)
    in_specs += [pl.BlockSpec(memory_space=pl.ANY)] * len(aliased)
    aliases = {n_in + i: 1 + i for i in range(len(aliased))}
    if sample:
        out_shape = [sds((rows, Q_COLS), F32), kv_shape, kv_shape,
                     sds((rows, A_WIDTH), F32), sds((rows, A_WIDTH), F32)]
        out_specs = [row_spec(Q_COLS), head_row_spec, head_row_spec,
                     row_spec(A_WIDTH), row_spec(A_WIDTH)]
    else:
        out_shape = [sds((Q_COLS, rows), BF16), kv_shape, kv_shape,
                     sds((rows, Q_COLS), BF16), sds((Q_COLS, rows), BF16),
                     sds((rows, A_WIDTH), BF16)]
        out_specs = [col_spec, head_row_spec, head_row_spec,
                     row_spec(Q_COLS), col_spec, row_spec(A_WIDTH)]
    return pl.pallas_call(
        functools.partial(_inproj_kernel, tm=tm, sample=sample, n_aliased=len(aliased)),
        grid=(rows // tm,),
        in_specs=in_specs,
        out_specs=out_specs,
        out_shape=out_shape,
        input_output_aliases=aliases,
        compiler_params=pltpu.CompilerParams(
            dimension_semantics=("arbitrary",), vmem_limit_bytes=V7X_VMEM_LIMIT_BYTES),
        name="inproj_sample" if sample else "inproj_prompt",
    )(x, gain, w_in, ln_g, ln_b, wsp, bsp, *aliased)


def _prompt_attn_step(first, qi, rb_ref, lamv_ref, idx_ref, sg_ref, qt_ref, k_ref, vt_ref, o_ref,
                      bias_scr, qw_scr, m_scr, acc_scr, *, lam_init):
    @pl.when(first)
    def _():
        for h in range(H_DIFF):
            for t in range(2):
                bias_scr[h, t] = _bias_from_buckets(idx_ref[t], rb_ref, h, LOG2_E)

    drow = lax.broadcasted_iota(jnp.int32, (HEAD_W, TQ), 0)
    first_map = drow < QK_HEAD_DIM
    for h in range(H_DIFF):
        qh = qt_ref[h * HEAD_W:(h + 1) * HEAD_W, :]
        zero = jnp.zeros_like(qh)
        qw_scr[h] = jnp.concatenate(
            [jnp.where(first_map, qh, zero), jnp.where(first_map, zero, qh)], axis=1)
    m_scr[...] = jnp.full_like(m_scr, NEG_INF)
    acc_scr[...] = jnp.zeros_like(acc_scr)
    ones_rows = jnp.ones((DENOM_ROWS, TK), BF16)

    def key_tiles(tiles):
        units = [(pl.multiple_of(j * TK, TK), kind, h) for j, kind in tiles for h in range(H_DIFF)]

        def scores(u):
            r0, _, h = u
            kj = k_ref[pl.ds(r0, TK), h * HEAD_W:(h + 1) * HEAD_W]
            return jnp.dot(kj, qw_scr[h], preferred_element_type=F32)

        def softmax(u, st):
            _, kind, h = u
            if kind is not None:
                bias = bias_scr[h, kind]
                st = jnp.concatenate([bias, bias], axis=1) + st
            m_old = m_scr[h]
            m_new = jnp.maximum(m_old, jnp.max(st, axis=0, keepdims=True))
            m_scr[h] = m_new
            return jnp.exp2(m_old - m_new), jnp.exp2(st - m_new).astype(BF16)

        def accumulate(u, alpha, pt):
            r0, _, h = u
            vtj = jnp.concatenate([vt_ref[h * HEAD_W:(h + 1) * HEAD_W, pl.ds(r0, TK)],
                                   ones_rows], axis=0)
            acc_scr[h] = alpha * acc_scr[h] + jnp.dot(vtj, pt, preferred_element_type=F32)

        n = len(units)
        st = {0: scores(units[0]), 1: scores(units[1])}
        p = {}
        for i in range(n):
            p[i] = softmax(units[i], st.pop(i))
            if i + 2 < n:
                st[i + 2] = scores(units[i + 2])
            if i >= 1:
                accumulate(units[i - 1], *p.pop(i - 1))
        accumulate(units[n - 1], *p.pop(n - 1))

    n_far = jnp.maximum(qi - 1, 0)

    def far_pair(jj, carry):
        key_tiles([(2 * jj, None), (2 * jj + 1, None)])
        return carry

    lax.fori_loop(0, n_far // 2, far_pair, 0)

    @pl.when(n_far % 2 == 1)
    def _():
        key_tiles([(n_far - 1, None)])

    def finalize():
        lam = _diff_lambda(lamv_ref, lam_init)
        for h in range(H_DIFF):
            acc = acc_scr[h, 0:V_HEAD_DIM, :]
            rl = 1.0 / acc_scr[h, V_HEAD_DIM:V_HEAD_DIM + 1, :]
            ot = acc[:, :TQ] * rl[:, :TQ] - lam * (acc[:, TQ:] * rl[:, TQ:])
            ot = ot * lax.rsqrt(jnp.mean(ot * ot, axis=0, keepdims=True) + SUBLN_EPS)
            ot = ot * sg_ref[...] * (1.0 - lam_init)
            o_ref[:, h * V_HEAD_DIM:(h + 1) * V_HEAD_DIM] = ot.T.astype(o_ref.dtype)

    @pl.when(qi >= 1)
    def _():
        key_tiles([(qi - 1, 1), (qi, 0)])
        finalize()

    @pl.when(qi == 0)
    def _():
        key_tiles([(qi, 0)])
        finalize()


def _sample_page_copies(step, sl, pt_ref, ck_hbm, cv_hbm, kbuf, vbuf, sem, *, layer, n_pages):
    copies = []
    for i in range(SPS):
        for p in range(n_pages):
            page = pt_ref[step * SPS + i, p]
            copies.append(pltpu.make_async_copy(ck_hbm.at[layer, page], kbuf.at[sl, i, p],
                                                sem.at[sl, 0]))
            copies.append(pltpu.make_async_copy(cv_hbm.at[layer, page], vbuf.at[sl, i, p],
                                                sem.at[sl, 1]))
    return copies


def _sample_attn_step(first, slot, rb_ref, lamv_ref, idx_ref, sg_ref, q_ref, kn_ref,
                      vn_ref, o_ref, kbuf, vbuf, s_scr, kpad, vpad, bias_scr,
                      *, n_pages, dec_seq, lam_init):
    rows = 2 * dec_seq
    past = n_pages * PAGE_SIZE

    @pl.when(first)
    def _():
        for h in range(H_DIFF):
            for t in range(2):
                bias_scr[t, h * rows:(h + 1) * rows, :] = _bias_from_buckets(
                    idx_ref[t], rb_ref, h, 1.0)

    lane = lax.broadcasted_iota(jnp.int32, (dec_seq, HEAD_W), 1)
    first_map = lane < QK_HEAD_DIM
    lam = _diff_lambda(lamv_ref, lam_init)

    def page_head(buf, i, p, h):
        return buf[slot, i, p, pl.ds(h, PAGE_SIZE, stride=H_DIFF), :]

    def scores(i):
        def new_rows(ref, h):
            return ref[pl.ds(i * (dec_seq * H_DIFF) + h, dec_seq, stride=H_DIFF), :]

        kpad[i] = jnp.zeros(kpad.shape[1:], F32)
        vpad[i] = jnp.zeros(vpad.shape[1:], F32)
        for h in range(H_DIFF):
            kpad[i, h, 0:dec_seq, :] = new_rows(kn_ref, h)
            vpad[i, h, 0:dec_seq, :] = new_rows(vn_ref, h)

        for h in range(H_DIFF):
            qh = q_ref[i * dec_seq:(i + 1) * dec_seq, h * HEAD_W:(h + 1) * HEAD_W]
            qm = jnp.concatenate([jnp.where(first_map, qh, 0.0), jnp.where(first_map, 0.0, qh)],
                                 axis=0).astype(BF16)
            r0, r1 = h * rows, (h + 1) * rows
            for p in range(0, n_pages, 2):
                k2 = jnp.concatenate([page_head(kbuf, i, p, h), page_head(kbuf, i, p + 1, h)],
                                     axis=0).astype(BF16)
                s_scr[i, r0:r1, p * PAGE_SIZE:(p + 2) * PAGE_SIZE] = _nt_dot(qm, k2)
            s_scr[i, r0:r1, past - PAGE_SIZE:past] += bias_scr[0, r0:r1, :]
            s_scr[i, r0:r1, past:] = (_nt_dot(qm, kpad[i, h].astype(BF16))
                                      + bias_scr[1, r0:r1, :])

    def weighted_values(i):
        s = s_scr[i]
        e = jnp.exp(s - jnp.max(s, axis=-1, keepdims=True))
        pn = e / jnp.sum(e, axis=-1, keepdims=True)
        for h in range(H_DIFF):
            r0 = h * rows
            a = (pn[r0:r0 + dec_seq] - lam * pn[r0 + dec_seq:r0 + rows]).astype(BF16)
            acc = jnp.dot(a[:, past:], vpad[i, h].astype(BF16), preferred_element_type=F32)
            for p in range(n_pages):
                acc = acc + jnp.dot(a[:, p * PAGE_SIZE:(p + 1) * PAGE_SIZE],
                                    page_head(vbuf, i, p, h).astype(BF16),
                                    preferred_element_type=F32)
            o_ref[i * dec_seq:(i + 1) * dec_seq, h * V_HEAD_DIM:(h + 1) * V_HEAD_DIM] = (
                _rms(acc, sg_ref[...], SUBLN_EPS) * (1.0 - lam_init))

    for i in range(SPS):
        scores(i)
    for i in range(SPS):
        weighted_values(i)


def _attention_kernel(pt_ref, rb_ref, lamv_ref,
                      idx_p_ref, sg_p_ref, qt_ref, k_ref, vt_ref,
                      idx_s_ref, sg_s_ref, q_ref, kn_ref, vn_ref, ck_hbm, cv_hbm,
                      op_ref, os_ref,
                      bias_p, qw_scr, m_scr, acc_scr,
                      kbuf, vbuf, sem, s_scr, kpad, vpad, bias_s,
                      *, layer, n_pages, dec_seq, lam_init):
    qi = pl.program_id(1)
    step = pl.program_id(0) * pl.num_programs(1) + qi
    nsteps = pl.num_programs(0) * pl.num_programs(1)
    first = step == 0
    slot = lax.rem(step, 2)
    copies = functools.partial(_sample_page_copies, pt_ref=pt_ref, ck_hbm=ck_hbm, cv_hbm=cv_hbm,
                               kbuf=kbuf, vbuf=vbuf, sem=sem, layer=layer, n_pages=n_pages)

    @pl.when(first)
    def _():
        for cp in copies(0, 0):
            cp.start()

    @pl.when(step + 1 < nsteps)
    def _():
        for cp in copies(step + 1, 1 - slot):
            cp.start()

    _prompt_attn_step(first, qi, rb_ref, lamv_ref, idx_p_ref, sg_p_ref, qt_ref, k_ref, vt_ref,
                      op_ref, bias_p, qw_scr, m_scr, acc_scr, lam_init=lam_init)

    for cp in copies(step, slot):
        cp.wait()
    _sample_attn_step(first, slot, rb_ref, lamv_ref, idx_s_ref, sg_s_ref, q_ref, kn_ref,
                      vn_ref, os_ref, kbuf, vbuf, s_scr, kpad, vpad, bias_s,
                      n_pages=n_pages, dec_seq=dec_seq, lam_init=lam_init)


def _attention(qt, k, vt, sg_p, q_s, kv_s, cache_k, cache_v, page_table, rel_bias, lamv, sg_s,
               *, layer, batch, seq, dec_seq, lam_init):
    nq = seq // TQ
    kk = np.arange(TK)[:, None]
    qq = np.arange(TQ)[None, :]
    idx_p = jnp.asarray(np.stack([_bucket_table(qq - kk), _bucket_table(TQ + qq - kk)]))

    dec_batch, n_pages = page_table.shape
    assert n_pages % 2 == 0 and dec_batch == SPS * batch * nq
    n_pool = cache_k.shape[1]
    page_rows = PAGE_SIZE * H_DIFF
    ck = cache_k.reshape(DEPTH, n_pool, page_rows, HEAD_W)
    cv = cache_v.reshape(DEPTH, n_pool, page_rows, V_HEAD_DIM)
    past_len = n_pages * PAGE_SIZE
    rows = 2 * dec_seq
    i = (np.arange(rows) % dec_seq)[:, None]
    c = np.arange(PAGE_SIZE)[None, :]
    idx_last = _bucket_table(past_len + i - (past_len - PAGE_SIZE + c))
    idx_new = np.where(c < dec_seq, _bucket_table(i - c), -1).astype(np.int32)
    idx_s = jnp.asarray(np.stack([idx_last, idx_new]))
    nkeys = past_len + PAGE_SIZE

    step = lambda b, i: b * nq + i
    const = lambda shape: pl.BlockSpec(shape, lambda b, i, pt: (0,) * len(shape),
                                       pipeline_mode=pl.Buffered(1))
    per_layer = lambda shape: pl.BlockSpec((None,) + shape,
                                           lambda b, i, pt: (layer,) + (0,) * len(shape),
                                           pipeline_mode=pl.Buffered(1))
    srow_spec = pl.BlockSpec((SPS * dec_seq, Q_COLS), lambda b, i, pt: (step(b, i), 0))
    snew_spec = pl.BlockSpec((None, SPS * dec_seq * H_DIFF, HEAD_W),
                             lambda b, i, pt: (layer, step(b, i), 0))
    grid_spec = pltpu.PrefetchScalarGridSpec(
        num_scalar_prefetch=1,
        grid=(batch, nq),
        in_specs=[
            pl.BlockSpec(memory_space=pltpu.SMEM),
            per_layer((4, QK_HEAD_DIM)),
            const((2, TK, TQ)),
            per_layer((V_HEAD_DIM, TQ)),
            pl.BlockSpec((Q_COLS, TQ), lambda b, i, pt: (0, step(b, i))),
            pl.BlockSpec((seq, Q_COLS), lambda b, i, pt: (b, 0)),
            pl.BlockSpec((Q_COLS, seq), lambda b, i, pt: (0, b)),
            const((2, rows, PAGE_SIZE)),
            per_layer((1, V_HEAD_DIM)),
            srow_spec, snew_spec, snew_spec,
            pl.BlockSpec(memory_space=pl.ANY), pl.BlockSpec(memory_space=pl.ANY),
        ],
        out_specs=[pl.BlockSpec((TQ, DIFF_WIDTH), lambda b, i, pt: (step(b, i), 0)), srow_spec],
        scratch_shapes=[
            pltpu.VMEM((H_DIFF, 2, TK, TQ), F32),
            pltpu.VMEM((H_DIFF, HEAD_W, 2 * TQ), BF16),
            pltpu.VMEM((H_DIFF, 1, 2 * TQ), F32),
            pltpu.VMEM((H_DIFF, V_HEAD_DIM + DENOM_ROWS, 2 * TQ), F32),
            pltpu.VMEM((2, SPS, n_pages, page_rows, HEAD_W), F32),
            pltpu.VMEM((2, SPS, n_pages, page_rows, V_HEAD_DIM), F32),
            pltpu.SemaphoreType.DMA((2, 2)),
            pltpu.VMEM((SPS, H_DIFF * rows, nkeys), F32),
            pltpu.VMEM((SPS, H_DIFF, PAGE_SIZE, HEAD_W), F32),
            pltpu.VMEM((SPS, H_DIFF, PAGE_SIZE, V_HEAD_DIM), F32),
            pltpu.VMEM((2, H_DIFF * rows, PAGE_SIZE), F32),
        ],
    )
    return pl.pallas_call(
        functools.partial(_attention_kernel, layer=layer, n_pages=n_pages, dec_seq=dec_seq,
                          lam_init=lam_init),
        grid_spec=grid_spec,
        out_shape=[jax.ShapeDtypeStruct((batch * seq, DIFF_WIDTH), BF16),
                   jax.ShapeDtypeStruct((dec_batch * dec_seq, DIFF_WIDTH), F32)],
        compiler_params=pltpu.CompilerParams(
            dimension_semantics=("arbitrary", "arbitrary"),
            vmem_limit_bytes=V7X_VMEM_LIMIT_BYTES),
        name="attention",
    )(page_table, rel_bias, lamv, idx_p, sg_p, qt, k, vt, idx_s, sg_s, q_s, kv_s[0], kv_s[1],
      ck, cv)


def _mix_and_prenorm(x_ref, ob_ref, oa_ref, wo_ref, g_post_ref, g_pre_ref):
    mix = jnp.dot(ob_ref[...].astype(BF16), wo_ref[0:DIFF_WIDTH, :], preferred_element_type=F32)
    mix = mix + jnp.dot(oa_ref[...].astype(BF16), wo_ref[DIFF_WIDTH:, :],
                        preferred_element_type=F32)
    x1 = x_ref[...] + _rms(mix, g_post_ref[...], NORM_EPS)
    return x1, _rms(x1, g_pre_ref[...], NORM_EPS)


def _conv_gate(cw, cb, h, h1, h2):
    return cb + cw[0:1] * h2 + cw[1:2] * h1 + cw[2:3] * h


def _ffn_prompt_kernel(x_ref, ob_ref, oa_ref, wo_ref, g_post_ref, g_pre_ref, g_out_ref,
                       wu_ref, cw_ref, cb_ref, wd_ref, y_ref, st_ref, perm, carry, act,
                       *, tm):
    nsub = tm // TS
    nv = TS // 8
    nj = D_FF // TF
    ncb = D_MODEL // LANE

    @pl.when(pl.program_id(1) == 0)
    def _():
        carry[...] = jnp.zeros_like(carry)

    def pre(t):
        rows = slice(t * TS, (t + 1) * TS)
        x1, xn2 = _mix_and_prenorm(x_ref.at[rows], ob_ref.at[rows], oa_ref.at[rows], wo_ref,
                                   g_post_ref, g_pre_ref)
        y_ref[rows, :] = x1
        for c in range(ncb):
            for s in range(8):
                perm[t, c, s * PITCH:s * PITCH + nv, :] = xn2[s * nv:(s + 1) * nv,
                                                              c * LANE:(c + 1) * LANE]
        return jnp.concatenate(
            [jnp.concatenate([perm[t, c, pl.ds(v, 8, stride=PITCH), :] for c in range(ncb)],
                             axis=1)
             for v in range(nv)], axis=0).astype(BF16)

    first_row = lax.broadcasted_iota(jnp.int32, (8, TF), 0) == 0

    def conv_ffn(t, xp):
        def up(j):
            return [jnp.dot(xp, wu_ref[:, half * D_FF + j * TF:half * D_FF + (j + 1) * TF],
                            preferred_element_type=F32) for half in range(2)]

        h_next = up(0)
        for j in range(nj):
            h_cur = h_next
            if j + 1 < nj:
                h_next = up(j + 1)
            conv = []
            for half in range(2):
                c0 = half * D_FF + j * TF
                c1 = c0 + TF
                h = h_cur[half]
                prev = carry[:, c0:c1]
                head = [jnp.where(first_row, pltpu.roll(prev[8 * i:8 * i + 8], 1, 0),
                                  pltpu.roll(h[TS - 16 + 8 * i:TS - 8 + 8 * i], 1, 0))
                        for i in range(2)]
                h1 = jnp.concatenate([head[1], h[:TS - 8]], axis=0)
                h2 = jnp.concatenate([head[0], head[1], h[:TS - 16]], axis=0)
                carry[:, c0:c1] = h[TS - 16:]
                conv.append(_conv_gate(cw_ref[:, c0:c1], cb_ref[:, c0:c1], h, h1, h2))
            act[t, :, j * TF:(j + 1) * TF] = (jax.nn.silu(conv[0]) * conv[1]).astype(BF16)

    def post(t):
        rows = slice(t * TS, (t + 1) * TS)
        fp = jnp.dot(act[t], wd_ref[...], preferred_element_type=F32)
        for v in range(nv):
            for c in range(ncb):
                perm[t, c, pl.ds(v, 8, stride=PITCH), :] = fp[8 * v:8 * v + 8,
                                                              c * LANE:(c + 1) * LANE]
        f = jnp.concatenate(
            [jnp.concatenate([perm[t, c, s * PITCH:s * PITCH + nv, :] for s in range(8)], axis=0)
             for c in range(ncb)], axis=1)
        y_ref[rows, :] = y_ref[rows, :] + _rms(f, g_out_ref[...], NORM_EPS)

    xps = [pre(t) for t in range(nsub)]
    for t in range(nsub):
        conv_ffn(t, xps[t])
        post(t)
    st_ref[0, 0:1, :] = carry[7:8, :]
    st_ref[0, 1:2, :] = carry[15:16, :]


def _ffn_prompt(x, o_b, o_a, w_out, g_post, g_pre, g_out, w_up, conv_w, conv_b, w_down,
                *, layer, batch, seq):
    tm = TM_FFN
    nt = seq // tm
    row_spec = lambda w: pl.BlockSpec((tm, w), lambda b, i: (b * nt + i, 0))
    per_layer = functools.partial(_layer_spec, layer=layer)
    return pl.pallas_call(
        functools.partial(_ffn_prompt_kernel, tm=tm),
        grid=(batch, nt),
        in_specs=[
            row_spec(D_MODEL), row_spec(DIFF_WIDTH), row_spec(A_WIDTH),
            per_layer((D_MODEL, D_MODEL)),
            per_layer((1, D_MODEL)), per_layer((1, D_MODEL)), per_layer((1, D_MODEL)),
            per_layer((D_MODEL, 2 * D_FF)),
            per_layer((CONV_W, 2 * D_FF)),
            per_layer((1, 2 * D_FF)),
            per_layer((D_FF, D_MODEL)),
        ],
        out_specs=[row_spec(D_MODEL),
                   pl.BlockSpec((1, CONV_W - 1, 2 * D_FF), lambda b, i: (b, 0, 0))],
        out_shape=[jax.ShapeDtypeStruct((batch * seq, D_MODEL), F32),
                   jax.ShapeDtypeStruct((batch, CONV_W - 1, 2 * D_FF), F32)],
        scratch_shapes=[pltpu.VMEM((tm // TS, D_MODEL // LANE, 8 * PITCH, LANE), F32),
                        pltpu.VMEM((16, 2 * D_FF), F32),
                        pltpu.VMEM((tm // TS, TS, D_FF), BF16)],
        compiler_params=pltpu.CompilerParams(
            dimension_semantics=("arbitrary", "arbitrary"),
            vmem_limit_bytes=V7X_VMEM_LIMIT_BYTES),
        name="ffn_prompt",
    )(x, o_b, o_a, w_out, g_post, g_pre, g_out, w_up, conv_w, conv_b, w_down)


def _ffn_sample_kernel(x_hbm, ob_hbm, oa_hbm, wo_ref, g_post_ref, g_pre_ref, g_out_ref,
                       wug_ref, wuu_ref, cwg_ref, cwu_ref, cbg_ref, cbu_ref, wd_ref,
                       stg_ref, stu_ref, y_hbm, sog_ref, sou_ref,
                       x_buf, ob_buf, oa_buf, sem, x1_scr, xn_scr, acc_scr,
                       *, dec_batch, dec_seq):
    j = pl.program_id(0)
    rows = dec_batch * dec_seq

    def position_copies(pairs, sem_ref, to_hbm=False):
        copies = []
        for hbm, buf in pairs:
            for t in range(dec_seq):
                ends = (hbm.at[:, t, :], buf.at[t * dec_batch:(t + 1) * dec_batch, :])
                copies.append(pltpu.make_async_copy(*(ends[::-1] if to_hbm else ends), sem_ref))
        return copies

    @pl.when(j == 0)
    def _():
        loads = position_copies([(x_hbm, x_buf), (ob_hbm, ob_buf), (oa_hbm, oa_buf)], sem.at[0])
        for cp in loads:
            cp.start()
        for cp in loads:
            cp.wait()
        x1, xn2 = _mix_and_prenorm(x_buf, ob_buf, oa_buf, wo_ref, g_post_ref, g_pre_ref)
        x1_scr[...] = x1
        xn_scr[...] = xn2.astype(BF16)
        acc_scr[...] = jnp.zeros_like(acc_scr)

    xn2 = xn_scr[...]
    conv = []
    for wu_ref, cw_ref, cb_ref, st_ref, so_ref in ((wug_ref, cwg_ref, cbg_ref, stg_ref, sog_ref),
                                                   (wuu_ref, cwu_ref, cbu_ref, stu_ref, sou_ref)):
        h = jnp.dot(xn2, wu_ref[...], preferred_element_type=F32)
        s0 = st_ref[0]
        s1 = st_ref[1]
        h1 = jnp.concatenate([s1, h[:rows - dec_batch]], axis=0)
        h2 = jnp.concatenate([s0, s1, h[:rows - 2 * dec_batch]], axis=0)
        conv.append(_conv_gate(cw_ref[...], cb_ref[...], h, h1, h2))
        so_ref[0] = h[rows - 2 * dec_batch:rows - dec_batch]
        so_ref[1] = h[rows - dec_batch:]
    act = (jax.nn.silu(conv[0]) * conv[1]).astype(BF16)
    acc_scr[...] += jnp.dot(act, wd_ref[...], preferred_element_type=F32)

    @pl.when(j == pl.num_programs(0) - 1)
    def _():
        x_buf[...] = x1_scr[...] + _rms(acc_scr[...], g_out_ref[...], NORM_EPS)
        stores = position_copies([(y_hbm, x_buf)], sem.at[1], to_hbm=True)
        for cp in stores:
            cp.start()
        for cp in stores:
            cp.wait()


def _ffn_sample(x, o_b, o_a, w_out, g_post, g_pre, g_out, w_up, conv_w, conv_b, w_down, state,
                *, layer, dec_batch, dec_seq):
    rows = dec_batch * dec_seq
    nj = D_FF // TF
    ns = CONV_W - 1

    def col_tile(shape, ax, first):
        return pl.BlockSpec((None,) + shape,
                            lambda j: (layer,) + tuple(first + j if a == ax else 0
                                                       for a in range(len(shape))))

    gate = functools.partial(col_tile, first=0)
    up = functools.partial(col_tile, first=nj)
    per_layer = functools.partial(_layer_spec, layer=layer)
    state_out = pl.BlockSpec((ns, dec_batch, TF), lambda j: (0, 0, j))
    any_spec = pl.BlockSpec(memory_space=pl.ANY)
    by_position = lambda a: a.reshape(dec_batch, dec_seq, a.shape[-1])
    sds = jax.ShapeDtypeStruct
    y, st_gate, st_up = pl.pallas_call(
        functools.partial(_ffn_sample_kernel, dec_batch=dec_batch, dec_seq=dec_seq),
        grid=(nj,),
        in_specs=[
            any_spec, any_spec, any_spec,
            per_layer((D_MODEL, D_MODEL)),
            per_layer((1, D_MODEL)), per_layer((1, D_MODEL)), per_layer((1, D_MODEL)),
            gate((D_MODEL, TF), 1), up((D_MODEL, TF), 1),
            gate((CONV_W, TF), 1), up((CONV_W, TF), 1),
            gate((1, TF), 1), up((1, TF), 1),
            gate((TF, D_MODEL), 0),
            gate((ns, dec_batch, TF), 2), up((ns, dec_batch, TF), 2),
        ],
        out_specs=[any_spec, state_out, state_out],
        out_shape=[sds((dec_batch, dec_seq, D_MODEL), F32), sds((ns, dec_batch, D_FF), F32),
                   sds((ns, dec_batch, D_FF), F32)],
        scratch_shapes=[pltpu.VMEM((rows, D_MODEL), F32), pltpu.VMEM((rows, DIFF_WIDTH), F32),
                        pltpu.VMEM((rows, A_WIDTH), F32),
                        pltpu.SemaphoreType.DMA((2,)),
                        pltpu.VMEM((rows, D_MODEL), F32), pltpu.VMEM((rows, D_MODEL), BF16),
                        pltpu.VMEM((rows, D_MODEL), F32)],
        compiler_params=pltpu.CompilerParams(
            dimension_semantics=("arbitrary",), vmem_limit_bytes=V7X_VMEM_LIMIT_BYTES),
        name="ffn_sample",
    )(by_position(x), by_position(o_b), by_position(o_a), w_out, g_post, g_pre, g_out,
      w_up, w_up, conv_w, conv_w, conv_b, conv_b, w_down, state, state)
    return y.reshape(rows, D_MODEL), jnp.concatenate([st_gate, st_up], axis=-1)


def kernel(x_prompt, x_sample, cache_k, cache_v, state_conv, page_table, rel_bias, norm_mix_pre, norm_mix_post, norm_ffn_pre, norm_ffn_post, w_in, lambda_q1, lambda_k1, lambda_q2, lambda_k2, subln_gain, sgu_ln_gain, sgu_ln_bias, w_spatial, b_spatial, w_out, w_up, conv_w, conv_b, w_down):
    batch, seq, _ = x_prompt.shape
    dec_batch, dec_seq, _ = x_sample.shape
    assert seq % TQ == 0 and seq % TM_FFN == 0 and (batch * seq) % TM_IN == 0
    assert (dec_batch * dec_seq) % TM_IN == 0 and CHUNK % dec_seq == 0 and TQ == TK

    y_p = x_prompt.reshape(batch * seq, D_MODEL)
    y_s = x_sample.reshape(dec_batch * dec_seq, D_MODEL)
    reps = CHUNK // dec_seq

    rows3 = lambda a: a.reshape(DEPTH, 1, -1)
    w_in_b, w_out_b = w_in.astype(BF16), w_out.astype(BF16)
    w_up_b, w_down_b = w_up.astype(BF16), w_down.astype(BF16)
    lamv = jnp.stack([lambda_q1, lambda_k1, lambda_q2, lambda_k2], axis=1)
    sg_rows = jnp.broadcast_to(subln_gain[:, :, None], (DEPTH, V_HEAD_DIM, TQ))
    ffn_args = (w_out_b, rows3(norm_mix_post), rows3(norm_ffn_pre), rows3(norm_ffn_post),
                w_up_b, conv_w, rows3(conv_b), w_down_b)
    in_args = (rows3(norm_mix_pre), w_in_b, rows3(sgu_ln_gain), rows3(sgu_ln_bias))
    gate_shape = (DEPTH, A_GROUPS, CHUNK, A_GROUP_DIM)
    bsp_p = jnp.broadcast_to(b_spatial[:, :, :, None], gate_shape)
    wsp_s = jnp.tile(w_spatial[:, :, :dec_seq, :dec_seq], (1, 1, reps, reps))
    bsp_s = jnp.broadcast_to(jnp.tile(b_spatial[:, :, :dec_seq], (1, 1, reps))[:, :, :, None],
                             gate_shape)
    state = state_conv.transpose(0, 2, 1, 3)

    cp, csm, svs = [], [], []
    kv_p = kv_s = None
    for l in range(DEPTH):
        lam_init = _lambda_init(l)

        qt, kf, vf, kb, vt, o_a = _inproj(y_p, *in_args, w_spatial, bsp_p, kv_p,
                                          layer=l, sample=False)
        kv_p = (kf, vf)
        q_s, kf, vf, o_a_s, vn = _inproj(y_s, *in_args, wsp_s, bsp_s, kv_s, layer=l, sample=True)
        kv_s = (kf, vf)
        o_b, o_b_s = _attention(qt, kb, vt, sg_rows, q_s, kv_s, cache_k, cache_v, page_table,
                                rel_bias, lamv, rows3(subln_gain), layer=l, batch=batch, seq=seq,
                                dec_seq=dec_seq, lam_init=lam_init)
        y_p, conv_p = _ffn_prompt(y_p, o_b, o_a, *ffn_args, layer=l, batch=batch, seq=seq)
        cp.append(conv_p)
        y_s, conv_s = _ffn_sample(y_s, o_b_s, o_a_s, *ffn_args, state,
                                  layer=l, dec_batch=dec_batch, dec_seq=dec_seq)
        csm.append(conv_s.transpose(1, 0, 2))
        svs.append(vn.reshape(dec_batch, dec_seq, A_WIDTH))

    kv_shape_p = (DEPTH, batch, seq, H_DIFF, HEAD_W)
    kv_shape_s = (DEPTH, dec_batch, dec_seq, H_DIFF, HEAD_W)
    return (y_p.reshape(batch, seq, D_MODEL), y_s.reshape(dec_batch, dec_seq, D_MODEL),
            kv_p[0].reshape(kv_shape_p), kv_p[1].reshape(kv_shape_p), jnp.stack(cp),
            kv_s[0].reshape(kv_shape_s), kv_s[1].reshape(kv_shape_s),
            jnp.stack(csm), jnp.stack(svs))
```
